```python
import math, functools
import jax, jax.numpy as jnp
from jax import lax
import numpy as np

D_MODEL = 1024
BATCH = 1
SEQ = 16384
DEPTH = 2
DEC_BATCH = 32
DEC_SEQ = 1
PAST_LEN = 16384
PAGE_SIZE = 128

N_HEADS = 8
HEAD_DIM = 64
N_KV = 2
GROUP = N_HEADS // N_KV
D_NSA = N_HEADS * HEAD_DIM
CMP_BLOCK = 32
CMP_HIDDEN = 256
SEL_BLOCK = 64
SEL_TOPK = 16
WINDOW = 512
Q_BLOCK = 128
N_SG = 8
SG_DIM = 64
D_SG = N_SG * SG_DIM
CHUNK = 128
D_FF = 2816
N_EXPERTS = 8
TOP_K = 2
D_FF_EXPERT = 1408
N_DENSE = (DEPTH + 1) // 2
N_MOE = DEPTH // 2
KV_COLS = 6 * N_KV * HEAD_DIM
OFF_KV = D_NSA
OFF_G = OFF_KV + KV_COLS
OFF_U = OFF_G + 3 * N_HEADS
OFF_V = OFF_U + D_SG
OFF_GA = OFF_V + D_SG
OFF_GB = OFF_GA + D_MODEL
IN_COLS = OFF_GB + D_MODEL
IN_SPLITS = (OFF_KV, OFF_G, OFF_U, OFF_V, OFF_GA, OFF_GB)
EPS = 1e-6
NEG = -1e30
FORCED = 1e9

kernel_name = 'nsa_sgu_hybrid_decode_step'


def rmsnorm(x, g):
    xf = x.astype(jnp.float32)
    y = xf * lax.rsqrt(jnp.mean(xf * xf, axis=-1, keepdims=True) + EPS)
    return (y * g.astype(jnp.float32)).astype(x.dtype)


def layernorm(x, g, b):
    xf = x.astype(jnp.float32)
    mu = jnp.mean(xf, axis=-1, keepdims=True)
    var = jnp.mean(jnp.square(xf - mu), axis=-1, keepdims=True)
    y = (xf - mu) * lax.rsqrt(var + EPS) * g.astype(jnp.float32) + b.astype(jnp.float32)
    return y.astype(x.dtype)


def masked_softmax(s, mask, axes):
    s = jnp.where(mask, s.astype(jnp.float32), NEG)
    m = jnp.max(s, axis=axes, keepdims=True)
    p = jnp.where(mask, jnp.exp(s - m), 0.0)
    return p / jnp.maximum(jnp.sum(p, axis=axes, keepdims=True), 1e-30)


def alibi_slopes():
    return jnp.exp2(-8.0 * jnp.arange(1, N_HEADS + 1, dtype=jnp.float32) / N_HEADS)


def adaln(c, w_ada, b_ada):
    mod = (jax.nn.silu(c) @ w_ada + b_ada)[:, None, :]
    return jnp.split(mod, 6, axis=-1)


def compress(rows, pe, w1, w2):
    B, L, G, hd = rows.shape
    nc = L // CMP_BLOCK
    blk = rows.reshape(B, nc, CMP_BLOCK, G, hd) + pe[None, None, :, None, :]
    blk = blk.transpose(0, 1, 3, 2, 4).reshape(B, nc, G, CMP_BLOCK * hd)
    return jax.nn.silu(blk @ w1) @ w2


def to_sel_blocks(rows):
    B, L, G, hd = rows.shape
    return rows.reshape(B, L // SEL_BLOCK, SEL_BLOCK, G, hd).transpose(0, 3, 1, 2, 4)


def nsa_query_block(q, q_pos, kc, vc, ks_blk, vs_blk, kw, vw, w_pos):
    B, Lq, H, hd = q.shape
    f32 = jnp.float32
    qg = q.reshape(B, Lq, N_KV, GROUP, hd).astype(f32) * (hd ** -0.5)
    slopes = alibi_slopes().reshape(N_KV, GROUP)
    nc = kc.shape[1]
    c_end = jnp.arange(nc, dtype=jnp.int32) * CMP_BLOCK + (CMP_BLOCK - 1)
    dist_c = q_pos[:, None] - c_end[None, :]
    s_c = jnp.einsum('bqgrd,bcgd->bgrqc', qg, kc.astype(f32)) - slopes[None, :, :, None, None] * dist_c.astype(f32)
    p_c = masked_softmax(s_c, (dist_c >= 0)[None, None, None], -1)
    o_cmp = jnp.einsum('bgrqc,bcgd->bqgrd', p_c, vc.astype(f32))
    ns = ks_blk.shape[2]
    imp = jnp.sum(p_c, axis=2).reshape(B, N_KV, Lq, ns, SEL_BLOCK // CMP_BLOCK).sum(-1)
    blk = jnp.arange(ns, dtype=jnp.int32)[None, :]
    cur = (q_pos // SEL_BLOCK)[:, None]
    forced = (blk == 0) | (blk == cur) | (blk == cur - 1)
    imp = jnp.where(forced, FORCED, imp)
    imp = jnp.where(blk <= cur, imp, NEG)
    top_s, idx = lax.top_k(imp, min(SEL_TOPK, ns))
    sel_ok = top_s > 0.5 * NEG
    b_i = jnp.arange(B)[:, None, None, None]
    g_i = jnp.arange(N_KV)[None, :, None, None]
    ks = ks_blk[b_i, g_i, idx].astype(f32)
    vs = vs_blk[b_i, g_i, idx].astype(f32)
    s_pos = idx[..., None] * SEL_BLOCK + jnp.arange(SEL_BLOCK, dtype=jnp.int32)
    dist_s = q_pos[None, None, :, None, None] - s_pos
    mask_s = ((dist_s >= 0) & sel_ok[..., None])[:, :, None]
    s_s = jnp.einsum('bqgrd,bgqksd->bgrqks', qg, ks) - slopes[None, :, :, None, None, None] * dist_s[:, :, None].astype(f32)
    p_s = masked_softmax(s_s, mask_s, (-2, -1))
    o_sel = jnp.einsum('bgrqks,bgqksd->bqgrd', p_s, vs)
    dist_w = q_pos[:, None] - w_pos[None, :]
    mask_w = (dist_w >= 0) & (dist_w < WINDOW) & (w_pos[None, :] >= 0)
    s_w = jnp.einsum('bqgrd,bkgd->bgrqk', qg, kw.astype(f32)) - slopes[None, :, :, None, None] * dist_w.astype(f32)
    p_w = masked_softmax(s_w, mask_w[None, None, None], -1)
    o_win = jnp.einsum('bgrqk,bkgd->bqgrd', p_w, vw.astype(f32))
    return jnp.stack([o_cmp.reshape(B, Lq, H, hd), o_sel.reshape(B, Lq, H, hd), o_win.reshape(B, Lq, H, hd)], axis=2)


def nsa_prompt(q, kv_rows, win_rows, lp):
    B, L = q.shape[:2]
    kc = compress(kv_rows[:, :, 0], lp['cmp_pe'][0], lp['cmp_w1'][0], lp['cmp_w2'][0])
    vc = compress(kv_rows[:, :, 1], lp['cmp_pe'][1], lp['cmp_w1'][1], lp['cmp_w2'][1])
    ks_blk = to_sel_blocks(kv_rows[:, :, 2])
    vs_blk = to_sel_blocks(kv_rows[:, :, 3])
    pad = ((0, 0), (WINDOW, 0), (0, 0), (0, 0))
    kw_pad = jnp.pad(win_rows[:, :, 0], pad)
    vw_pad = jnp.pad(win_rows[:, :, 1], pad)

    def body(i):
        q0 = i * Q_BLOCK
        qb = lax.dynamic_slice_in_dim(q, q0, Q_BLOCK, axis=1)
        q_pos = q0 + jnp.arange(Q_BLOCK, dtype=jnp.int32)
        kwb = lax.dynamic_slice_in_dim(kw_pad, q0, Q_BLOCK + WINDOW, axis=1)
        vwb = lax.dynamic_slice_in_dim(vw_pad, q0, Q_BLOCK + WINDOW, axis=1)
        w_pos = q0 - WINDOW + jnp.arange(Q_BLOCK + WINDOW, dtype=jnp.int32)
        return nsa_query_block(qb, q_pos, kc, vc, ks_blk, vs_blk, kwb, vwb, w_pos)

    out = lax.map(body, jnp.arange(L // Q_BLOCK, dtype=jnp.int32))
    return out.transpose(1, 0, 2, 3, 4, 5).reshape(B, L, 3, N_HEADS, HEAD_DIM)


def nsa_sample(q, kv_rows, win_rows, past_kv, win_buf, lp):
    B, L = q.shape[:2]
    P = past_kv.shape[1]
    rows = jnp.concatenate([past_kv, kv_rows.astype(past_kv.dtype)], axis=1)
    l_tot = P + L
    l_pad = -(-l_tot // SEL_BLOCK) * SEL_BLOCK
    rows = jnp.pad(rows, ((0, 0), (0, l_pad - l_tot), (0, 0), (0, 0), (0, 0)))
    kc = compress(rows[:, :, 0], lp['cmp_pe'][0], lp['cmp_w1'][0], lp['cmp_w2'][0])
    vc = compress(rows[:, :, 1], lp['cmp_pe'][1], lp['cmp_w1'][1], lp['cmp_w2'][1])
    ks_blk = to_sel_blocks(rows[:, :, 2])
    vs_blk = to_sel_blocks(rows[:, :, 3])
    wall = jnp.concatenate([win_buf, win_rows.astype(win_buf.dtype)], axis=1)
    w_pos = P - win_buf.shape[1] + jnp.arange(wall.shape[1], dtype=jnp.int32)
    q_pos = P + jnp.arange(L, dtype=jnp.int32)
    o3 = nsa_query_block(q, q_pos, kc, vc, ks_blk, vs_blk, wall[:, :, 0], wall[:, :, 1], w_pos)
    return o3, wall


def spatial_mix(v, w_s, b_s):
    B, L = v.shape[:2]
    lp_ = -(-L // CHUNK) * CHUNK
    vp = jnp.pad(v, ((0, 0), (0, lp_ - L), (0, 0), (0, 0))).reshape(B, lp_ // CHUNK, CHUNK, N_SG, SG_DIM)
    tril = jnp.tril(jnp.ones((CHUNK, CHUNK), dtype=bool))
    w = jnp.where(tril[None], w_s, 0.0).astype(v.dtype)
    out = jnp.einsum('gts,bcsgd->bctgd', w, vp) + b_s.T.astype(v.dtype)[None, None, :, :, None]
    return out.reshape(B, lp_, N_SG, SG_DIM)[:, :L]


def swiglu(h, w_gu, w_down):
    a, b = jnp.split(h @ w_gu, 2, axis=-1)
    return (jax.nn.silu(a) * b) @ w_down


def moe_ffn(h, router_w, router_b, w_gu, w_down):
    logits = (h @ router_w).astype(jnp.float32) + router_b.astype(jnp.float32)
    top_v, top_i = lax.top_k(logits, TOP_K)
    wts = jax.nn.softmax(top_v, axis=-1)
    combine = jnp.sum(jax.nn.one_hot(top_i, N_EXPERTS, dtype=jnp.float32) * wts[..., None], axis=-2)
    y = jnp.zeros_like(h)
    for e in range(N_EXPERTS):
        y = y + combine[..., e:e + 1].astype(h.dtype) * swiglu(h, w_gu[e], w_down[e])
    return y


def trunk_layer(x, c, lp, ffn, past_kv, win_buf):
    B, L = x.shape[:2]
    sh1, sc1, g1, sh2, sc2, g2 = adaln(c, lp['w_ada'], lp['b_ada'])
    h = rmsnorm(x, lp['norm_mix']) * (1.0 + sc1) + sh1
    z = h @ lp['w_in']
    q, kv, ng, u, v, ga, gb = jnp.split(z, IN_SPLITS, axis=-1)
    q = q.reshape(B, L, N_HEADS, HEAD_DIM)
    kv = kv.reshape(B, L, 6, N_KV, HEAD_DIM)
    kv_rows, win_rows = kv[:, :, :4], kv[:, :, 4:]
    if past_kv is None:
        o3 = nsa_prompt(q, kv_rows, win_rows, lp)
        win_state = win_rows[:, -min(WINDOW, L):]
    else:
        o3, wall = nsa_sample(q, kv_rows, win_rows, past_kv, win_buf, lp)
        win_state = wall[:, -min(WINDOW, wall.shape[1]):]
    gates = jax.nn.sigmoid(ng.astype(jnp.float32)).reshape(B, L, 3, N_HEADS, 1)
    o_nsa = jnp.sum(o3 * gates, axis=2).astype(x.dtype).reshape(B, L, D_NSA)
    u = jax.nn.gelu(u)
    v = layernorm(jax.nn.gelu(v), lp['sg_norm_g'], lp['sg_norm_b'])
    v_mix = spatial_mix(v.reshape(B, L, N_SG, SG_DIM), lp['sg_w'], lp['sg_b']).reshape(B, L, D_SG)
    o_sg = u * v_mix
    merged = jax.nn.sigmoid(ga) * (o_nsa @ lp['w_branch_nsa']) + jax.nn.sigmoid(gb) * (o_sg @ lp['w_branch_sg'])
    x = x + g1 * (merged @ lp['w_out'])
    h2 = rmsnorm(x, lp['norm_ffn']) * (1.0 + sc2) + sh2
    x = x + g2 * ffn(h2)
    return x, kv_rows, win_state, v


def setup_inputs(seed: int = 0) -> dict:
    key = jax.random.key(seed)
    ks = jax.random.split(key, 32)

    def nrm(k, shape, scale):
        return jax.random.normal(k, shape, jnp.float32) * scale

    n_pages = PAST_LEN // PAGE_SIZE
    n_used = DEC_BATCH * n_pages
    n_phys = n_used + max(1, n_used // 4)
    page_table = jax.random.permutation(ks[0], n_phys)[:n_used].astype(jnp.int32).reshape(DEC_BATCH, n_pages)
    w_buf = min(WINDOW, PAST_LEN)
    return {
        'x_prompt': nrm(ks[1], (BATCH, SEQ, D_MODEL), 1.0),
        'x_sample': nrm(ks[2], (DEC_BATCH, DEC_SEQ, D_MODEL), 1.0),
        'cache_kv': nrm(ks[3], (DEPTH, n_phys, PAGE_SIZE, 4, N_KV, HEAD_DIM), 1.0),
        'state_win': nrm(ks[4], (DEPTH, DEC_BATCH, w_buf, 2, N_KV, HEAD_DIM), 1.0),
        'page_table': page_table,
        'c_prompt': nrm(ks[5], (BATCH, D_MODEL), 1.0),
        'c_sample': nrm(ks[6], (DEC_BATCH, D_MODEL), 1.0),
        'norm_mix_g': 1.0 + nrm(ks[7], (DEPTH, D_MODEL), 0.02),
        'norm_ffn_g': 1.0 + nrm(ks[8], (DEPTH, D_MODEL), 0.02),
        'norm_final_g': 1.0 + nrm(ks[9], (D_MODEL,), 0.02),
        'w_ada': nrm(ks[10], (DEPTH, D_MODEL, 6 * D_MODEL), 0.5 * D_MODEL ** -0.5),
        'b_ada': nrm(ks[11], (DEPTH, 6 * D_MODEL), 0.02),
        'w_in': nrm(ks[12], (DEPTH, D_MODEL, IN_COLS), D_MODEL ** -0.5),
        'cmp_pe': nrm(ks[13], (DEPTH, 2, CMP_BLOCK, HEAD_DIM), 0.02),
        'cmp_w1': nrm(ks[14], (DEPTH, 2, CMP_BLOCK * HEAD_DIM, CMP_HIDDEN), (CMP_BLOCK * HEAD_DIM) ** -0.5),
        'cmp_w2': nrm(ks[15], (DEPTH, 2, CMP_HIDDEN, HEAD_DIM), CMP_HIDDEN ** -0.5),
        'sg_norm_g': 1.0 + nrm(ks[16], (DEPTH, D_SG), 0.02),
        'sg_norm_b': nrm(ks[17], (DEPTH, D_SG), 0.02),
        'sg_w': nrm(ks[18], (DEPTH, N_SG, CHUNK, CHUNK), CHUNK ** -0.5),
        'sg_b': 1.0 + nrm(ks[19], (DEPTH, N_SG, CHUNK), 0.02),
        'w_branch_nsa': nrm(ks[20], (DEPTH, D_NSA, D_MODEL), D_NSA ** -0.5),
        'w_branch_sg': nrm(ks[21], (DEPTH, D_SG, D_MODEL), D_SG ** -0.5),
        'w_out': nrm(ks[22], (DEPTH, D_MODEL, D_MODEL), D_MODEL ** -0.5),
        'ffn_w_gu': nrm(ks[23], (N_DENSE, D_MODEL, 2 * D_FF), D_MODEL ** -0.5),
        'ffn_w_down': nrm(ks[24], (N_DENSE, D_FF, D_MODEL), D_FF ** -0.5),
        'router_w': nrm(ks[25], (N_MOE, D_MODEL, N_EXPERTS), D_MODEL ** -0.5),
        'router_b': nrm(ks[26], (N_MOE, N_EXPERTS), 0.01),
        'moe_w_gu': nrm(ks[27], (N_MOE, N_EXPERTS, D_MODEL, 2 * D_FF_EXPERT), D_MODEL ** -0.5),
        'moe_w_down': nrm(ks[28], (N_MOE, N_EXPERTS, D_FF_EXPERT, D_MODEL), D_FF_EXPERT ** -0.5),
    }


def reference(x_prompt, x_sample, cache_kv, state_win, page_table, c_prompt, c_sample,
              norm_mix_g, norm_ffn_g, norm_final_g, w_ada, b_ada, w_in, cmp_pe, cmp_w1, cmp_w2,
              sg_norm_g, sg_norm_b, sg_w, sg_b, w_branch_nsa, w_branch_sg, w_out,
              ffn_w_gu, ffn_w_down, router_w, router_b, moe_w_gu, moe_w_down):
    n_seq, n_pages = page_table.shape
    xp, xs = x_prompt, x_sample
    kv_p, kv_s, win_p, win_s, sgv_s = [], [], [], [], []
    for i in range(DEPTH):
        lp = {
            'w_ada': w_ada[i], 'b_ada': b_ada[i], 'norm_mix': norm_mix_g[i], 'norm_ffn': norm_ffn_g[i],
            'w_in': w_in[i], 'cmp_pe': cmp_pe[i], 'cmp_w1': cmp_w1[i], 'cmp_w2': cmp_w2[i],
            'sg_norm_g': sg_norm_g[i], 'sg_norm_b': sg_norm_b[i], 'sg_w': sg_w[i], 'sg_b': sg_b[i],
            'w_branch_nsa': w_branch_nsa[i], 'w_branch_sg': w_branch_sg[i], 'w_out': w_out[i],
        }
        if i % 2 == 0:
            ffn = functools.partial(swiglu, w_gu=ffn_w_gu[i // 2], w_down=ffn_w_down[i // 2])
        else:
            ffn = functools.partial(moe_ffn, router_w=router_w[i // 2], router_b=router_b[i // 2],
                                    w_gu=moe_w_gu[i // 2], w_down=moe_w_down[i // 2])
        past = cache_kv[i][page_table].reshape(n_seq, n_pages * PAGE_SIZE, 4, N_KV, HEAD_DIM)
        xp, kvr_p, winr_p, _ = trunk_layer(xp, c_prompt, lp, ffn, None, None)
        xs, kvr_s, winr_s, v_s = trunk_layer(xs, c_sample, lp, ffn, past, state_win[i])
        kv_p.append(kvr_p)
        kv_s.append(kvr_s)
        win_p.append(winr_p)
        win_s.append(winr_s)
        sgv_s.append(v_s)
    y_prompt = rmsnorm(xp, norm_final_g)
    y_sample = rmsnorm(xs, norm_final_g)
    return (y_prompt, y_sample, jnp.stack(kv_p), jnp.stack(kv_s), jnp.stack(win_p), jnp.stack(win_s), jnp.stack(sgv_s))
```

```python
import functools

import jax
import jax.numpy as jnp
from jax import lax
from jax.experimental import pallas as pl
from jax.experimental.pallas import tpu as pltpu

F32 = jnp.float32
BF16 = jnp.bfloat16
I32 = jnp.int32
HIGHEST = lax.Precision.HIGHEST

LANES = 128
SUBLANES = 8
VMEM_LIMIT_BYTES = 56 * 1024 * 1024

D_MODEL = 1024
N_HEADS = 8
HEAD_DIM = 64
N_KV = 2
GROUP = N_HEADS // N_KV
D_NSA = N_HEADS * HEAD_DIM
CMP_BLOCK = 32
CMP_HIDDEN = 256
SEL_BLOCK = 64
SEL_TOPK = 16
WINDOW = 512
Q_BLOCK = 128
N_SG = 8
SG_DIM = 64
D_SG = N_SG * SG_DIM
CHUNK = 128
N_EXPERTS = 8
D_FF_CHUNK = 1408
PAGE_SIZE = 128
EPS = 1e-6
NEG = -1e30
FORCED = 1e9
REMOVED = -3e38

KV_COLS = 6 * N_KV * HEAD_DIM
OFF_KV = D_NSA
OFF_G = OFF_KV + KV_COLS
OFF_U = OFF_G + 3 * N_HEADS
OFF_V = OFF_U + D_SG
OFF_GA = OFF_V + D_SG
OFF_GB = OFF_GA + D_MODEL
IN_COLS = OFF_GB + D_MODEL

WN_KV = 0
WN_U = WN_KV + KV_COLS
WN_V = WN_U + D_SG
WN_GA = WN_V + D_SG
WN_GB = WN_GA + D_MODEL
WN_KAUG = WN_GB + D_MODEL
WN_COLS = WN_KAUG + 4 * LANES
WT_Q = 0
WT_V = D_NSA
WT_G = WT_V + 4 * HEAD_DIM
WT_ROWS = WT_G + 32

POS_HI_LANE = HEAD_DIM
POS_LO_LANE = HEAD_DIM + 1
ONE_LANE = HEAD_DIM + 2


def _cparams(sem):
    return pltpu.CompilerParams(dimension_semantics=sem, vmem_limit_bytes=VMEM_LIMIT_BYTES)


def _bdot(a, b):
    return jnp.dot(a.astype(BF16), b.astype(BF16), preferred_element_type=F32)


def _dot_nt(a, b):
    return lax.dot_general(a.astype(BF16), b.astype(BF16), (((1,), (1,)), ((), ())),
                           preferred_element_type=F32)


def _silu(x):
    return x * jax.nn.sigmoid(x)


def _pos_aug(pos, lane):
    hi = (pos >> 7).astype(F32)
    lo = (pos & 127).astype(F32)
    return jnp.where(lane == POS_HI_LANE, hi,
                     jnp.where(lane == POS_LO_LANE, lo,
                               jnp.where(lane == ONE_LANE, 1.0, 0.0)))


def _group_slopes(g, lane_head):
    out = jnp.zeros(lane_head.shape, F32)
    for gg in range(N_KV):
        for hh in range(GROUP):
            s = 2.0 ** (-8.0 * (gg * GROUP + hh + 1) / N_HEADS)
            out = jnp.where((lane_head == hh) & (g == gg), s, out)
    return out


def _qt_aug(qt_ref, g, q0_blocks):
    heads = [qt_ref[h * HEAD_DIM:(h + 1) * HEAD_DIM, :] for h in range(GROUP)]
    q = jnp.concatenate(heads, axis=1)
    n = GROUP * Q_BLOCK
    row = lax.broadcasted_iota(I32, (LANES - HEAD_DIM, n), 0)
    lane_head = lax.broadcasted_iota(I32, (LANES - HEAD_DIM, n), 1) // Q_BLOCK
    slope = _group_slopes(g, lane_head)
    q0f = (q0_blocks).astype(F32)
    aug = jnp.where(row == 0, slope * 128.0,
                    jnp.where(row == 1, slope,
                              jnp.where(row == 2, -(slope * 128.0) * q0f, 0.0)))
    return jnp.concatenate([q.astype(BF16), aug.astype(BF16)], axis=0)


def _ada_kernel(c_ref, w_ref, b_ref, o_ref):
    c = c_ref[...]
    o_ref[0] = jnp.dot(_silu(c), w_ref[0], precision=HIGHEST, preferred_element_type=F32) + b_ref[0]


def _ada(c_all, w_ada, b_ada):
    depth = w_ada.shape[0]
    rows = c_all.shape[0]
    tn = 1024
    n = w_ada.shape[2]
    return pl.pallas_call(
        _ada_kernel,
        grid=(depth, n // tn),
        in_specs=[pl.BlockSpec((rows, D_MODEL), lambda l, j: (0, 0)),
                  pl.BlockSpec((1, D_MODEL, tn), lambda l, j: (l, 0, j)),
                  pl.BlockSpec((1, 1, tn), lambda l, j: (l, 0, j))],
        out_specs=pl.BlockSpec((1, rows, tn), lambda l, j: (l, 0, j)),
        out_shape=jax.ShapeDtypeStruct((depth, rows, n), F32),
        compiler_params=_cparams(("arbitrary", "arbitrary")),
        name="ada",
    )(c_all, w_ada, b_ada.reshape(depth, 1, n))


def _in_proj_kernel(x_ref, g_ref, sc_ref, sh_ref, wn_ref, wt_ref, lng_ref, lnb_ref,
                    kv_ref, u_ref, v_ref, ga_ref, gb_ref, kaug_ref, qt_ref, vt_ref, gt_ref, *, tm):
    i = pl.program_id(0)
    x = x_ref[...]
    ms = jnp.mean(x * x, axis=-1, keepdims=True)
    h = x * lax.rsqrt(ms + EPS) * g_ref[...]
    h = h * (1.0 + sc_ref[...]) + sh_ref[...]
    hb = h.astype(BF16)

    def seg(a, b):
        return jnp.dot(hb, wn_ref[:, a:b], preferred_element_type=F32)

    kv_ref[...] = seg(WN_KV, WN_U)
    u_ref[...] = jax.nn.gelu(seg(WN_U, WN_V)).astype(BF16)
    v = jax.nn.gelu(seg(WN_V, WN_GA))
    mu = jnp.mean(v, axis=-1, keepdims=True)
    var = jnp.mean(jnp.square(v - mu), axis=-1, keepdims=True)
    v_ref[...] = (v - mu) * lax.rsqrt(var + EPS) * lng_ref[...] + lnb_ref[...]
    ga_ref[...] = jax.nn.sigmoid(seg(WN_GA, WN_GB)).astype(BF16)
    gb_ref[...] = jax.nn.sigmoid(seg(WN_GB, WN_KAUG)).astype(BF16)

    pos = lax.broadcasted_iota(I32, (tm, LANES), 0) + i * tm
    lane = lax.broadcasted_iota(I32, (tm, LANES), 1)
    aug = _pos_aug(pos, lane)
    for j in range(4):
        k = seg(WN_KAUG + j * LANES, WN_KAUG + (j + 1) * LANES)
        kaug_ref[j] = (k + aug).astype(BF16)

    zt = _dot_nt(wt_ref[...], hb)
    qt_ref[...] = (zt[WT_Q:WT_V] * (HEAD_DIM ** -0.5)).astype(BF16)
    vt_ref[...] = zt[WT_V:WT_G].astype(BF16)
    gt_ref[...] = jax.nn.sigmoid(zt[WT_G:WT_ROWS])


def _in_proj(x, g, sc, sh, wn, wt, lng, lnb, tm):
    m = x.shape[0]
    mod_rows = sc.shape[0]
    mod_block = (1, D_MODEL) if mod_rows == 1 else (tm, D_MODEL)
    mod_map = (lambda i: (0, 0)) if mod_rows == 1 else (lambda i: (i, 0))
    row = lambda i: (i, 0)
    col = lambda i: (0, i)
    const = lambda i: (0, 0)
    out_shape = (
        jax.ShapeDtypeStruct((m, KV_COLS), F32),
        jax.ShapeDtypeStruct((m, D_SG), BF16),
        jax.ShapeDtypeStruct((m, D_SG), F32),
        jax.ShapeDtypeStruct((m, D_MODEL), BF16),
        jax.ShapeDtypeStruct((m, D_MODEL), BF16),
        jax.ShapeDtypeStruct((4, m, LANES), BF16),
        jax.ShapeDtypeStruct((D_NSA, m), BF16),
        jax.ShapeDtypeStruct((4 * HEAD_DIM, m), BF16),
        jax.ShapeDtypeStruct((32, m), F32),
    )
    out_specs = (
        pl.BlockSpec((tm, KV_COLS), row),
        pl.BlockSpec((tm, D_SG), row),
        pl.BlockSpec((tm, D_SG), row),
        pl.BlockSpec((tm, D_MODEL), row),
        pl.BlockSpec((tm, D_MODEL), row),
        pl.BlockSpec((4, tm, LANES), lambda i: (0, i, 0)),
        pl.BlockSpec((D_NSA, tm), col),
        pl.BlockSpec((4 * HEAD_DIM, tm), col),
        pl.BlockSpec((32, tm), col),
    )
    return pl.pallas_call(
        functools.partial(_in_proj_kernel, tm=tm),
        grid=(m // tm,),
        in_specs=[pl.BlockSpec((tm, D_MODEL), row),
                  pl.BlockSpec((1, D_MODEL), const),
                  pl.BlockSpec(mod_block, mod_map),
                  pl.BlockSpec(mod_block, mod_map),
                  pl.BlockSpec((D_MODEL, WN_COLS), const),
                  pl.BlockSpec((WT_ROWS, D_MODEL), const),
                  pl.BlockSpec((1, D_SG), const),
                  pl.BlockSpec((1, D_SG), const)],
        out_specs=out_specs,
        out_shape=out_shape,
        compiler_params=_cparams(("arbitrary",)),
        name="in_proj",
    )(x, g, sc, sh, wn, wt, lng, lnb)


def _pack_w_in(w):
    kv = w[:, OFF_KV:OFF_G]

    def kvcol(j, g):
        return kv[:, (j * N_KV + g) * HEAD_DIM:(j * N_KV + g + 1) * HEAD_DIM]

    zpad = jnp.zeros((D_MODEL, LANES - HEAD_DIM), w.dtype)
    kaug = [jnp.concatenate([kvcol(j, g), zpad], axis=1) for j in (2, 4) for g in range(N_KV)]
    wn = jnp.concatenate([kv, w[:, OFF_U:OFF_V], w[:, OFF_V:OFF_GA], w[:, OFF_GA:OFF_GB],
                          w[:, OFF_GB:IN_COLS]] + kaug, axis=1)
    vt = [kvcol(j, g) for j in (3, 5) for g in range(N_KV)]
    gpad = jnp.zeros((D_MODEL, 32 - 3 * N_HEADS), w.dtype)
    wt = jnp.concatenate([w[:, :OFF_KV]] + vt + [w[:, OFF_G:OFF_U], gpad], axis=1).T
    return wn.astype(BF16), wt.astype(BF16)


def _cmp_block_end(nc):
    r = lax.broadcasted_iota(I32, (nc, LANES), 0)
    half = nc // 2
    c = 2 * (r % half) + r // half
    return c * CMP_BLOCK + (CMP_BLOCK - 1)


def _compress_kernel(xk_ref, xv_ref, pe_ref, w1_ref, w2k_ref, w2vt_ref, kc_ref, vct_ref, *, nc):
    xk = xk_ref[0, 0] + pe_ref[0]
    hk = _silu(_bdot(xk, w1_ref[0]))
    kc = _bdot(hk, w2k_ref[...])
    lane = lax.broadcasted_iota(I32, (nc, LANES), 1)
    kc_ref[0] = (kc + _pos_aug(_cmp_block_end(nc), lane)).astype(BF16)
    xv = xv_ref[0, 0] + pe_ref[1]
    hv = _silu(_bdot(xv, w1_ref[1]))
    vct_ref[0] = _dot_nt(w2vt_ref[...], hv).astype(BF16)


def _compress_prompt(kv, pe, w1, w2):
    m = kv.shape[0]
    nc = m // CMP_BLOCK
    half = nc // 2
    kvc = kv[:, :2 * N_KV * HEAD_DIM].reshape(half, 2, CMP_BLOCK, 2, N_KV, HEAD_DIM)
    x = kvc.transpose(3, 4, 1, 0, 2, 5).reshape(2, N_KV, nc, CMP_BLOCK * HEAD_DIM)
    pe_flat = pe.reshape(2, 1, CMP_BLOCK * HEAD_DIM)
    w2k = jnp.concatenate([w2[0], jnp.zeros((CMP_HIDDEN, LANES - HEAD_DIM), w2.dtype)], axis=1).astype(BF16)
    w2vt = w2[1].T.astype(BF16)
    kd = CMP_BLOCK * HEAD_DIM
    return pl.pallas_call(
        functools.partial(_compress_kernel, nc=nc),
        grid=(N_KV,),
        in_specs=[pl.BlockSpec((1, 1, nc, kd), lambda g: (0, g, 0, 0)),
                  pl.BlockSpec((1, 1, nc, kd), lambda g: (1, g, 0, 0)),
                  pl.BlockSpec((2, 1, kd), lambda g: (0, 0, 0)),
                  pl.BlockSpec((2, kd, CMP_HIDDEN), lambda g: (0, 0, 0)),
                  pl.BlockSpec((CMP_HIDDEN, LANES), lambda g: (0, 0)),
                  pl.BlockSpec((HEAD_DIM, CMP_HIDDEN), lambda g: (0, 0))],
        out_specs=(pl.BlockSpec((1, nc, LANES), lambda g: (g, 0, 0)),
                   pl.BlockSpec((1, HEAD_DIM, nc), lambda g: (g, 0, 0))),
        out_shape=(jax.ShapeDtypeStruct((N_KV, nc, LANES), BF16),
                   jax.ShapeDtypeStruct((N_KV, HEAD_DIM, nc), BF16)),
        compiler_params=_cparams(("arbitrary",)),
        name="compress_prompt",
    )(x, x, pe_flat, w1.astype(BF16), w2k, w2vt)


def _topk_mask_rows(imp, k):
    rows = lax.broadcasted_iota(I32, imp.shape, 0)
    big = jnp.int32(2 ** 30)

    def body(_, carry):
        v, sel = carry
        m = jnp.max(v, axis=0, keepdims=True)
        first = jnp.min(jnp.where(v == m, rows, big), axis=0, keepdims=True)
        hit = rows == first
        return jnp.where(hit, REMOVED, v), jnp.where(hit, 1.0, sel)

    _, sel = lax.fori_loop(0, k, body, (imp, jnp.zeros(imp.shape, F32)))
    return sel


def _select_kernel(qt_ref, kc_ref, vct_ref, mask_ref, cnt_ref, ocmp_ref, *, nc):
    i = pl.program_id(0)
    ns = nc // 2
    n = GROUP * Q_BLOCK
    q0 = i * Q_BLOCK
    cend = _cmp_block_end(nc)[:, :1]
    qpos = q0 + lax.broadcasted_iota(I32, (1, n), 1) % Q_BLOCK
    valid = cend <= qpos
    blk = lax.broadcasted_iota(I32, (ns, Q_BLOCK), 0)
    cur = (q0 + lax.broadcasted_iota(I32, (ns, Q_BLOCK), 1)) // SEL_BLOCK
    ones = jnp.ones((SUBLANES, Q_BLOCK), BF16)
    for g in range(N_KV):
        qa = _qt_aug(qt_ref.at[g * GROUP * HEAD_DIM:(g + 1) * GROUP * HEAD_DIM], g, i)
        s = jnp.dot(kc_ref[g], qa, preferred_element_type=F32)
        s = jnp.where(valid, s, NEG)
        m = jnp.max(s, axis=0, keepdims=True)
        p = jnp.where(valid, jnp.exp(s - m), 0.0)
        p = p / jnp.maximum(jnp.sum(p, axis=0, keepdims=True), 1e-30)
        ocmp_ref[0, g] = jnp.dot(vct_ref[g], p.astype(BF16), preferred_element_type=F32)
        ph = p[:, 0:Q_BLOCK]
        for h in range(1, GROUP):
            ph = ph + p[:, h * Q_BLOCK:(h + 1) * Q_BLOCK]
        imp = ph[:ns] + ph[ns:]
        forced = (blk == 0) | (blk == cur) | (blk == cur - 1)
        imp = jnp.where(forced, FORCED, imp)
        imp = jnp.where(blk <= cur, imp, NEG)
        sel = _topk_mask_rows(imp, min(SEL_TOPK, ns))
        sel = jnp.where(blk <= cur, sel, 0.0)
        mask_ref[0, g] = sel
        cnt_ref[0, g] = _dot_nt(ones, sel)


def _select(qt, kc, vct):
    m = qt.shape[1]
    nqb = m // Q_BLOCK
    nc = m // CMP_BLOCK
    ns = nc // 2
    n = GROUP * Q_BLOCK
    return pl.pallas_call(
        functools.partial(_select_kernel, nc=nc),
        grid=(nqb,),
        in_specs=[pl.BlockSpec((D_NSA, Q_BLOCK), lambda i: (0, i)),
                  pl.BlockSpec((N_KV, nc, LANES), lambda i: (0, 0, 0)),
                  pl.BlockSpec((N_KV, HEAD_DIM, nc), lambda i: (0, 0, 0))],
        out_specs=(pl.BlockSpec((1, N_KV, ns, Q_BLOCK), lambda i: (i, 0, 0, 0)),
                   pl.BlockSpec((1, N_KV, SUBLANES, ns), lambda i: (i, 0, 0, 0)),
                   pl.BlockSpec((1, N_KV, HEAD_DIM, n), lambda i: (i, 0, 0, 0))),
        out_shape=(jax.ShapeDtypeStruct((nqb, N_KV, ns, Q_BLOCK), F32),
                   jax.ShapeDtypeStruct((nqb, N_KV, SUBLANES, ns), F32),
                   jax.ShapeDtypeStruct((nqb, N_KV, HEAD_DIM, n), F32)),
        compiler_params=_cparams(("arbitrary",)),
        name="nsa_select",
    )(qt, kc, vct)


def _online_update(state, s, valid, vt_tile):
    m, l, acc = state
    s = jnp.where(valid, s, NEG)
    m_new = jnp.maximum(m, jnp.max(s, axis=0, keepdims=True))
    alpha = jnp.exp(m - m_new)
    p = jnp.where(valid, jnp.exp(s - m_new), 0.0)
    l = alpha * l + jnp.sum(p, axis=0, keepdims=True)
    acc = alpha * acc + jnp.dot(vt_tile, p.astype(BF16), preferred_element_type=F32)
    return m_new, l, acc


def _attend_kernel(cnt_ref, ids_ref, qt_ref, ksel_ref, vsel_ref, kwin_ref, vwin_ref, mask_ref,
                   ocmp_ref, gt_ref, o_ref, *, nqb, ns):
    g = pl.program_id(0)
    i = pl.program_id(1)
    n = GROUP * Q_BLOCK
    q0 = i * Q_BLOCK
    qa = _qt_aug(qt_ref, g, i)
    qpos = q0 + lax.broadcasted_iota(I32, (1, n), 1) % Q_BLOCK
    init = (jnp.full((1, n), NEG, F32), jnp.zeros((1, n), F32), jnp.zeros((HEAD_DIM, n), F32))

    slot = g * nqb + i
    kpos_s = lax.broadcasted_iota(I32, (SEL_BLOCK, 1), 0)

    def sel_body(t, state):
        b = ids_ref[slot * ns + t]
        k = ksel_ref[0, pl.ds(pl.multiple_of(b * SEL_BLOCK, SEL_BLOCK), SEL_BLOCK), :]
        s = jnp.dot(k, qa, preferred_element_type=F32)
        row = mask_ref[0, 0, pl.ds(b, 1), :]
        chosen = jnp.concatenate([row] * GROUP, axis=1) > 0.5
        valid = chosen & (kpos_s + b * SEL_BLOCK <= qpos)
        return _online_update(state, s, valid, vsel_ref[0, b])

    _, l_s, acc_s = lax.fori_loop(0, cnt_ref[slot], sel_body, init)
    o_sel = acc_s / jnp.maximum(l_s, 1e-30)

    kpos_w = lax.broadcasted_iota(I32, (Q_BLOCK, 1), 0)
    nwin = WINDOW // Q_BLOCK + 1

    def win_body(j, state):
        tile = i - (nwin - 1) + j
        start = pl.multiple_of(tile * Q_BLOCK, Q_BLOCK)
        k = kwin_ref[0, pl.ds(start, Q_BLOCK), :]
        s = jnp.dot(k, qa, preferred_element_type=F32)
        dist = qpos - (kpos_w + start)
        valid = (dist >= 0) & (dist < WINDOW)
        return _online_update(state, s, valid, vwin_ref[0, tile])

    _, l_w, acc_w = lax.fori_loop(jnp.maximum(nwin - 1 - i, 0), nwin, win_body, init)
    o_win = acc_w / jnp.maximum(l_w, 1e-30)

    def gate(branch):
        rows = [gt_ref[pl.ds(branch * N_HEADS + g * GROUP + h, 1), :] for h in range(GROUP)]
        return jnp.concatenate(rows, axis=1)

    o = ocmp_ref[0, 0] * gate(0) + o_sel * gate(1) + o_win * gate(2)
    for hp in range(GROUP // 2):
        pair = jnp.concatenate([o[:, (2 * hp) * Q_BLOCK:(2 * hp + 1) * Q_BLOCK],
                                o[:, (2 * hp + 1) * Q_BLOCK:(2 * hp + 2) * Q_BLOCK]], axis=0)
        o_ref[:, hp * LANES:(hp + 1) * LANES] = pair.T.astype(o_ref.dtype)


def _attend(cnt, ids, qt, kaug, vt_blocks_sel, vt_blocks_win, mask, ocmp, gt):
    m = qt.shape[1]
    nqb = m // Q_BLOCK
    ns = m // SEL_BLOCK
    n = GROUP * Q_BLOCK
    gh = GROUP * HEAD_DIM
    grid_spec = pltpu.PrefetchScalarGridSpec(
        num_scalar_prefetch=2,
        grid=(N_KV, nqb),
        in_specs=[pl.BlockSpec((gh, Q_BLOCK), lambda g, i, c, d: (g, i)),
                  pl.BlockSpec((1, m, LANES), lambda g, i, c, d: (g, 0, 0)),
                  pl.BlockSpec((1, ns, HEAD_DIM, SEL_BLOCK), lambda g, i, c, d: (g, 0, 0, 0)),
                  pl.BlockSpec((1, m, LANES), lambda g, i, c, d: (N_KV + g, 0, 0)),
                  pl.BlockSpec((1, nqb, HEAD_DIM, Q_BLOCK), lambda g, i, c, d: (g, 0, 0, 0)),
                  pl.BlockSpec((1, 1, ns, Q_BLOCK), lambda g, i, c, d: (i, g, 0, 0)),
                  pl.BlockSpec((1, 1, HEAD_DIM, n), lambda g, i, c, d: (i, g, 0, 0)),
                  pl.BlockSpec((32, Q_BLOCK), lambda g, i, c, d: (0, i))],
        out_specs=pl.BlockSpec((Q_BLOCK, gh), lambda g, i, c, d: (i, g)),
    )
    return pl.pallas_call(
        functools.partial(_attend_kernel, nqb=nqb, ns=ns),
        grid_spec=grid_spec,
        out_shape=jax.ShapeDtypeStruct((m, D_NSA), BF16),
        compiler_params=_cparams(("arbitrary", "arbitrary")),
        name="nsa_attend",
    )(cnt, ids, qt, kaug, vt_blocks_sel, kaug, vt_blocks_win, mask, ocmp, gt)


def _nsa_prompt(kv, kaug, qt, vt, gt, pe, w1, w2):
    m = kv.shape[0]
    nqb = m // Q_BLOCK
    ns = m // SEL_BLOCK
    kc, vct = _compress_prompt(kv, pe, w1, w2)
    mask, cnt8, ocmp = _select(qt, kc, vct)
    flags = cnt8[:, :, 0, :].transpose(1, 0, 2) > 0.5
    cnt = jnp.sum(flags, axis=-1).astype(I32).reshape(-1)
    ids = jnp.argsort(jnp.where(flags, 0, 1), axis=-1, stable=True).astype(I32).reshape(-1)
    vt4 = vt.reshape(4, HEAD_DIM, m)
    vsel = vt4[:N_KV].reshape(N_KV, HEAD_DIM, ns, SEL_BLOCK).transpose(0, 2, 1, 3)
    vwin = vt4[N_KV:].reshape(N_KV, HEAD_DIM, nqb, Q_BLOCK).transpose(0, 2, 1, 3)
    return _attend(cnt, ids, qt, kaug, vsel, vwin, mask, ocmp, gt)


def _mix_kernel(x_ref, onsa_ref, u_ref, v_ref, ga_ref, gb_ref, g1_ref, wa_ref, wb_ref, wo_ref,
                sgw_ref, sgb_ref, o_ref, osg_ref, *, tm, chunked):
    if chunked:
        lane = lax.broadcasted_iota(I32, (CHUNK, LANES), 1)
        for c in range(tm // CHUNK):
            rows = slice(c * CHUNK, (c + 1) * CHUNK)
            for pr in range(N_SG // 2):
                cols = slice(pr * LANES, (pr + 1) * LANES)
                vp = v_ref[rows, cols].astype(BF16)
                a = jnp.dot(sgw_ref[2 * pr], vp, preferred_element_type=F32)
                b = jnp.dot(sgw_ref[2 * pr + 1], vp, preferred_element_type=F32)
                mix = jnp.where(lane < SG_DIM, a, b) + sgb_ref[:, cols]
                osg_ref[rows, cols] = (u_ref[rows, cols].astype(F32) * mix).astype(BF16)
    else:
        mix = v_ref[...] * sgw_ref[...] + sgb_ref[...]
        osg_ref[...] = (u_ref[...].astype(F32) * mix).astype(BF16)
    a = jnp.dot(onsa_ref[...], wa_ref[...], preferred_element_type=F32)
    b = jnp.dot(osg_ref[...], wb_ref[...], preferred_element_type=F32)
    merged = ga_ref[...].astype(F32) * a + gb_ref[...].astype(F32) * b
    y = jnp.dot(merged.astype(BF16), wo_ref[...], preferred_element_type=F32)
    o_ref[...] = x_ref[...] + g1_ref[...] * y


def _mix_out(x, onsa, u, v, ga, gb, g1, wa, wb, wo, sgw, sgb, tm, chunked):
    m = x.shape[0]
    mod_rows = g1.shape[0]
    mod_block = (1, D_MODEL) if mod_rows == 1 else (tm, D_MODEL)
    mod_map = (lambda i: (0, 0)) if mod_rows == 1 else (lambda i: (i, 0))
    row = lambda i: (i, 0)
    const2 = lambda i: (0, 0)
    if chunked:
        sg_specs = [pl.BlockSpec((N_SG, CHUNK, CHUNK), lambda i: (0, 0, 0)),
                    pl.BlockSpec((CHUNK, D_SG), const2)]
    else:
        sg_specs = [pl.BlockSpec((1, D_SG), const2), pl.BlockSpec((1, D_SG), const2)]
    return pl.pallas_call(
        functools.partial(_mix_kernel, tm=tm, chunked=chunked),
        grid=(m // tm,),
        in_specs=[pl.BlockSpec((tm, D_MODEL), row),
                  pl.BlockSpec((tm, D_NSA), row),
                  pl.BlockSpec((tm, D_SG), row),
                  pl.BlockSpec((tm, D_SG), row),
                  pl.BlockSpec((tm, D_MODEL), row),
                  pl.BlockSpec((tm, D_MODEL), row),
                  pl.BlockSpec(mod_block, mod_map),
                  pl.BlockSpec((D_NSA, D_MODEL), const2),
                  pl.BlockSpec((D_SG, D_MODEL), const2),
                  pl.BlockSpec((D_MODEL, D_MODEL), const2)] + sg_specs,
        out_specs=pl.BlockSpec((tm, D_MODEL), row),
        out_shape=jax.ShapeDtypeStruct((m, D_MODEL), F32),
        scratch_shapes=[pltpu.VMEM((tm, D_SG), BF16)],
        compiler_params=_cparams(("arbitrary",)),
        name="mix_out",
    )(x, onsa, u, v, ga, gb, g1, wa, wb, wo, sgw, sgb)


def _top2_combine(logits):
    lane = lax.broadcasted_iota(I32, logits.shape, 1)
    big = jnp.int32(2 ** 30)
    z = jnp.where(lane < N_EXPERTS, logits, -jnp.inf)
    t1 = jnp.max(z, axis=-1, keepdims=True)
    i1 = jnp.min(jnp.where(z == t1, lane, big), axis=-1, keepdims=True)
    z2 = jnp.where(lane == i1, -jnp.inf, z)
    t2 = jnp.max(z2, axis=-1, keepdims=True)
    i2 = jnp.min(jnp.where(z2 == t2, lane, big), axis=-1, keepdims=True)
    e = jnp.exp(t2 - t1)
    den = 1.0 + e
    return jnp.where(lane == i1, 1.0 / den, 0.0) + jnp.where(lane == i2, e / den, 0.0)


def _ffn_kernel(x_ref, g_ref, sc_ref, sh_ref, g2_ref, wa_ref, wb_ref, wd_ref, rw_ref, rb_ref, gf_ref,
                o_ref, h_ref, acc_ref, comb_ref, *, routed, final_norm, n_chunks):
    e = pl.program_id(1)

    @pl.when(e == 0)
    def _():
        x = x_ref[...]
        ms = jnp.mean(x * x, axis=-1, keepdims=True)
        h = x * lax.rsqrt(ms + EPS) * g_ref[...]
        h = h * (1.0 + sc_ref[...]) + sh_ref[...]
        h_ref[...] = h.astype(BF16)
        acc_ref[...] = jnp.zeros_like(acc_ref)
        if routed:
            logits = jnp.dot(h, rw_ref[...], precision=HIGHEST, preferred_element_type=F32) + rb_ref[...]
            comb = _top2_combine(logits)
            for ee in range(N_EXPERTS):
                comb_ref[ee] = jnp.broadcast_to(comb[:, ee:ee + 1], comb.shape)

    hb = h_ref[...]
    a = jnp.dot(hb, wa_ref[0], preferred_element_type=F32)
    b = jnp.dot(hb, wb_ref[0], preferred_element_type=F32)
    t = (_silu(a) * b).astype(BF16)
    y = jnp.dot(t, wd_ref[0], preferred_element_type=F32)
    if routed:
        w = comb_ref[e]
        for c in range(D_MODEL // LANES):
            cols = slice(c * LANES, (c + 1) * LANES)
            acc_ref[:, cols] += w * y[:, cols]
    else:
        acc_ref[...] += y

    @pl.when(e == n_chunks - 1)
    def _():
        out = x_ref[...] + g2_ref[...] * acc_ref[...]
        if final_norm:
            ms = jnp.mean(out * out, axis=-1, keepdims=True)
            out = out * lax.rsqrt(ms + EPS) * gf_ref[...]
        o_ref[...] = out


def _ffn(x, g, sc, sh, g2, wa, wb, wd, rw, rb, gf, tm, routed, final_norm):
    m = x.shape[0]
    n_chunks = wa.shape[0]
    mod_rows = sc.shape[0]
    mod_block = (1, D_MODEL) if mod_rows == 1 else (tm, D_MODEL)
    mod_map = (lambda i, e: (0, 0)) if mod_rows == 1 else (lambda i, e: (i, 0))
    row = lambda i, e: (i, 0)
    const2 = lambda i, e: (0, 0)
    return pl.pallas_call(
        functools.partial(_ffn_kernel, routed=routed, final_norm=final_norm, n_chunks=n_chunks),
        grid=(m // tm, n_chunks),
        in_specs=[pl.BlockSpec((tm, D_MODEL), row),
                  pl.BlockSpec((1, D_MODEL), const2),
                  pl.BlockSpec(mod_block, mod_map),
                  pl.BlockSpec(mod_block, mod_map),
                  pl.BlockSpec(mod_block, mod_map),
                  pl.BlockSpec((1, D_MODEL, D_FF_CHUNK), lambda i, e: (e, 0, 0)),
                  pl.BlockSpec((1, D_MODEL, D_FF_CHUNK), lambda i, e: (e, 0, 0)),
                  pl.BlockSpec((1, D_FF_CHUNK, D_MODEL), lambda i, e: (e, 0, 0)),
                  pl.BlockSpec((D_MODEL, LANES), const2),
                  pl.BlockSpec((1, LANES), const2),
                  pl.BlockSpec((1, D_MODEL), const2)],
        out_specs=pl.BlockSpec((tm, D_MODEL), row),
        out_shape=jax.ShapeDtypeStruct((m, D_MODEL), F32),
        scratch_shapes=[pltpu.VMEM((tm, D_MODEL), BF16),
                        pltpu.VMEM((tm, D_MODEL), F32),
                        pltpu.VMEM((N_EXPERTS, tm, LANES), F32)],
        compiler_params=_cparams(("arbitrary", "arbitrary")),
        name="ffn",
    )(x, g, sc, sh, g2, wa, wb, wd, rw, rb, gf)


def _pack_dense_ffn(w_gu, w_down):
    d_ff = w_down.shape[0]
    n = d_ff // D_FF_CHUNK
    wa = w_gu[:, :d_ff].reshape(D_MODEL, n, D_FF_CHUNK).transpose(1, 0, 2)
    wb = w_gu[:, d_ff:].reshape(D_MODEL, n, D_FF_CHUNK).transpose(1, 0, 2)
    wd = w_down.reshape(n, D_FF_CHUNK, D_MODEL)
    return wa.astype(BF16), wb.astype(BF16), wd.astype(BF16)


def _pack_moe_ffn(w_gu, w_down):
    return (w_gu[:, :, :D_FF_CHUNK].astype(BF16), w_gu[:, :, D_FF_CHUNK:].astype(BF16),
            w_down.astype(BF16))


PAGES_PER_STEP = 16
BLOCKS_PER_PAGE = PAGE_SIZE // CMP_BLOCK
CMP_COLS = 2 * N_KV * HEAD_DIM


def _compress_sample_kernel(pt_ref, *refs, n_steps):
    del pt_ref
    pages = refs[:2 * PAGES_PER_STEP]
    pe_ref, w1_ref, w2_ref, kc_ref, vc_ref, x_ref, acc_ref = refs[2 * PAGES_PER_STEP:]
    s = pl.program_id(1)
    nb = PAGES_PER_STEP * BLOCKS_PER_PAGE
    half = nb // 2
    acc_ref[...] = jnp.zeros_like(acc_ref)
    for t in range(CMP_BLOCK):
        for j in range(2):
            for p in range(PAGES_PER_STEP):
                page = pages[j * PAGES_PER_STEP + p]
                x_ref[j, 2 * p:2 * p + 2, :] = page[0, pl.ds(t, 2, stride=2 * CMP_BLOCK), :]
                x_ref[j, half + 2 * p:half + 2 * p + 2, :] = page[0, pl.ds(CMP_BLOCK + t, 2, stride=2 * CMP_BLOCK), :]
            xj = (x_ref[j] + pe_ref[t, j]).astype(BF16)
            for g in range(N_KV):
                acc_ref[j * N_KV + g] += jnp.dot(xj, w1_ref[j, g, t], preferred_element_type=F32)
    row = lax.broadcasted_iota(I32, (nb, LANES), 0)
    lane = lax.broadcasted_iota(I32, (nb, LANES), 1)
    c = s * nb + 2 * (row % half) + row // half
    aug = _pos_aug(c * CMP_BLOCK + (CMP_BLOCK - 1), lane)
    for g in range(N_KV):
        hk = _silu(acc_ref[g]).astype(BF16)
        kc = (jnp.dot(hk, w2_ref[0], preferred_element_type=F32) + aug).astype(BF16)
        kc_ref[0, g, 0] = kc[:half]
        kc_ref[0, g, 1] = kc[half:]
        hv = _silu(acc_ref[N_KV + g]).astype(BF16)
        vc = jnp.dot(hv, w2_ref[1], preferred_element_type=F32).astype(BF16)
        vc_ref[0, g, 0] = vc[:half]
        vc_ref[0, g, 1] = vc[half:]


def _compress_sample(cache, page_table, pe, w1, w2):
    n_seq, n_pages = page_table.shape
    n_steps = n_pages // PAGES_PER_STEP
    nb = PAGES_PER_STEP * BLOCKS_PER_PAGE
    half = nb // 2
    nc_half = n_pages * BLOCKS_PER_PAGE // 2
    pe4 = jnp.concatenate([pe, pe], axis=-1).transpose(1, 0, 2).reshape(CMP_BLOCK, 2, 1, LANES)
    w1r = w1.reshape(2, CMP_BLOCK, HEAD_DIM, CMP_HIDDEN)
    zero = jnp.zeros_like(w1r)
    w1p = jnp.stack([jnp.concatenate([w1r, zero], axis=2), jnp.concatenate([zero, w1r], axis=2)],
                    axis=1).astype(BF16)
    w2p = jnp.pad(w2, ((0, 0), (0, 0), (0, LANES - HEAD_DIM))).astype(BF16)

    def page_map(j, k):
        return lambda b, s, pt: (pt[b * n_pages + s * PAGES_PER_STEP + k], 0, j)

    grid_spec = pltpu.PrefetchScalarGridSpec(
        num_scalar_prefetch=1,
        grid=(n_seq, n_steps),
        in_specs=[pl.BlockSpec((1, PAGE_SIZE, LANES), page_map(j, k))
                  for j in range(2) for k in range(PAGES_PER_STEP)] + [
            pl.BlockSpec((CMP_BLOCK, 2, 1, LANES), lambda b, s, pt: (0, 0, 0, 0)),
            pl.BlockSpec((2, N_KV, CMP_BLOCK, LANES, CMP_HIDDEN), lambda b, s, pt: (0, 0, 0, 0, 0)),
            pl.BlockSpec((2, CMP_HIDDEN, LANES), lambda b, s, pt: (0, 0, 0))],
        out_specs=(pl.BlockSpec((1, N_KV, 2, half, LANES), lambda b, s, pt: (b, 0, 0, s, 0)),
                   pl.BlockSpec((1, N_KV, 2, half, LANES), lambda b, s, pt: (b, 0, 0, s, 0))),
        scratch_shapes=[pltpu.VMEM((2, nb, LANES), F32),
                        pltpu.VMEM((2 * N_KV, nb, CMP_HIDDEN), F32)],
    )
    kc, vc = pl.pallas_call(
        functools.partial(_compress_sample_kernel, n_steps=n_steps),
        grid_spec=grid_spec,
        out_shape=(jax.ShapeDtypeStruct((n_seq, N_KV, 2, nc_half, LANES), BF16),
                   jax.ShapeDtypeStruct((n_seq, N_KV, 2, nc_half, LANES), BF16)),
        compiler_params=_cparams(("arbitrary", "arbitrary")),
        name="compress_sample",
    )(jnp.pad(page_table, ((0, 1), (0, 0))).reshape(-1),
      *([cache] * (2 * PAGES_PER_STEP)), pe4, w1p, w2p)
    return (kc.reshape(n_seq, N_KV, 2 * nc_half, LANES), vc.reshape(n_seq, N_KV, 2 * nc_half, LANES))


def _row_slopes(shape):
    head = lax.broadcasted_iota(I32, shape, 0)
    out = jnp.zeros(shape, F32)
    for h in range(N_HEADS):
        out = jnp.where(head == h, 2.0 ** (-8.0 * (h + 1) / N_HEADS), out)
    return out


def _sample_select_kernel(q_ref, kc_ref, vc_ref, ocmp_ref, imp_ref, *, nc):
    q = q_ref[0]
    ns = nc // 2
    rowgroup = lax.broadcasted_iota(I32, (N_HEADS, 1), 0) // GROUP
    o = jnp.zeros((N_HEADS, LANES), F32)
    for g in range(N_KV):
        s = _dot_nt(q, kc_ref[0, g])
        m = jnp.max(s, axis=-1, keepdims=True)
        p = jnp.exp(s - m)
        p = p / jnp.maximum(jnp.sum(p, axis=-1, keepdims=True), 1e-30)
        og = jnp.dot(p.astype(BF16), vc_ref[0, g], preferred_element_type=F32)
        mine = rowgroup == g
        o = jnp.where(mine, og, o)
        ph = jnp.sum(jnp.where(mine, p, 0.0), axis=0, keepdims=True)
        imp_ref[0, pl.ds(g, 1), :] = ph[:, :ns] + ph[:, ns:]
    ocmp_ref[0] = o


def _sample_select(q_aug, kc, vc):
    n_seq, _, nc, _ = kc.shape
    ns = nc // 2
    return pl.pallas_call(
        functools.partial(_sample_select_kernel, nc=nc),
        grid=(n_seq,),
        in_specs=[pl.BlockSpec((1, N_HEADS, LANES), lambda b: (b, 0, 0)),
                  pl.BlockSpec((1, N_KV, nc, LANES), lambda b: (b, 0, 0, 0)),
                  pl.BlockSpec((1, N_KV, nc, LANES), lambda b: (b, 0, 0, 0))],
        out_specs=(pl.BlockSpec((1, N_HEADS, LANES), lambda b: (b, 0, 0)),
                   pl.BlockSpec((1, N_KV, ns), lambda b: (b, 0, 0))),
        out_shape=(jax.ShapeDtypeStruct((n_seq, N_HEADS, LANES), F32),
                   jax.ShapeDtypeStruct((n_seq, N_KV, ns), F32)),
        compiler_params=_cparams(("arbitrary",)),
        name="sample_select",
    )(q_aug, kc, vc)


def _sample_topk_kernel(imp_ref, ids_ref, *, ns, k):
    imp = imp_ref[...]
    lane = lax.broadcasted_iota(I32, imp.shape, 1)
    out_lane = lax.broadcasted_iota(I32, ids_ref.shape, 1)
    big = jnp.int32(2 ** 30)
    v = jnp.where((lane == 0) | (lane == ns - 1), FORCED, imp)
    ids = jnp.zeros(ids_ref.shape, I32)
    for t in range(k):
        m = jnp.max(v, axis=-1, keepdims=True)
        first = jnp.min(jnp.where(v == m, lane, big), axis=-1, keepdims=True)
        v = jnp.where(lane == first, REMOVED, v)
        ids = jnp.where(out_lane == t, first, ids)
    ids_ref[...] = ids


def _sample_topk(imp, k):
    rows, ns = imp.shape
    return pl.pallas_call(
        functools.partial(_sample_topk_kernel, ns=ns, k=k),
        out_shape=jax.ShapeDtypeStruct((rows, LANES), I32),
        name="sample_topk",
    )(imp)


SEL_PAST = SEL_TOPK - 1


def _sample_attend_kernel(pt_ref, ids_ref, *refs, past_len):
    del pt_ref
    nblk = N_KV * SEL_PAST
    blocks = refs[:nblk]
    win_ref, new_ref, q_ref, ocmp_ref, gate_ref, o_ref = refs[nblk:]
    b = pl.program_id(0)
    slope = _row_slopes((N_HEADS, 1))
    new_sel = new_ref[0, :, 2 * LANES:4 * LANES].astype(BF16).astype(F32)
    new_win = new_ref[0, :, 4 * LANES:6 * LANES].astype(BF16).astype(F32)
    gates = gate_ref[0]
    w_buf = win_ref.shape[1]
    for g in range(N_KV):
        q = q_ref[0, g]
        qf = q.astype(F32)
        s_new = jnp.sum(qf * new_sel, axis=-1, keepdims=True)
        tiles, scores = [], []
        for t in range(SEL_PAST):
            blk = ids_ref[(b * N_KV + g) * SEL_TOPK + t]
            tile = blocks[g * SEL_PAST + t][0].astype(BF16)
            pos = blk * SEL_BLOCK + lax.broadcasted_iota(I32, (1, SEL_BLOCK), 1)
            s = _dot_nt(q, tile) - slope * (past_len - pos).astype(F32)
            tiles.append(tile)
            scores.append(s)
        m = s_new
        for s in scores:
            m = jnp.maximum(m, jnp.max(s, axis=-1, keepdims=True))
        p_new = jnp.exp(s_new - m)
        den = p_new
        acc = p_new * new_sel
        for s, tile in zip(scores, tiles):
            p = jnp.exp(s - m)
            den = den + jnp.sum(p, axis=-1, keepdims=True)
            acc = acc + jnp.dot(p.astype(BF16), tile, preferred_element_type=F32)
        o_sel = acc / jnp.maximum(den, 1e-30)
        win = win_ref[0].astype(BF16)
        dist = w_buf - lax.broadcasted_iota(I32, (1, w_buf), 1)
        ok = dist < WINDOW
        s = jnp.where(ok, _dot_nt(q, win) - slope * dist.astype(F32), NEG)
        s_new = jnp.sum(qf * new_win, axis=-1, keepdims=True)
        m = jnp.maximum(s_new, jnp.max(s, axis=-1, keepdims=True))
        p = jnp.where(ok, jnp.exp(s - m), 0.0)
        p_new = jnp.exp(s_new - m)
        den = p_new + jnp.sum(p, axis=-1, keepdims=True)
        o_win = (p_new * new_win + jnp.dot(p.astype(BF16), win, preferred_element_type=F32)) / jnp.maximum(den, 1e-30)
        o_ref[0, g] = gates[:, 0:1] * ocmp_ref[0] + gates[:, 1:2] * o_sel + gates[:, 2:3] * o_win


def _sample_attend(cache2, page_table, ids, state_win, kv_new, q_sel, ocmp, gates, past_len):
    n_seq, n_pages = page_table.shape
    w_buf = state_win.shape[1]
    ids3 = ids.reshape(n_seq, N_KV, SEL_TOPK)
    phys = jnp.take_along_axis(page_table[:, None, :], ids3 // 2, axis=-1) * 2 + ids3 % 2
    phys = jnp.pad(phys, ((0, 1), (0, 0), (0, 0))).astype(I32)

    def blk_map(g, t):
        return lambda b, ph, idr: (ph[(b * N_KV + g) * SEL_TOPK + t], 0, 1)

    grid_spec = pltpu.PrefetchScalarGridSpec(
        num_scalar_prefetch=2,
        grid=(n_seq,),
        in_specs=[pl.BlockSpec((1, SEL_BLOCK, 2 * LANES), blk_map(g, t))
                  for g in range(N_KV) for t in range(SEL_PAST)] + [
            pl.BlockSpec((1, w_buf, 2 * LANES), lambda b, pt, idr: (b, 0, 0)),
            pl.BlockSpec((1, 1, KV_COLS), lambda b, pt, idr: (b, 0, 0)),
            pl.BlockSpec((1, N_KV, N_HEADS, 2 * LANES), lambda b, pt, idr: (b, 0, 0, 0)),
            pl.BlockSpec((1, N_HEADS, 2 * LANES), lambda b, pt, idr: (b, 0, 0)),
            pl.BlockSpec((1, N_HEADS, LANES), lambda b, pt, idr: (b, 0, 0))],
        out_specs=pl.BlockSpec((1, N_KV, N_HEADS, 2 * LANES), lambda b, pt, idr: (b, 0, 0, 0)),
    )
    return pl.pallas_call(
        functools.partial(_sample_attend_kernel, past_len=past_len),
        grid_spec=grid_spec,
        out_shape=jax.ShapeDtypeStruct((n_seq, N_KV, N_HEADS, 2 * LANES), F32),
        compiler_params=_cparams(("arbitrary",)),
        name="sample_attend",
    )(phys.reshape(-1), ids.reshape(-1), *([cache2] * (N_KV * SEL_PAST)), state_win, kv_new, q_sel,
      ocmp, gates)


def _nsa_sample(cache_l, page_table, state_win_l, kv_new, qt, gt, pe, w1, w2):
    n_seq, n_pages = page_table.shape
    past_len = n_pages * PAGE_SIZE
    n_phys = cache_l.shape[0]
    cache = cache_l.reshape(n_phys, PAGE_SIZE, 4 * N_KV * HEAD_DIM)
    kc, vc = _compress_sample(cache, page_table, pe, w1, w2)
    q = qt.T.reshape(n_seq, N_HEADS, HEAD_DIM)
    slopes = (2.0 ** (-8.0 * jnp.arange(1, N_HEADS + 1, dtype=F32) / N_HEADS)).reshape(1, N_HEADS, 1)
    aug = jnp.concatenate([slopes * 128.0, slopes, -(slopes * 128.0) * (past_len // 128),
                           jnp.zeros((1, N_HEADS, LANES - HEAD_DIM - 3), F32)], axis=-1)
    q_aug = jnp.concatenate([q, jnp.broadcast_to(aug, (n_seq, N_HEADS, LANES - HEAD_DIM)).astype(BF16)], axis=-1)
    ocmp, imp = _sample_select(q_aug, kc, vc)
    ids = _sample_topk(imp.reshape(n_seq * N_KV, -1), SEL_PAST)[:, :SEL_TOPK]
    head_group = (jnp.arange(N_HEADS) // GROUP).reshape(1, 1, N_HEADS, 1)
    lane_group = (jnp.arange(2 * LANES) // HEAD_DIM).reshape(1, 1, 1, 2 * LANES)
    q_rep = jnp.tile(q, (1, 1, 2 * LANES // HEAD_DIM))[:, None]
    g_idx = jnp.arange(N_KV).reshape(1, N_KV, 1, 1)
    q_sel = jnp.where((head_group == g_idx) & (lane_group == g_idx), q_rep, 0).astype(BF16)
    gates = gt[:3 * N_HEADS].reshape(3, N_HEADS, n_seq).transpose(2, 1, 0)
    gates = jnp.pad(gates, ((0, 0), (0, 0), (0, LANES - 3)))
    cache2 = cache_l.reshape(n_phys * 2, SEL_BLOCK, 4 * N_KV * HEAD_DIM)
    win = state_win_l.reshape(n_seq, state_win_l.shape[1], 2 * N_KV * HEAD_DIM)
    ocmp = jnp.tile(ocmp[:, :, :HEAD_DIM], (1, 1, 2 * LANES // HEAD_DIM))
    o = _sample_attend(cache2, page_table, ids, win, kv_new.reshape(n_seq, 1, KV_COLS), q_sel, ocmp, gates,
                       past_len)
    parts = []
    for h in range(N_HEADS):
        g = h // GROUP
        v0 = LANES + g * HEAD_DIM
        parts.append(o[:, g, h, v0:v0 + HEAD_DIM])
    return jnp.concatenate(parts, axis=-1).astype(BF16)


def _sg_chunk_params(sg_w, sg_b):
    w = jnp.tril(sg_w).astype(BF16)
    bias = jnp.repeat(sg_b.T, SG_DIM, axis=1)
    return w, bias


def _pad_rows(a, mult):
    pad = (-a.shape[0]) % mult
    return jnp.pad(a, ((0, pad),) + ((0, 0),) * (a.ndim - 1)) if pad else a


def _prep_weights(norm_mix_g, norm_ffn_g, norm_final_g, w_in, cmp_pe, cmp_w1, cmp_w2, sg_norm_g, sg_norm_b,
                  sg_w, sg_b, w_branch_nsa, w_branch_sg, w_out, ffn_w_gu, ffn_w_down, router_w, router_b,
                  moe_w_gu, moe_w_down):
    depth = w_in.shape[0]
    layers = []
    for i in range(depth):
        wn, wt = _pack_w_in(w_in[i])
        sgw_chunk, sgb_chunk = _sg_chunk_params(sg_w[i], sg_b[i])
        lw = {
            "norm_mix": norm_mix_g[i].reshape(1, D_MODEL), "norm_ffn": norm_ffn_g[i].reshape(1, D_MODEL),
            "norm_final": norm_final_g.reshape(1, D_MODEL), "wn": wn, "wt": wt,
            "sg_norm_g": sg_norm_g[i].reshape(1, D_SG), "sg_norm_b": sg_norm_b[i].reshape(1, D_SG),
            "cmp_pe": cmp_pe[i], "cmp_w1": cmp_w1[i], "cmp_w2": cmp_w2[i],
            "wa": w_branch_nsa[i].astype(BF16), "wb": w_branch_sg[i].astype(BF16), "wo": w_out[i].astype(BF16),
            "sgw_chunk": sgw_chunk, "sgb_chunk": sgb_chunk,
            "sgw_first": jnp.repeat(sg_w[i][:, 0, 0], SG_DIM).reshape(1, D_SG),
            "sgb_first": jnp.repeat(sg_b[i][:, 0], SG_DIM).reshape(1, D_SG),
            "routed": i % 2 == 1, "final": i == depth - 1,
        }
        if i % 2 == 0:
            lw["ffn_a"], lw["ffn_b"], lw["ffn_d"] = _pack_dense_ffn(ffn_w_gu[i // 2], ffn_w_down[i // 2])
            lw["router_w"] = jnp.zeros((D_MODEL, LANES), F32)
            lw["router_b"] = jnp.zeros((1, LANES), F32)
        else:
            lw["ffn_a"], lw["ffn_b"], lw["ffn_d"] = _pack_moe_ffn(moe_w_gu[i // 2], moe_w_down[i // 2])
            lw["router_w"] = jnp.pad(router_w[i // 2], ((0, 0), (0, LANES - N_EXPERTS)))
            lw["router_b"] = jnp.pad(router_b[i // 2], (0, LANES - N_EXPERTS)).reshape(1, LANES)
        layers.append(lw)
    return layers


def _prompt_layer(x, mod, lw, tm):
    sh1, sc1, g1, sh2, sc2, g2 = mod
    kv, u, v, ga, gb, kaug, qt, vt, gt = _in_proj(x, lw["norm_mix"], sc1, sh1, lw["wn"], lw["wt"],
                                                   lw["sg_norm_g"], lw["sg_norm_b"], tm)
    onsa = _nsa_prompt(kv, kaug, qt, vt, gt, lw["cmp_pe"], lw["cmp_w1"], lw["cmp_w2"])
    x = _mix_out(x, onsa, u, v, ga, gb, g1, lw["wa"], lw["wb"], lw["wo"], lw["sgw_chunk"], lw["sgb_chunk"],
                 tm, True)
    x = _ffn(x, lw["norm_ffn"], sc2, sh2, g2, lw["ffn_a"], lw["ffn_b"], lw["ffn_d"], lw["router_w"],
             lw["router_b"], lw["norm_final"], lw["ffn_tm"], lw["routed"], lw["final"])
    return x, kv


def _sample_layer(x, mod, lw, cache_l, page_table, state_win_l):
    sh1, sc1, g1, sh2, sc2, g2 = mod
    tm = x.shape[0]
    kv, u, v, ga, gb, _, qt, _, gt = _in_proj(x, lw["norm_mix"], sc1, sh1, lw["wn"], lw["wt"],
                                              lw["sg_norm_g"], lw["sg_norm_b"], tm)
    onsa = _nsa_sample(cache_l, page_table, state_win_l, kv, qt, gt, lw["cmp_pe"], lw["cmp_w1"], lw["cmp_w2"])
    x = _mix_out(x, onsa, u, v, ga, gb, g1, lw["wa"], lw["wb"], lw["wo"], lw["sgw_first"], lw["sgb_first"],
                 tm, False)
    x = _ffn(x, lw["norm_ffn"], sc2, sh2, g2, lw["ffn_a"], lw["ffn_b"], lw["ffn_d"], lw["router_w"],
             lw["router_b"], lw["norm_final"], tm, lw["routed"], lw["final"])
    return x, kv, v


PROMPT_ROW_TILE = 512
PROMPT_FFN_ROW_TILE = 512


def kernel(x_prompt, x_sample, cache_kv, state_win, page_table, c_prompt, c_sample, norm_mix_g, norm_ffn_g,
           norm_final_g, w_ada, b_ada, w_in, cmp_pe, cmp_w1, cmp_w2, sg_norm_g, sg_norm_b, sg_w, sg_b,
           w_branch_nsa, w_branch_sg, w_out, ffn_w_gu, ffn_w_down, router_w, router_b, moe_w_gu, moe_w_down):
    batch, seq, _ = x_prompt.shape
    n_seq, dec_seq, _ = x_sample.shape
    depth = w_in.shape[0]
    assert batch == 1 and dec_seq == 1
    assert seq % PROMPT_FFN_ROW_TILE == 0 and seq // Q_BLOCK <= 256
    past_len = page_table.shape[1] * PAGE_SIZE
    assert past_len % CHUNK == 0 and past_len // LANES <= 256
    assert state_win.shape[2] == WINDOW and page_table.shape[1] % PAGES_PER_STEP == 0

    layers = _prep_weights(norm_mix_g, norm_ffn_g, norm_final_g, w_in, cmp_pe, cmp_w1, cmp_w2, sg_norm_g,
                           sg_norm_b, sg_w, sg_b, w_branch_nsa, w_branch_sg, w_out, ffn_w_gu, ffn_w_down,
                           router_w, router_b, moe_w_gu, moe_w_down)
    c_all = _pad_rows(jnp.concatenate([c_prompt, c_sample], axis=0), SUBLANES)
    mods = _ada(c_all, w_ada, b_ada)

    xp = x_prompt[0]
    xs = x_sample[:, 0]
    kv_p, kv_s, win_p, win_s, sgv_s = [], [], [], [], []
    w_keep = min(WINDOW, seq)
    for i in range(depth):
        lw = dict(layers[i])
        lw["ffn_tm"] = PROMPT_FFN_ROW_TILE
        mod_p = tuple(mods[i, 0:1, j * D_MODEL:(j + 1) * D_MODEL] for j in range(6))
        mod_s = tuple(mods[i, 1:1 + n_seq, j * D_MODEL:(j + 1) * D_MODEL] for j in range(6))
        xp, kvp = _prompt_layer(xp, mod_p, lw, PROMPT_ROW_TILE)
        xs, kvs, v_s = _sample_layer(xs, mod_s, lw, cache_kv[i], page_table, state_win[i])
        kv_rows = 4 * N_KV * HEAD_DIM
        kv_p.append(kvp[:, :kv_rows].reshape(1, seq, 4, N_KV, HEAD_DIM))
        win_p.append(kvp[seq - w_keep:, kv_rows:].reshape(1, w_keep, 2, N_KV, HEAD_DIM))
        kv_s.append(kvs[:, :kv_rows].reshape(n_seq, 1, 4, N_KV, HEAD_DIM))
        new_win = kvs[:, kv_rows:].reshape(n_seq, 1, 2, N_KV, HEAD_DIM)
        win_s.append(jnp.concatenate([state_win[i][:, 1:], new_win], axis=1))
        sgv_s.append(v_s.reshape(n_seq, 1, D_SG))
    return (xp[None], xs[:, None], jnp.stack(kv_p), jnp.stack(kv_s), jnp.stack(win_p), jnp.stack(win_s),
            jnp.stack(sgv_s))
```

```python
import functools

import jax
import jax.numpy as jnp
from jax import lax
from jax.experimental import pallas as pl
from jax.experimental.pallas import tpu as pltpu

F32 = jnp.float32
BF16 = jnp.bfloat16
I32 = jnp.int32
HIGHEST = lax.Precision.HIGHEST

LANES = 128
SUBLANES = 8
VMEM_LIMIT_BYTES = 56 * 1024 * 1024

D_MODEL = 1024
N_HEADS = 8
HEAD_DIM = 64
N_KV = 2
GROUP = N_HEADS // N_KV
D_NSA = N_HEADS * HEAD_DIM
CMP_BLOCK = 32
CMP_HIDDEN = 256
SEL_BLOCK = 64
SEL_TOPK = 16
WINDOW = 512
Q_BLOCK = 128
N_SG = 8
SG_DIM = 64
D_SG = N_SG * SG_DIM
CHUNK = 128
N_EXPERTS = 8
D_FF_CHUNK = 1408
PAGE_SIZE = 128
EPS = 1e-6
NEG = -1e30
FORCED = 1e9
REMOVED = -3e38

KV_COLS = 6 * N_KV * HEAD_DIM
OFF_KV = D_NSA
OFF_G = OFF_KV + KV_COLS
OFF_U = OFF_G + 3 * N_HEADS
OFF_V = OFF_U + D_SG
OFF_GA = OFF_V + D_SG
OFF_GB = OFF_GA + D_MODEL
IN_COLS = OFF_GB + D_MODEL

WN_U = 0
WN_V = WN_U + D_SG
WN_GA = WN_V + D_SG
WN_GB = WN_GA + D_MODEL
WN_KAUG = WN_GB + D_MODEL
WN_KV = WN_KAUG + 4 * LANES
WN_COLS = WN_KV + KV_COLS
WT_Q = 0
WT_KV = D_NSA
WT_G = WT_KV + KV_COLS
WT_ROWS = WT_G + 32

POS_HI_LANE = HEAD_DIM
POS_LO_LANE = HEAD_DIM + 1
ONE_LANE = HEAD_DIM + 2


def _cparams(sem):
    return pltpu.CompilerParams(dimension_semantics=sem, vmem_limit_bytes=VMEM_LIMIT_BYTES)


def _bdot(a, b):
    return jnp.dot(a.astype(BF16), b.astype(BF16), preferred_element_type=F32)


def _dot_nt(a, b):
    return lax.dot_general(a.astype(BF16), b.astype(BF16), (((1,), (1,)), ((), ())),
                           preferred_element_type=F32)


def _silu(x):
    return x * jax.nn.sigmoid(x)


def _pos_aug(pos, lane):
    hi = (pos >> 7).astype(F32)
    lo = (pos & 127).astype(F32)
    return jnp.where(lane == POS_HI_LANE, hi,
                     jnp.where(lane == POS_LO_LANE, lo,
                               jnp.where(lane == ONE_LANE, 1.0, 0.0)))


def _group_slopes(g, lane_head):
    out = jnp.zeros(lane_head.shape, F32)
    for gg in range(N_KV):
        for hh in range(GROUP):
            s = 2.0 ** (-8.0 * (gg * GROUP + hh + 1) / N_HEADS)
            out = jnp.where((lane_head == hh) & (g == gg), s, out)
    return out


def _qt_aug(qt_ref, g, q0_blocks):
    heads = [qt_ref[h * HEAD_DIM:(h + 1) * HEAD_DIM, :] for h in range(GROUP)]
    q = jnp.concatenate(heads, axis=1)
    n = GROUP * Q_BLOCK
    row = lax.broadcasted_iota(I32, (LANES - HEAD_DIM, n), 0)
    lane_head = lax.broadcasted_iota(I32, (LANES - HEAD_DIM, n), 1) // Q_BLOCK
    slope = _group_slopes(g, lane_head)
    q0f = (q0_blocks).astype(F32)
    aug = jnp.where(row == 0, slope * 128.0,
                    jnp.where(row == 1, slope,
                              jnp.where(row == 2, -(slope * 128.0) * q0f, 0.0)))
    return jnp.concatenate([q.astype(BF16), aug.astype(BF16)], axis=0)


def _ada_kernel(c_ref, w_ref, b_ref, o_ref):
    c = c_ref[...]
    o_ref[0] = jnp.dot(_silu(c), w_ref[0], precision=HIGHEST, preferred_element_type=F32) + b_ref[0]


def _ada(c_all, w_ada, b_ada):
    depth = w_ada.shape[0]
    rows = c_all.shape[0]
    tn = 1024
    n = w_ada.shape[2]
    return pl.pallas_call(
        _ada_kernel,
        grid=(depth, n // tn),
        in_specs=[pl.BlockSpec((rows, D_MODEL), lambda l, j: (0, 0)),
                  pl.BlockSpec((1, D_MODEL, tn), lambda l, j: (l, 0, j)),
                  pl.BlockSpec((1, 1, tn), lambda l, j: (l, 0, j))],
        out_specs=pl.BlockSpec((1, rows, tn), lambda l, j: (l, 0, j)),
        out_shape=jax.ShapeDtypeStruct((depth, rows, n), F32),
        compiler_params=_cparams(("arbitrary", "arbitrary")),
        name="ada",
    )(c_all, w_ada, b_ada.reshape(depth, 1, n))


def _in_proj_kernel(x_ref, g_ref, sc_ref, sh_ref, wn_ref, wt_ref, lng_ref, lnb_ref, *out_refs, tm, natural_kv):
    if natural_kv:
        kvt_ref, u_ref, v_ref, ga_ref, gb_ref, kaug_ref, qt_ref, vt_ref, gt_ref, kv_ref = out_refs
    else:
        kvt_ref, u_ref, v_ref, ga_ref, gb_ref, kaug_ref, qt_ref, vt_ref, gt_ref = out_refs
    i = pl.program_id(0)
    x = x_ref[...]
    ms = jnp.mean(x * x, axis=-1, keepdims=True)
    h = x * lax.rsqrt(ms + EPS) * g_ref[...]
    h = h * (1.0 + sc_ref[...]) + sh_ref[...]
    hb = h.astype(BF16)

    def seg(a, b):
        return jnp.dot(hb, wn_ref[:, a:b], preferred_element_type=F32)

    u_ref[...] = jax.nn.gelu(seg(WN_U, WN_V)).astype(BF16)
    v = jax.nn.gelu(seg(WN_V, WN_GA))
    mu = jnp.mean(v, axis=-1, keepdims=True)
    var = jnp.mean(jnp.square(v - mu), axis=-1, keepdims=True)
    v_ref[...] = (v - mu) * lax.rsqrt(var + EPS) * lng_ref[...] + lnb_ref[...]
    ga_ref[...] = jax.nn.sigmoid(seg(WN_GA, WN_GB)).astype(BF16)
    gb_ref[...] = jax.nn.sigmoid(seg(WN_GB, WN_KAUG)).astype(BF16)

    pos = lax.broadcasted_iota(I32, (tm, LANES), 0) + i * tm
    lane = lax.broadcasted_iota(I32, (tm, LANES), 1)
    aug = _pos_aug(pos, lane)
    for j in range(4):
        k = seg(WN_KAUG + j * LANES, WN_KAUG + (j + 1) * LANES)
        kaug_ref[j] = (k + aug).astype(BF16)
    if natural_kv:
        kv_ref[...] = seg(WN_KV, WN_COLS)

    zt = _dot_nt(wt_ref[...], hb)
    qt_ref[...] = (zt[WT_Q:WT_KV] * (HEAD_DIM ** -0.5)).astype(BF16)
    kvt = zt[WT_KV:WT_G]
    kvt_ref[...] = kvt
    two = N_KV * HEAD_DIM
    vt_ref[0:two, :] = kvt[3 * two:4 * two].astype(BF16)
    vt_ref[two:2 * two, :] = kvt[5 * two:6 * two].astype(BF16)
    gt_ref[...] = jax.nn.sigmoid(zt[WT_G:WT_ROWS])


def _in_proj(x, g, sc, sh, wn, wt, lng, lnb, tm, natural_kv):
    m = x.shape[0]
    mod_rows = sc.shape[0]
    mod_block = (1, D_MODEL) if mod_rows == 1 else (tm, D_MODEL)
    mod_map = (lambda i: (0, 0)) if mod_rows == 1 else (lambda i: (i, 0))
    row = lambda i: (i, 0)
    col = lambda i: (0, i)
    const = lambda i: (0, 0)
    out_shape = [
        jax.ShapeDtypeStruct((KV_COLS, m), F32),
        jax.ShapeDtypeStruct((m, D_SG), BF16),
        jax.ShapeDtypeStruct((m, D_SG), F32),
        jax.ShapeDtypeStruct((m, D_MODEL), BF16),
        jax.ShapeDtypeStruct((m, D_MODEL), BF16),
        jax.ShapeDtypeStruct((4, m, LANES), BF16),
        jax.ShapeDtypeStruct((D_NSA, m), BF16),
        jax.ShapeDtypeStruct((4 * HEAD_DIM, m), BF16),
        jax.ShapeDtypeStruct((32, m), F32),
    ]
    out_specs = [
        pl.BlockSpec((KV_COLS, tm), col),
        pl.BlockSpec((tm, D_SG), row),
        pl.BlockSpec((tm, D_SG), row),
        pl.BlockSpec((tm, D_MODEL), row),
        pl.BlockSpec((tm, D_MODEL), row),
        pl.BlockSpec((4, tm, LANES), lambda i: (0, i, 0)),
        pl.BlockSpec((D_NSA, tm), col),
        pl.BlockSpec((4 * HEAD_DIM, tm), col),
        pl.BlockSpec((32, tm), col),
    ]
    if natural_kv:
        out_shape.append(jax.ShapeDtypeStruct((m, KV_COLS), F32))
        out_specs.append(pl.BlockSpec((tm, KV_COLS), row))
    return pl.pallas_call(
        functools.partial(_in_proj_kernel, tm=tm, natural_kv=natural_kv),
        grid=(m // tm,),
        in_specs=[pl.BlockSpec((tm, D_MODEL), row),
                  pl.BlockSpec((1, D_MODEL), const),
                  pl.BlockSpec(mod_block, mod_map),
                  pl.BlockSpec(mod_block, mod_map),
                  pl.BlockSpec((D_MODEL, WN_COLS), const),
                  pl.BlockSpec((WT_ROWS, D_MODEL), const),
                  pl.BlockSpec((1, D_SG), const),
                  pl.BlockSpec((1, D_SG), const)],
        out_specs=tuple(out_specs),
        out_shape=tuple(out_shape),
        compiler_params=_cparams(("arbitrary",)),
        name="in_proj",
    )(x, g, sc, sh, wn, wt, lng, lnb)


def _pack_w_in(w):
    kv = w[:, OFF_KV:OFF_G]

    def kvcol(j, g):
        return kv[:, (j * N_KV + g) * HEAD_DIM:(j * N_KV + g + 1) * HEAD_DIM]

    zpad = jnp.zeros((D_MODEL, LANES - HEAD_DIM), w.dtype)
    kaug = [jnp.concatenate([kvcol(j, g), zpad], axis=1) for j in (2, 4) for g in range(N_KV)]
    wn = jnp.concatenate([w[:, OFF_U:OFF_V], w[:, OFF_V:OFF_GA], w[:, OFF_GA:OFF_GB],
                          w[:, OFF_GB:IN_COLS]] + kaug + [kv], axis=1)
    gpad = jnp.zeros((D_MODEL, 32 - 3 * N_HEADS), w.dtype)
    wt = jnp.concatenate([w[:, :OFF_KV], kv, w[:, OFF_G:OFF_U], gpad], axis=1).T
    return wn.astype(BF16), wt.astype(BF16)


def _cmp_block_end(nc):
    r = lax.broadcasted_iota(I32, (nc, LANES), 0)
    half = nc // 2
    c = 2 * (r % half) + r // half
    return c * CMP_BLOCK + (CMP_BLOCK - 1)


def _compress_kernel(xk_ref, xv_ref, pe_ref, w1_ref, w2k_ref, w2vt_ref, kc_ref, vct_ref, *, nc):
    xk = xk_ref[0, 0] + pe_ref[0]
    hk = _silu(_bdot(xk, w1_ref[0]))
    kc = _bdot(hk, w2k_ref[...])
    lane = lax.broadcasted_iota(I32, (nc, LANES), 1)
    kc_ref[0] = (kc + _pos_aug(_cmp_block_end(nc), lane)).astype(BF16)
    xv = xv_ref[0, 0] + pe_ref[1]
    hv = _silu(_bdot(xv, w1_ref[1]))
    vct_ref[0] = _dot_nt(w2vt_ref[...], hv).astype(BF16)


def _compress_prompt(kvt, pe, w1, w2):
    m = kvt.shape[1]
    nc = m // CMP_BLOCK
    half = nc // 2
    kvc = kvt[:2 * N_KV * HEAD_DIM].reshape(2, N_KV, HEAD_DIM, half, 2, CMP_BLOCK)
    x = kvc.transpose(0, 1, 4, 3, 5, 2).reshape(2, N_KV, nc, CMP_BLOCK * HEAD_DIM)
    pe_flat = pe.reshape(2, 1, CMP_BLOCK * HEAD_DIM)
    w2k = jnp.concatenate([w2[0], jnp.zeros((CMP_HIDDEN, LANES - HEAD_DIM), w2.dtype)], axis=1).astype(BF16)
    w2vt = w2[1].T.astype(BF16)
    kd = CMP_BLOCK * HEAD_DIM
    return pl.pallas_call(
        functools.partial(_compress_kernel, nc=nc),
        grid=(N_KV,),
        in_specs=[pl.BlockSpec((1, 1, nc, kd), lambda g: (0, g, 0, 0)),
                  pl.BlockSpec((1, 1, nc, kd), lambda g: (1, g, 0, 0)),
                  pl.BlockSpec((2, 1, kd), lambda g: (0, 0, 0)),
                  pl.BlockSpec((2, kd, CMP_HIDDEN), lambda g: (0, 0, 0)),
                  pl.BlockSpec((CMP_HIDDEN, LANES), lambda g: (0, 0)),
                  pl.BlockSpec((HEAD_DIM, CMP_HIDDEN), lambda g: (0, 0))],
        out_specs=(pl.BlockSpec((1, nc, LANES), lambda g: (g, 0, 0)),
                   pl.BlockSpec((1, HEAD_DIM, nc), lambda g: (g, 0, 0))),
        out_shape=(jax.ShapeDtypeStruct((N_KV, nc, LANES), BF16),
                   jax.ShapeDtypeStruct((N_KV, HEAD_DIM, nc), BF16)),
        compiler_params=_cparams(("arbitrary",)),
        name="compress_prompt",
    )(x, x, pe_flat, w1.astype(BF16), w2k, w2vt)


def _topk_mask_rows(imp, k):
    rows = lax.broadcasted_iota(I32, imp.shape, 0)
    big = jnp.int32(2 ** 30)

    def body(_, carry):
        v, sel = carry
        m = jnp.max(v, axis=0, keepdims=True)
        first = jnp.min(jnp.where(v == m, rows, big), axis=0, keepdims=True)
        hit = rows == first
        return jnp.where(hit, REMOVED, v), jnp.where(hit, 1.0, sel)

    _, sel = lax.fori_loop(0, k, body, (imp, jnp.zeros(imp.shape, F32)))
    return sel


def _visited_blocks(sel):
    ns = sel.shape[0]
    ones_q = jnp.ones((SUBLANES, sel.shape[1]), BF16)
    flags = (_dot_nt(ones_q, sel) > 0.5).astype(F32)
    r = lax.broadcasted_iota(I32, (ns, ns), 0)
    c = lax.broadcasted_iota(I32, (ns, ns), 1)
    upper = jnp.where(r <= c, 1.0, 0.0)
    prefix = _bdot(flags, upper)
    before = jnp.where(prefix[0:1, :] <= r.astype(F32), 1.0, 0.0)
    ids = _dot_nt(jnp.ones((SUBLANES, ns), BF16), before)
    ids = jnp.minimum(ids, ns - 1.0).astype(I32)
    total = jnp.broadcast_to(prefix[:, ns - 1:ns], (SUBLANES, ns))
    return ids, total


def _select_kernel(qt_ref, kc_ref, vct_ref, mask_ref, ids_ref, cnt_ref, ocmp_ref, *, nc):
    i = pl.program_id(0)
    ns = nc // 2
    n = GROUP * Q_BLOCK
    q0 = i * Q_BLOCK
    cend = _cmp_block_end(nc)[:, :1]
    qpos = q0 + lax.broadcasted_iota(I32, (1, n), 1) % Q_BLOCK
    valid = cend <= qpos
    blk = lax.broadcasted_iota(I32, (ns, Q_BLOCK), 0)
    cur = (q0 + lax.broadcasted_iota(I32, (ns, Q_BLOCK), 1)) // SEL_BLOCK
    for g in range(N_KV):
        qa = _qt_aug(qt_ref.at[g * GROUP * HEAD_DIM:(g + 1) * GROUP * HEAD_DIM], g, i)
        s = jnp.dot(kc_ref[g], qa, preferred_element_type=F32)
        s = jnp.where(valid, s, NEG)
        m = jnp.max(s, axis=0, keepdims=True)
        p = jnp.where(valid, jnp.exp(s - m), 0.0)
        p = p / jnp.maximum(jnp.sum(p, axis=0, keepdims=True), 1e-30)
        ocmp_ref[0, g] = jnp.dot(vct_ref[g], p.astype(BF16), preferred_element_type=F32)
        ph = p[:, 0:Q_BLOCK]
        for h in range(1, GROUP):
            ph = ph + p[:, h * Q_BLOCK:(h + 1) * Q_BLOCK]
        imp = ph[:ns] + ph[ns:]
        forced = (blk == 0) | (blk == cur) | (blk == cur - 1)
        imp = jnp.where(forced, FORCED, imp)
        imp = jnp.where(blk <= cur, imp, NEG)
        sel = _topk_mask_rows(imp, min(SEL_TOPK, ns))
        sel = jnp.where(blk <= cur, sel, 0.0)
        mask_ref[0, g] = sel
        ids_ref[0, g], cnt_ref[0, g] = _visited_blocks(sel)


def _select(qt, kc, vct):
    m = qt.shape[1]
    nqb = m // Q_BLOCK
    nc = m // CMP_BLOCK
    ns = nc // 2
    n = GROUP * Q_BLOCK
    return pl.pallas_call(
        functools.partial(_select_kernel, nc=nc),
        grid=(nqb,),
        in_specs=[pl.BlockSpec((D_NSA, Q_BLOCK), lambda i: (0, i)),
                  pl.BlockSpec((N_KV, nc, LANES), lambda i: (0, 0, 0)),
                  pl.BlockSpec((N_KV, HEAD_DIM, nc), lambda i: (0, 0, 0))],
        out_specs=(pl.BlockSpec((1, N_KV, ns, Q_BLOCK), lambda i: (i, 0, 0, 0)),
                   pl.BlockSpec((1, N_KV, SUBLANES, ns), lambda i: (i, 0, 0, 0)),
                   pl.BlockSpec((1, N_KV, SUBLANES, ns), lambda i: (i, 0, 0, 0)),
                   pl.BlockSpec((1, N_KV, HEAD_DIM, n), lambda i: (i, 0, 0, 0))),
        out_shape=(jax.ShapeDtypeStruct((nqb, N_KV, ns, Q_BLOCK), F32),
                   jax.ShapeDtypeStruct((nqb, N_KV, SUBLANES, ns), I32),
                   jax.ShapeDtypeStruct((nqb, N_KV, SUBLANES, ns), F32),
                   jax.ShapeDtypeStruct((nqb, N_KV, HEAD_DIM, n), F32)),
        compiler_params=_cparams(("arbitrary",)),
        name="nsa_select",
    )(qt, kc, vct)


SEL_PER_STEP = 4
WIN_TILES_PER_STEP = 2


def _online_update(state, s, valid, vt_tiles):
    m, l, acc = state
    s = jnp.where(valid, s, NEG)
    m_new = jnp.maximum(m, jnp.max(s, axis=0, keepdims=True))
    alpha = jnp.exp(m - m_new)
    p = jnp.where(valid, jnp.exp(s - m_new), 0.0)
    l = alpha * l + jnp.sum(p, axis=0, keepdims=True)
    pb = p.astype(BF16)
    acc = alpha * acc
    r0 = 0
    for vt in vt_tiles:
        rows = vt.shape[1]
        acc = acc + jnp.dot(vt, pb[r0:r0 + rows], preferred_element_type=F32)
        r0 += rows
    return m_new, l, acc


def _attend_kernel(cnt_ref, ids_ref, qt_ref, ksel_ref, vsel_ref, kwin_ref, vwin_ref, mask_ref,
                   ocmp_ref, gt_ref, o_ref, *, nqb, ns):
    g = pl.program_id(0)
    i = pl.program_id(1)
    n = GROUP * Q_BLOCK
    q0 = i * Q_BLOCK
    qa = _qt_aug(qt_ref, g, i)
    qpos = q0 + lax.broadcasted_iota(I32, (1, n), 1) % Q_BLOCK
    init = (jnp.full((1, n), NEG, F32), jnp.zeros((1, n), F32), jnp.zeros((HEAD_DIM, n), F32))

    slot = g * nqb + i
    kpos_s = lax.broadcasted_iota(I32, (SEL_BLOCK, 1), 0)

    count = cnt_ref[slot]

    def sel_body(t, state):
        ks, vts, valids = [], [], []
        for u in range(SEL_PER_STEP):
            e = t * SEL_PER_STEP + u
            b = ids_ref[slot * ns + e]
            ks.append(ksel_ref[0, pl.ds(pl.multiple_of(b * SEL_BLOCK, SEL_BLOCK), SEL_BLOCK), :])
            row = mask_ref[0, 0, pl.ds(b, 1), :]
            chosen = (jnp.concatenate([row] * GROUP, axis=1) > 0.5) & (e < count)
            valids.append(chosen & (kpos_s + b * SEL_BLOCK <= qpos))
            vts.append(vsel_ref[0, b])
        s = jnp.dot(jnp.concatenate(ks, axis=0), qa, preferred_element_type=F32)
        return _online_update(state, s, jnp.concatenate(valids, axis=0), vts)

    steps = (count + (SEL_PER_STEP - 1)) // SEL_PER_STEP
    _, l_s, acc_s = lax.fori_loop(0, steps, sel_body, init)
    o_sel = acc_s / jnp.maximum(l_s, 1e-30)

    kpos_w = lax.broadcasted_iota(I32, (Q_BLOCK, 1), 0)
    nwin = WINDOW // Q_BLOCK + 1
    state = init
    for j0 in range(0, nwin, WIN_TILES_PER_STEP):
        ks, vts, valids = [], [], []
        for j in range(j0, min(j0 + WIN_TILES_PER_STEP, nwin)):
            tile = i - (nwin - 1) + j
            held = jnp.maximum(tile, 0)
            ks.append(kwin_ref[0, pl.ds(pl.multiple_of(held * Q_BLOCK, Q_BLOCK), Q_BLOCK), :])
            kpos = kpos_w + tile * Q_BLOCK
            dist = qpos - kpos
            valids.append((dist >= 0) & (dist < WINDOW) & (kpos >= 0))
            vts.append(vwin_ref[0, held])
        s = jnp.dot(jnp.concatenate(ks, axis=0), qa, preferred_element_type=F32)
        state = _online_update(state, s, jnp.concatenate(valids, axis=0), vts)
    _, l_w, acc_w = state
    o_win = acc_w / jnp.maximum(l_w, 1e-30)

    def gate(branch):
        rows = [gt_ref[pl.ds(branch * N_HEADS + g * GROUP + h, 1), :] for h in range(GROUP)]
        return jnp.concatenate(rows, axis=1)

    o = ocmp_ref[0, 0] * gate(0) + o_sel * gate(1) + o_win * gate(2)
    for hp in range(GROUP // 2):
        pair = jnp.concatenate([o[:, (2 * hp) * Q_BLOCK:(2 * hp + 1) * Q_BLOCK],
                                o[:, (2 * hp + 1) * Q_BLOCK:(2 * hp + 2) * Q_BLOCK]], axis=0)
        o_ref[:, hp * LANES:(hp + 1) * LANES] = pair.T.astype(o_ref.dtype)


def _attend(cnt, ids, qt, kaug, vt_blocks_sel, vt_blocks_win, mask, ocmp, gt):
    m = qt.shape[1]
    nqb = m // Q_BLOCK
    ns = m // SEL_BLOCK
    n = GROUP * Q_BLOCK
    gh = GROUP * HEAD_DIM
    grid_spec = pltpu.PrefetchScalarGridSpec(
        num_scalar_prefetch=2,
        grid=(N_KV, nqb),
        in_specs=[pl.BlockSpec((gh, Q_BLOCK), lambda g, i, c, d: (g, i)),
                  pl.BlockSpec((1, m, LANES), lambda g, i, c, d: (g, 0, 0)),
                  pl.BlockSpec((1, ns, HEAD_DIM, SEL_BLOCK), lambda g, i, c, d: (g, 0, 0, 0)),
                  pl.BlockSpec((1, m, LANES), lambda g, i, c, d: (N_KV + g, 0, 0)),
                  pl.BlockSpec((1, nqb, HEAD_DIM, Q_BLOCK), lambda g, i, c, d: (g, 0, 0, 0)),
                  pl.BlockSpec((1, 1, ns, Q_BLOCK), lambda g, i, c, d: (i, g, 0, 0)),
                  pl.BlockSpec((1, 1, HEAD_DIM, n), lambda g, i, c, d: (i, g, 0, 0)),
                  pl.BlockSpec((32, Q_BLOCK), lambda g, i, c, d: (0, i))],
        out_specs=pl.BlockSpec((Q_BLOCK, gh), lambda g, i, c, d: (i, g)),
    )
    return pl.pallas_call(
        functools.partial(_attend_kernel, nqb=nqb, ns=ns),
        grid_spec=grid_spec,
        out_shape=jax.ShapeDtypeStruct((m, D_NSA), BF16),
        compiler_params=_cparams(("arbitrary", "arbitrary")),
        name="nsa_attend",
    )(cnt, ids, qt, kaug, vt_blocks_sel, kaug, vt_blocks_win, mask, ocmp, gt)


def _nsa_prompt(kvt, kaug, qt, vt, gt, pe, w1, w2):
    m = kvt.shape[1]
    nqb = m // Q_BLOCK
    ns = m // SEL_BLOCK
    kc, vct = _compress_prompt(kvt, pe, w1, w2)
    mask, ids8, cnt8, ocmp = _select(qt, kc, vct)
    ids = ids8[:, :, 0, :].transpose(1, 0, 2).reshape(-1)
    cnt = cnt8[:, :, 0, 0].T.astype(I32).reshape(-1)
    vt4 = vt.reshape(4, HEAD_DIM, m)
    vsel = vt4[:N_KV].reshape(N_KV, HEAD_DIM, ns, SEL_BLOCK).transpose(0, 2, 1, 3)
    vwin = vt4[N_KV:].reshape(N_KV, HEAD_DIM, nqb, Q_BLOCK).transpose(0, 2, 1, 3)
    return _attend(cnt, ids, qt, kaug, vsel, vwin, mask, ocmp, gt)


def _mix_kernel(x_ref, onsa_ref, u_ref, v_ref, ga_ref, gb_ref, g1_ref, wa_ref, wb_ref, wo_ref,
                sgw_ref, sgb_ref, o_ref, osg_ref, *, tm, chunked):
    if chunked:
        lane = lax.broadcasted_iota(I32, (CHUNK, LANES), 1)
        for c in range(tm // CHUNK):
            rows = slice(c * CHUNK, (c + 1) * CHUNK)
            for pr in range(N_SG // 2):
                cols = slice(pr * LANES, (pr + 1) * LANES)
                vp = v_ref[rows, cols].astype(BF16)
                a = jnp.dot(sgw_ref[2 * pr], vp, preferred_element_type=F32)
                b = jnp.dot(sgw_ref[2 * pr + 1], vp, preferred_element_type=F32)
                mix = jnp.where(lane < SG_DIM, a, b) + sgb_ref[:, cols]
                osg_ref[rows, cols] = (u_ref[rows, cols].astype(F32) * mix).astype(BF16)
    else:
        mix = v_ref[...] * sgw_ref[...] + sgb_ref[...]
        osg_ref[...] = (u_ref[...].astype(F32) * mix).astype(BF16)
    a = jnp.dot(onsa_ref[...], wa_ref[...], preferred_element_type=F32)
    b = jnp.dot(osg_ref[...], wb_ref[...], preferred_element_type=F32)
    merged = ga_ref[...].astype(F32) * a + gb_ref[...].astype(F32) * b
    y = jnp.dot(merged.astype(BF16), wo_ref[...], preferred_element_type=F32)
    o_ref[...] = x_ref[...] + g1_ref[...] * y


def _mix_out(x, onsa, u, v, ga, gb, g1, wa, wb, wo, sgw, sgb, tm, chunked):
    m = x.shape[0]
    mod_rows = g1.shape[0]
    mod_block = (1, D_MODEL) if mod_rows == 1 else (tm, D_MODEL)
    mod_map = (lambda i: (0, 0)) if mod_rows == 1 else (lambda i: (i, 0))
    row = lambda i: (i, 0)
    const2 = lambda i: (0, 0)
    if chunked:
        sg_specs = [pl.BlockSpec((N_SG, CHUNK, CHUNK), lambda i: (0, 0, 0)),
                    pl.BlockSpec((CHUNK, D_SG), const2)]
    else:
        sg_specs = [pl.BlockSpec((1, D_SG), const2), pl.BlockSpec((1, D_SG), const2)]
    return pl.pallas_call(
        functools.partial(_mix_kernel, tm=tm, chunked=chunked),
        grid=(m // tm,),
        in_specs=[pl.BlockSpec((tm, D_MODEL), row),
                  pl.BlockSpec((tm, D_NSA), row),
                  pl.BlockSpec((tm, D_SG), row),
                  pl.BlockSpec((tm, D_SG), row),
                  pl.BlockSpec((tm, D_MODEL), row),
                  pl.BlockSpec((tm, D_MODEL), row),
                  pl.BlockSpec(mod_block, mod_map),
                  pl.BlockSpec((D_NSA, D_MODEL), const2),
                  pl.BlockSpec((D_SG, D_MODEL), const2),
                  pl.BlockSpec((D_MODEL, D_MODEL), const2)] + sg_specs,
        out_specs=pl.BlockSpec((tm, D_MODEL), row),
        out_shape=jax.ShapeDtypeStruct((m, D_MODEL), F32),
        scratch_shapes=[pltpu.VMEM((tm, D_SG), BF16)],
        compiler_params=_cparams(("arbitrary",)),
        name="mix_out",
    )(x, onsa, u, v, ga, gb, g1, wa, wb, wo, sgw, sgb)


def _top2_combine(logits):
    lane = lax.broadcasted_iota(I32, logits.shape, 1)
    big = jnp.int32(2 ** 30)
    z = jnp.where(lane < N_EXPERTS, logits, -jnp.inf)
    t1 = jnp.max(z, axis=-1, keepdims=True)
    i1 = jnp.min(jnp.where(z == t1, lane, big), axis=-1, keepdims=True)
    z2 = jnp.where(lane == i1, -jnp.inf, z)
    t2 = jnp.max(z2, axis=-1, keepdims=True)
    i2 = jnp.min(jnp.where(z2 == t2, lane, big), axis=-1, keepdims=True)
    e = jnp.exp(t2 - t1)
    den = 1.0 + e
    return jnp.where(lane == i1, 1.0 / den, 0.0) + jnp.where(lane == i2, e / den, 0.0)


def _ffn_kernel(x_ref, g_ref, sc_ref, sh_ref, g2_ref, wa_ref, wb_ref, wd_ref, rw_ref, rb_ref, gf_ref,
                o_ref, h_ref, acc_ref, comb_ref, *, routed, final_norm, n_chunks):
    e = pl.program_id(1)

    @pl.when(e == 0)
    def _():
        x = x_ref[...]
        ms = jnp.mean(x * x, axis=-1, keepdims=True)
        h = x * lax.rsqrt(ms + EPS) * g_ref[...]
        h = h * (1.0 + sc_ref[...]) + sh_ref[...]
        h_ref[...] = h.astype(BF16)
        acc_ref[...] = jnp.zeros_like(acc_ref)
        if routed:
            logits = jnp.dot(h, rw_ref[...], precision=HIGHEST, preferred_element_type=F32) + rb_ref[...]
            comb = _top2_combine(logits)
            for ee in range(N_EXPERTS):
                comb_ref[ee] = jnp.broadcast_to(comb[:, ee:ee + 1], comb.shape)

    hb = h_ref[...]
    a = jnp.dot(hb, wa_ref[0], preferred_element_type=F32)
    b = jnp.dot(hb, wb_ref[0], preferred_element_type=F32)
    t = (_silu(a) * b).astype(BF16)
    y = jnp.dot(t, wd_ref[0], preferred_element_type=F32)
    if routed:
        w = comb_ref[e]
        for c in range(D_MODEL // LANES):
            cols = slice(c * LANES, (c + 1) * LANES)
            acc_ref[:, cols] += w * y[:, cols]
    else:
        acc_ref[...] += y

    @pl.when(e == n_chunks - 1)
    def _():
        out = x_ref[...] + g2_ref[...] * acc_ref[...]
        if final_norm:
            ms = jnp.mean(out * out, axis=-1, keepdims=True)
            out = out * lax.rsqrt(ms + EPS) * gf_ref[...]
        o_ref[...] = out


def _ffn(x, g, sc, sh, g2, wa, wb, wd, rw, rb, gf, tm, routed, final_norm):
    m = x.shape[0]
    n_chunks = wa.shape[0]
    mod_rows = sc.shape[0]
    mod_block = (1, D_MODEL) if mod_rows == 1 else (tm, D_MODEL)
    mod_map = (lambda i, e: (0, 0)) if mod_rows == 1 else (lambda i, e: (i, 0))
    row = lambda i, e: (i, 0)
    const2 = lambda i, e: (0, 0)
    return pl.pallas_call(
        functools.partial(_ffn_kernel, routed=routed, final_norm=final_norm, n_chunks=n_chunks),
        grid=(m // tm, n_chunks),
        in_specs=[pl.BlockSpec((tm, D_MODEL), row),
                  pl.BlockSpec((1, D_MODEL), const2),
                  pl.BlockSpec(mod_block, mod_map),
                  pl.BlockSpec(mod_block, mod_map),
                  pl.BlockSpec(mod_block, mod_map),
                  pl.BlockSpec((1, D_MODEL, D_FF_CHUNK), lambda i, e: (e, 0, 0)),
                  pl.BlockSpec((1, D_MODEL, D_FF_CHUNK), lambda i, e: (e, 0, 0)),
                  pl.BlockSpec((1, D_FF_CHUNK, D_MODEL), lambda i, e: (e, 0, 0)),
                  pl.BlockSpec((D_MODEL, LANES), const2),
                  pl.BlockSpec((1, LANES), const2),
                  pl.BlockSpec((1, D_MODEL), const2)],
        out_specs=pl.BlockSpec((tm, D_MODEL), row),
        out_shape=jax.ShapeDtypeStruct((m, D_MODEL), F32),
        scratch_shapes=[pltpu.VMEM((tm, D_MODEL), BF16),
                        pltpu.VMEM((tm, D_MODEL), F32),
                        pltpu.VMEM((N_EXPERTS, tm, LANES), F32)],
        compiler_params=_cparams(("arbitrary", "arbitrary")),
        name="ffn",
    )(x, g, sc, sh, g2, wa, wb, wd, rw, rb, gf)


def _pack_dense_ffn(w_gu, w_down):
    d_ff = w_down.shape[0]
    n = d_ff // D_FF_CHUNK
    wa = w_gu[:, :d_ff].reshape(D_MODEL, n, D_FF_CHUNK).transpose(1, 0, 2)
    wb = w_gu[:, d_ff:].reshape(D_MODEL, n, D_FF_CHUNK).transpose(1, 0, 2)
    wd = w_down.reshape(n, D_FF_CHUNK, D_MODEL)
    return wa.astype(BF16), wb.astype(BF16), wd.astype(BF16)


def _pack_moe_ffn(w_gu, w_down):
    return (w_gu[:, :, :D_FF_CHUNK].astype(BF16), w_gu[:, :, D_FF_CHUNK:].astype(BF16),
            w_down.astype(BF16))


PAGES_PER_STEP = 16
BLOCKS_PER_PAGE = PAGE_SIZE // CMP_BLOCK
CMP_COLS = 2 * N_KV * HEAD_DIM


def _compress_sample_kernel(pt_ref, *refs, n_steps):
    del pt_ref
    pages = refs[:2 * PAGES_PER_STEP]
    pe_ref, w1_ref, w2_ref, kc_ref, vc_ref, rows_ref, acc_ref = refs[2 * PAGES_PER_STEP:]
    s = pl.program_id(1)
    nb = PAGES_PER_STEP * BLOCKS_PER_PAGE
    half = nb // 2
    for j in range(2):
        for p in range(PAGES_PER_STEP):
            tile = pages[j * PAGES_PER_STEP + p][...].reshape(N_KV * HEAD_DIM, PAGE_SIZE)
            rows_ref[j, p * PAGE_SIZE:(p + 1) * PAGE_SIZE, :] = tile.T
    acc_ref[...] = jnp.zeros_like(acc_ref)
    for t in range(CMP_BLOCK):
        for j in range(2):
            even = rows_ref[j, pl.ds(t, half, stride=2 * CMP_BLOCK), :]
            odd = rows_ref[j, pl.ds(CMP_BLOCK + t, half, stride=2 * CMP_BLOCK), :]
            xj = (jnp.concatenate([even, odd], axis=0) + pe_ref[t, j]).astype(BF16)
            for g in range(N_KV):
                acc_ref[j * N_KV + g] += jnp.dot(xj, w1_ref[j, g, t], preferred_element_type=F32)
    row = lax.broadcasted_iota(I32, (nb, LANES), 0)
    lane = lax.broadcasted_iota(I32, (nb, LANES), 1)
    c = s * nb + 2 * (row % half) + row // half
    aug = _pos_aug(c * CMP_BLOCK + (CMP_BLOCK - 1), lane)
    for g in range(N_KV):
        hk = _silu(acc_ref[g]).astype(BF16)
        kc = (jnp.dot(hk, w2_ref[0], preferred_element_type=F32) + aug).astype(BF16)
        kc_ref[0, g, 0] = kc[:half]
        kc_ref[0, g, 1] = kc[half:]
        hv = _silu(acc_ref[N_KV + g]).astype(BF16)
        vc = jnp.dot(hv, w2_ref[1], preferred_element_type=F32).astype(BF16)
        vc_ref[0, g, 0] = vc[:half]
        vc_ref[0, g, 1] = vc[half:]


def _compress_sample(cache_t, layer, page_table, pe, w1, w2):
    n_seq, n_pages = page_table.shape
    n_steps = n_pages // PAGES_PER_STEP
    nb = PAGES_PER_STEP * BLOCKS_PER_PAGE
    half = nb // 2
    nc_half = n_pages * BLOCKS_PER_PAGE // 2
    pe4 = jnp.concatenate([pe, pe], axis=-1).transpose(1, 0, 2).reshape(CMP_BLOCK, 2, 1, LANES)
    w1r = w1.reshape(2, CMP_BLOCK, HEAD_DIM, CMP_HIDDEN)
    zero = jnp.zeros_like(w1r)
    w1p = jnp.stack([jnp.concatenate([w1r, zero], axis=2), jnp.concatenate([zero, w1r], axis=2)],
                    axis=1).astype(BF16)
    w2p = jnp.pad(w2, ((0, 0), (0, 0), (0, LANES - HEAD_DIM))).astype(BF16)

    def page_map(j, k):
        return lambda b, s, pt: (layer, pt[b * n_pages + s * PAGES_PER_STEP + k], j, 0, 0, 0)

    grid_spec = pltpu.PrefetchScalarGridSpec(
        num_scalar_prefetch=1,
        grid=(n_seq, n_steps),
        in_specs=[pl.BlockSpec((None, None, None, N_KV, HEAD_DIM, PAGE_SIZE), page_map(j, k))
                  for j in range(2) for k in range(PAGES_PER_STEP)] + [
            pl.BlockSpec((CMP_BLOCK, 2, 1, LANES), lambda b, s, pt: (0, 0, 0, 0)),
            pl.BlockSpec((2, N_KV, CMP_BLOCK, LANES, CMP_HIDDEN), lambda b, s, pt: (0, 0, 0, 0, 0)),
            pl.BlockSpec((2, CMP_HIDDEN, LANES), lambda b, s, pt: (0, 0, 0))],
        out_specs=(pl.BlockSpec((1, N_KV, 2, half, LANES), lambda b, s, pt: (b, 0, 0, s, 0)),
                   pl.BlockSpec((1, N_KV, 2, half, LANES), lambda b, s, pt: (b, 0, 0, s, 0))),
        scratch_shapes=[pltpu.VMEM((2, PAGES_PER_STEP * PAGE_SIZE, LANES), F32),
                        pltpu.VMEM((2 * N_KV, nb, CMP_HIDDEN), F32)],
    )
    kc, vc = pl.pallas_call(
        functools.partial(_compress_sample_kernel, n_steps=n_steps),
        grid_spec=grid_spec,
        out_shape=(jax.ShapeDtypeStruct((n_seq, N_KV, 2, nc_half, LANES), BF16),
                   jax.ShapeDtypeStruct((n_seq, N_KV, 2, nc_half, LANES), BF16)),
        compiler_params=_cparams(("arbitrary", "arbitrary")),
        name="compress_sample",
    )(jnp.pad(page_table, ((0, 1), (0, 0))).reshape(-1),
      *([cache_t] * (2 * PAGES_PER_STEP)), pe4, w1p, w2p)
    return (kc.reshape(n_seq, N_KV, 2 * nc_half, LANES), vc.reshape(n_seq, N_KV, 2 * nc_half, LANES))


def _row_slopes(shape):
    head = lax.broadcasted_iota(I32, shape, 0)
    out = jnp.zeros(shape, F32)
    for h in range(N_HEADS):
        out = jnp.where(head == h, 2.0 ** (-8.0 * (h + 1) / N_HEADS), out)
    return out


def _sample_select_kernel(q_ref, kc_ref, vc_ref, ocmp_ref, imp_ref, *, nc):
    q = q_ref[0]
    ns = nc // 2
    rowgroup = lax.broadcasted_iota(I32, (N_HEADS, 1), 0) // GROUP
    o = jnp.zeros((N_HEADS, LANES), F32)
    for g in range(N_KV):
        s = _dot_nt(q, kc_ref[0, g])
        m = jnp.max(s, axis=-1, keepdims=True)
        p = jnp.exp(s - m)
        p = p / jnp.maximum(jnp.sum(p, axis=-1, keepdims=True), 1e-30)
        og = jnp.dot(p.astype(BF16), vc_ref[0, g], preferred_element_type=F32)
        mine = rowgroup == g
        o = jnp.where(mine, og, o)
        ph = jnp.sum(jnp.where(mine, p, 0.0), axis=0, keepdims=True)
        imp_ref[0, pl.ds(g, 1), :] = ph[:, :ns] + ph[:, ns:]
    ocmp_ref[0] = o


def _sample_select(q_aug, kc, vc):
    n_seq, _, nc, _ = kc.shape
    ns = nc // 2
    return pl.pallas_call(
        functools.partial(_sample_select_kernel, nc=nc),
        grid=(n_seq,),
        in_specs=[pl.BlockSpec((1, N_HEADS, LANES), lambda b: (b, 0, 0)),
                  pl.BlockSpec((1, N_KV, nc, LANES), lambda b: (b, 0, 0, 0)),
                  pl.BlockSpec((1, N_KV, nc, LANES), lambda b: (b, 0, 0, 0))],
        out_specs=(pl.BlockSpec((1, N_HEADS, LANES), lambda b: (b, 0, 0)),
                   pl.BlockSpec((1, N_KV, ns), lambda b: (b, 0, 0))),
        out_shape=(jax.ShapeDtypeStruct((n_seq, N_HEADS, LANES), F32),
                   jax.ShapeDtypeStruct((n_seq, N_KV, ns), F32)),
        compiler_params=_cparams(("arbitrary",)),
        name="sample_select",
    )(q_aug, kc, vc)


def _sample_topk_kernel(imp_ref, ids_ref, *, ns, k):
    imp = imp_ref[...]
    lane = lax.broadcasted_iota(I32, imp.shape, 1)
    out_lane = lax.broadcasted_iota(I32, ids_ref.shape, 1)
    big = jnp.int32(2 ** 30)
    v = jnp.where((lane == 0) | (lane == ns - 1), FORCED, imp)
    ids = jnp.zeros(ids_ref.shape, I32)
    for t in range(k):
        m = jnp.max(v, axis=-1, keepdims=True)
        first = jnp.min(jnp.where(v == m, lane, big), axis=-1, keepdims=True)
        v = jnp.where(lane == first, REMOVED, v)
        ids = jnp.where(out_lane == t, first, ids)
    ids_ref[...] = ids


def _sample_topk(imp, k):
    rows, ns = imp.shape
    return pl.pallas_call(
        functools.partial(_sample_topk_kernel, ns=ns, k=k),
        out_shape=jax.ShapeDtypeStruct((rows, LANES), I32),
        name="sample_topk",
    )(imp)


SEL_PAST = SEL_TOPK - 1


def _sample_attend_kernel(pt_ref, ids_ref, *refs, past_len):
    del pt_ref
    nblk = N_KV * SEL_PAST
    blocks = refs[:nblk]
    win_ref, new_ref, q_ref, ocmp_ref, gate_ref, o_ref = refs[nblk:]
    b = pl.program_id(0)
    slope = _row_slopes((N_HEADS, 1))
    rowgroup = lax.broadcasted_iota(I32, (N_HEADS, 1), 0) // GROUP
    gates = gate_ref[0]
    w_buf = win_ref.shape[-1]
    q = q_ref[0]
    qf = q.astype(F32)
    lane = lax.broadcasted_iota(I32, (1, PAGE_SIZE), 1)
    o = jnp.zeros((N_HEADS, HEAD_DIM), F32)
    for g in range(N_KV):
        def new_row(j):
            return new_ref[0, j * N_KV + g:j * N_KV + g + 1, :].astype(BF16).astype(F32)

        s_new = jnp.sum(qf * new_row(2), axis=-1, keepdims=True)
        scores, oks = [], []
        for t in range(SEL_PAST):
            blk = ids_ref[(b * N_KV + g) * SEL_TOPK + t]
            kt = blocks[g * SEL_PAST + t][0].astype(BF16)
            in_blk = (lane // SEL_BLOCK) == (blk % 2)
            pos = (blk // 2) * PAGE_SIZE + lane
            s = jnp.dot(q, kt, preferred_element_type=F32) - slope * (past_len - pos).astype(F32)
            scores.append(jnp.where(in_blk, s, NEG))
            oks.append(in_blk)
        m = s_new
        for s in scores:
            m = jnp.maximum(m, jnp.max(s, axis=-1, keepdims=True))
        p_new = jnp.exp(s_new - m)
        den = p_new
        acc = p_new * new_row(3)
        for t in range(SEL_PAST):
            p = jnp.where(oks[t], jnp.exp(scores[t] - m), 0.0)
            den = den + jnp.sum(p, axis=-1, keepdims=True)
            acc = acc + _dot_nt(p, blocks[g * SEL_PAST + t][1])
        o_sel = acc / jnp.maximum(den, 1e-30)
        kwt = win_ref[0, g].astype(BF16)
        dist = w_buf - lax.broadcasted_iota(I32, (1, w_buf), 1)
        ok = dist < WINDOW
        s = jnp.where(ok, jnp.dot(q, kwt, preferred_element_type=F32) - slope * dist.astype(F32), NEG)
        s_new = jnp.sum(qf * new_row(4), axis=-1, keepdims=True)
        m = jnp.maximum(s_new, jnp.max(s, axis=-1, keepdims=True))
        p = jnp.where(ok, jnp.exp(s - m), 0.0)
        p_new = jnp.exp(s_new - m)
        den = p_new + jnp.sum(p, axis=-1, keepdims=True)
        o_win = (p_new * new_row(5) + _dot_nt(p, win_ref[1, g])) / jnp.maximum(den, 1e-30)
        og = gates[:, 0:1] * ocmp_ref[0][:, :HEAD_DIM] + gates[:, 1:2] * o_sel + gates[:, 2:3] * o_win
        o = jnp.where(rowgroup == g, og, o)
    o_ref[0] = o


def _sample_attend(cache_t, win_t, layer, page_table, ids, kv_new, q, ocmp, gates, past_len):
    n_seq, n_pages = page_table.shape
    w_buf = win_t.shape[-1]
    ids3 = ids.reshape(n_seq, N_KV, SEL_TOPK)
    phys = jnp.take_along_axis(page_table[:, None, :], ids3 // 2, axis=-1)
    phys = jnp.pad(phys, ((0, 1), (0, 0), (0, 0))).astype(I32)

    def blk_map(g, t):
        return lambda b, ph, idr: (layer, ph[(b * N_KV + g) * SEL_TOPK + t], 1, g, 0, 0)

    grid_spec = pltpu.PrefetchScalarGridSpec(
        num_scalar_prefetch=2,
        grid=(n_seq,),
        in_specs=[pl.BlockSpec((None, None, 2, None, HEAD_DIM, PAGE_SIZE), blk_map(g, t))
                  for g in range(N_KV) for t in range(SEL_PAST)] + [
            pl.BlockSpec((None, None, 2, N_KV, HEAD_DIM, w_buf), lambda b, pt, idr: (layer, b, 0, 0, 0, 0)),
            pl.BlockSpec((1, 6 * N_KV, HEAD_DIM), lambda b, pt, idr: (b, 0, 0)),
            pl.BlockSpec((1, N_HEADS, HEAD_DIM), lambda b, pt, idr: (b, 0, 0)),
            pl.BlockSpec((1, N_HEADS, LANES), lambda b, pt, idr: (b, 0, 0)),
            pl.BlockSpec((1, N_HEADS, LANES), lambda b, pt, idr: (b, 0, 0))],
        out_specs=pl.BlockSpec((1, N_HEADS, HEAD_DIM), lambda b, pt, idr: (b, 0, 0)),
    )
    return pl.pallas_call(
        functools.partial(_sample_attend_kernel, past_len=past_len),
        grid_spec=grid_spec,
        out_shape=jax.ShapeDtypeStruct((n_seq, N_HEADS, HEAD_DIM), F32),
        compiler_params=_cparams(("arbitrary",)),
        name="sample_attend",
    )(phys.reshape(-1), ids.reshape(-1), *([cache_t] * (N_KV * SEL_PAST)), win_t, kv_new, q, ocmp, gates)


def _nsa_sample(cache_t, win_t, layer, page_table, kv_new, qt, gt, pe, w1, w2):
    n_seq, n_pages = page_table.shape
    past_len = n_pages * PAGE_SIZE
    kc, vc = _compress_sample(cache_t, layer, page_table, pe, w1, w2)
    q = qt.T.reshape(n_seq, N_HEADS, HEAD_DIM)
    slopes = (2.0 ** (-8.0 * jnp.arange(1, N_HEADS + 1, dtype=F32) / N_HEADS)).reshape(1, N_HEADS, 1)
    aug = jnp.concatenate([slopes * 128.0, slopes, -(slopes * 128.0) * (past_len // 128),
                           jnp.zeros((1, N_HEADS, LANES - HEAD_DIM - 3), F32)], axis=-1)
    q_aug = jnp.concatenate([q, jnp.broadcast_to(aug, (n_seq, N_HEADS, LANES - HEAD_DIM)).astype(BF16)], axis=-1)
    ocmp, imp = _sample_select(q_aug, kc, vc)
    ids = _sample_topk(imp.reshape(n_seq * N_KV, -1), SEL_PAST)[:, :SEL_TOPK]
    gates = gt[:3 * N_HEADS].reshape(3, N_HEADS, n_seq).transpose(2, 1, 0)
    gates = jnp.pad(gates, ((0, 0), (0, 0), (0, LANES - 3)))
    o = _sample_attend(cache_t, win_t, layer, page_table, ids, kv_new.reshape(n_seq, 6 * N_KV, HEAD_DIM), q,
                       ocmp, gates, past_len)
    return o.reshape(n_seq, D_NSA).astype(BF16)


def _sg_chunk_params(sg_w, sg_b):
    w = jnp.tril(sg_w).astype(BF16)
    bias = jnp.repeat(sg_b.T, SG_DIM, axis=1)
    return w, bias


def _pad_rows(a, mult):
    pad = (-a.shape[0]) % mult
    return jnp.pad(a, ((0, pad),) + ((0, 0),) * (a.ndim - 1)) if pad else a


def _prep_weights(norm_mix_g, norm_ffn_g, norm_final_g, w_in, cmp_pe, cmp_w1, cmp_w2, sg_norm_g, sg_norm_b,
                  sg_w, sg_b, w_branch_nsa, w_branch_sg, w_out, ffn_w_gu, ffn_w_down, router_w, router_b,
                  moe_w_gu, moe_w_down):
    depth = w_in.shape[0]
    layers = []
    for i in range(depth):
        wn, wt = _pack_w_in(w_in[i])
        sgw_chunk, sgb_chunk = _sg_chunk_params(sg_w[i], sg_b[i])
        lw = {
            "norm_mix": norm_mix_g[i].reshape(1, D_MODEL), "norm_ffn": norm_ffn_g[i].reshape(1, D_MODEL),
            "norm_final": norm_final_g.reshape(1, D_MODEL), "wn": wn, "wt": wt,
            "sg_norm_g": sg_norm_g[i].reshape(1, D_SG), "sg_norm_b": sg_norm_b[i].reshape(1, D_SG),
            "cmp_pe": cmp_pe[i], "cmp_w1": cmp_w1[i], "cmp_w2": cmp_w2[i],
            "wa": w_branch_nsa[i].astype(BF16), "wb": w_branch_sg[i].astype(BF16), "wo": w_out[i].astype(BF16),
            "sgw_chunk": sgw_chunk, "sgb_chunk": sgb_chunk,
            "sgw_first": jnp.repeat(sg_w[i][:, 0, 0], SG_DIM).reshape(1, D_SG),
            "sgb_first": jnp.repeat(sg_b[i][:, 0], SG_DIM).reshape(1, D_SG),
            "routed": i % 2 == 1, "final": i == depth - 1,
        }
        if i % 2 == 0:
            lw["ffn_a"], lw["ffn_b"], lw["ffn_d"] = _pack_dense_ffn(ffn_w_gu[i // 2], ffn_w_down[i // 2])
            lw["router_w"] = jnp.zeros((D_MODEL, LANES), F32)
            lw["router_b"] = jnp.zeros((1, LANES), F32)
        else:
            lw["ffn_a"], lw["ffn_b"], lw["ffn_d"] = _pack_moe_ffn(moe_w_gu[i // 2], moe_w_down[i // 2])
            lw["router_w"] = jnp.pad(router_w[i // 2], ((0, 0), (0, LANES - N_EXPERTS)))
            lw["router_b"] = jnp.pad(router_b[i // 2], (0, LANES - N_EXPERTS)).reshape(1, LANES)
        layers.append(lw)
    return layers


def _prompt_layer(x, mod, lw, tm):
    sh1, sc1, g1, sh2, sc2, g2 = mod
    kvt, u, v, ga, gb, kaug, qt, vt, gt = _in_proj(x, lw["norm_mix"], sc1, sh1, lw["wn"], lw["wt"],
                                                    lw["sg_norm_g"], lw["sg_norm_b"], tm, False)
    onsa = _nsa_prompt(kvt, kaug, qt, vt, gt, lw["cmp_pe"], lw["cmp_w1"], lw["cmp_w2"])
    x = _mix_out(x, onsa, u, v, ga, gb, g1, lw["wa"], lw["wb"], lw["wo"], lw["sgw_chunk"], lw["sgb_chunk"],
                 tm, True)
    x = _ffn(x, lw["norm_ffn"], sc2, sh2, g2, lw["ffn_a"], lw["ffn_b"], lw["ffn_d"], lw["router_w"],
             lw["router_b"], lw["norm_final"], lw["ffn_tm"], lw["routed"], lw["final"])
    return x, kvt


def _sample_layer(x, mod, lw, cache_t, win_t, layer, page_table):
    sh1, sc1, g1, sh2, sc2, g2 = mod
    tm = x.shape[0]
    kvt, u, v, ga, gb, _, qt, _, gt, kv = _in_proj(x, lw["norm_mix"], sc1, sh1, lw["wn"], lw["wt"],
                                                   lw["sg_norm_g"], lw["sg_norm_b"], tm, True)
    onsa = _nsa_sample(cache_t, win_t, layer, page_table, kv, qt, gt, lw["cmp_pe"], lw["cmp_w1"], lw["cmp_w2"])
    x = _mix_out(x, onsa, u, v, ga, gb, g1, lw["wa"], lw["wb"], lw["wo"], lw["sgw_first"], lw["sgb_first"],
                 tm, False)
    x = _ffn(x, lw["norm_ffn"], sc2, sh2, g2, lw["ffn_a"], lw["ffn_b"], lw["ffn_d"], lw["router_w"],
             lw["router_b"], lw["norm_final"], tm, lw["routed"], lw["final"])
    return x, kv, kvt, v


PROMPT_ROW_TILE = 512
PROMPT_FFN_ROW_TILE = 512


def kernel(x_prompt, x_sample, cache_kv, state_win, page_table, c_prompt, c_sample, norm_mix_g, norm_ffn_g,
           norm_final_g, w_ada, b_ada, w_in, cmp_pe, cmp_w1, cmp_w2, sg_norm_g, sg_norm_b, sg_w, sg_b,
           w_branch_nsa, w_branch_sg, w_out, ffn_w_gu, ffn_w_down, router_w, router_b, moe_w_gu, moe_w_down):
    batch, seq, _ = x_prompt.shape
    n_seq, dec_seq, _ = x_sample.shape
    depth = w_in.shape[0]
    assert batch == 1 and dec_seq == 1
    assert seq % PROMPT_FFN_ROW_TILE == 0 and seq // Q_BLOCK <= 256
    past_len = page_table.shape[1] * PAGE_SIZE
    assert past_len % CHUNK == 0 and past_len // LANES <= 256
    assert state_win.shape[2] == WINDOW and page_table.shape[1] % PAGES_PER_STEP == 0

    layers = _prep_weights(norm_mix_g, norm_ffn_g, norm_final_g, w_in, cmp_pe, cmp_w1, cmp_w2, sg_norm_g,
                           sg_norm_b, sg_w, sg_b, w_branch_nsa, w_branch_sg, w_out, ffn_w_gu, ffn_w_down,
                           router_w, router_b, moe_w_gu, moe_w_down)
    c_all = _pad_rows(jnp.concatenate([c_prompt, c_sample], axis=0), SUBLANES)
    mods = _ada(c_all, w_ada, b_ada)

    cache_t = cache_kv.transpose(0, 1, 3, 4, 5, 2)
    win_t = state_win.transpose(0, 1, 3, 4, 5, 2)

    xp = x_prompt[0]
    xs = x_sample[:, 0]
    kv_p, kv_s, win_p, win_new, sgv_s = [], [], [], [], []
    w_keep = min(WINDOW, seq)
    kv_rows = 4 * N_KV * HEAD_DIM
    for i in range(depth):
        lw = dict(layers[i])
        lw["ffn_tm"] = PROMPT_FFN_ROW_TILE
        mod_p = tuple(mods[i, 0:1, j * D_MODEL:(j + 1) * D_MODEL] for j in range(6))
        mod_s = tuple(mods[i, 1:1 + n_seq, j * D_MODEL:(j + 1) * D_MODEL] for j in range(6))
        xp, kvt_p = _prompt_layer(xp, mod_p, lw, PROMPT_ROW_TILE)
        xs, kvs, kvt_s, v_s = _sample_layer(xs, mod_s, lw, cache_t, win_t, i, page_table)
        kv_p.append(kvt_p[:kv_rows].reshape(4, N_KV, HEAD_DIM, seq))
        win_p.append(kvt_p[kv_rows:, seq - w_keep:].reshape(2, N_KV, HEAD_DIM, w_keep))
        kv_s.append(kvs[:, :kv_rows].reshape(n_seq, 1, 4, N_KV, HEAD_DIM))
        win_new.append(kvt_s[kv_rows:].reshape(2, N_KV, HEAD_DIM, n_seq).transpose(3, 0, 1, 2)[..., None])
        sgv_s.append(v_s.reshape(n_seq, 1, D_SG))
    kv_prompt = jnp.stack(kv_p).transpose(0, 4, 1, 2, 3)[:, None]
    win_prompt = jnp.stack(win_p).transpose(0, 4, 1, 2, 3)[:, None]
    win_sample = jnp.concatenate([win_t[..., 1:], jnp.stack(win_new)], axis=-1).transpose(0, 1, 5, 2, 3, 4)
    return (xp[None], xs[:, None], kv_prompt, jnp.stack(kv_s), win_prompt, win_sample, jnp.stack(sgv_s))
```

```python
import functools

import jax
import jax.numpy as jnp
from jax import lax
from jax.experimental import pallas as pl
from jax.experimental.pallas import tpu as pltpu

F32 = jnp.float32
BF16 = jnp.bfloat16
I32 = jnp.int32
HIGHEST = lax.Precision.HIGHEST

LANES = 128
SUBLANES = 8
VMEM_LIMIT_BYTES = 56 * 1024 * 1024

D_MODEL = 1024
N_HEADS = 8
HEAD_DIM = 64
N_KV = 2
GROUP = N_HEADS // N_KV
D_NSA = N_HEADS * HEAD_DIM
CMP_BLOCK = 32
CMP_HIDDEN = 256
SEL_BLOCK = 64
SEL_TOPK = 16
WINDOW = 512
Q_BLOCK = 128
N_SG = 8
SG_DIM = 64
D_SG = N_SG * SG_DIM
CHUNK = 128
N_EXPERTS = 8
D_FF_CHUNK = 1408
PAGE_SIZE = 128
EPS = 1e-6
NEG = -1e30
FORCED = 1e9
REMOVED = -3e38

KV_COLS = 6 * N_KV * HEAD_DIM
OFF_KV = D_NSA
OFF_G = OFF_KV + KV_COLS
OFF_U = OFF_G + 3 * N_HEADS
OFF_V = OFF_U + D_SG
OFF_GA = OFF_V + D_SG
OFF_GB = OFF_GA + D_MODEL
IN_COLS = OFF_GB + D_MODEL

WN_U = 0
WN_V = WN_U + D_SG
WN_GA = WN_V + D_SG
WN_GB = WN_GA + D_MODEL
WN_KAUG = WN_GB + D_MODEL
WN_KV = WN_KAUG + 4 * LANES
WN_COLS = WN_KV + KV_COLS
WT_Q = 0
WT_KV = D_NSA
WT_G = WT_KV + KV_COLS
WT_ROWS = WT_G + 32

POS_HI_LANE = HEAD_DIM
POS_LO_LANE = HEAD_DIM + 1
ONE_LANE = HEAD_DIM + 2


def _cparams(sem):
    return pltpu.CompilerParams(dimension_semantics=sem, vmem_limit_bytes=VMEM_LIMIT_BYTES)


def _bdot(a, b):
    return jnp.dot(a.astype(BF16), b.astype(BF16), preferred_element_type=F32)


def _dot_nt(a, b):
    return lax.dot_general(a.astype(BF16), b.astype(BF16), (((1,), (1,)), ((), ())),
                           preferred_element_type=F32)


def _silu(x):
    return x * jax.nn.sigmoid(x)


def _pos_aug(pos, lane):
    hi = (pos >> 7).astype(F32)
    lo = (pos & 127).astype(F32)
    return jnp.where(lane == POS_HI_LANE, hi,
                     jnp.where(lane == POS_LO_LANE, lo,
                               jnp.where(lane == ONE_LANE, 1.0, 0.0)))


def _group_slopes(g, lane_head):
    out = jnp.zeros(lane_head.shape, F32)
    for gg in range(N_KV):
        for hh in range(GROUP):
            s = 2.0 ** (-8.0 * (gg * GROUP + hh + 1) / N_HEADS)
            out = jnp.where((lane_head == hh) & (g == gg), s, out)
    return out


def _qt_aug(qt_ref, g, q0_blocks):
    heads = [qt_ref[h * HEAD_DIM:(h + 1) * HEAD_DIM, :] for h in range(GROUP)]
    q = jnp.concatenate(heads, axis=1)
    n = GROUP * Q_BLOCK
    row = lax.broadcasted_iota(I32, (LANES - HEAD_DIM, n), 0)
    lane_head = lax.broadcasted_iota(I32, (LANES - HEAD_DIM, n), 1) // Q_BLOCK
    slope = _group_slopes(g, lane_head)
    q0f = (q0_blocks).astype(F32)
    aug = jnp.where(row == 0, slope * 128.0,
                    jnp.where(row == 1, slope,
                              jnp.where(row == 2, -(slope * 128.0) * q0f, 0.0)))
    return jnp.concatenate([q.astype(BF16), aug.astype(BF16)], axis=0)


def _ada_kernel(c_ref, w_ref, b_ref, o_ref):
    c = c_ref[...]
    o_ref[0] = jnp.dot(_silu(c), w_ref[0], precision=HIGHEST, preferred_element_type=F32) + b_ref[0]


def _ada(c_all, w_ada, b_ada):
    depth = w_ada.shape[0]
    rows = c_all.shape[0]
    tn = 1024
    n = w_ada.shape[2]
    return pl.pallas_call(
        _ada_kernel,
        grid=(depth, n // tn),
        in_specs=[pl.BlockSpec((rows, D_MODEL), lambda l, j: (0, 0)),
                  pl.BlockSpec((1, D_MODEL, tn), lambda l, j: (l, 0, j)),
                  pl.BlockSpec((1, 1, tn), lambda l, j: (l, 0, j))],
        out_specs=pl.BlockSpec((1, rows, tn), lambda l, j: (l, 0, j)),
        out_shape=jax.ShapeDtypeStruct((depth, rows, n), F32),
        compiler_params=_cparams(("arbitrary", "arbitrary")),
        name="ada",
    )(c_all, w_ada, b_ada.reshape(depth, 1, n))


def _in_proj_kernel(x_ref, g_ref, sc_ref, sh_ref, wn_ref, wt_ref, lng_ref, lnb_ref, *out_refs, tm, natural_kv):
    if natural_kv:
        kvt_ref, u_ref, v_ref, ga_ref, gb_ref, kaug_ref, qt_ref, vt_ref, gt_ref, kv_ref = out_refs
    else:
        kvt_ref, u_ref, v_ref, ga_ref, gb_ref, kaug_ref, qt_ref, vt_ref, gt_ref = out_refs
    i = pl.program_id(0)
    x = x_ref[...]
    ms = jnp.mean(x * x, axis=-1, keepdims=True)
    h = x * lax.rsqrt(ms + EPS) * g_ref[...]
    h = h * (1.0 + sc_ref[...]) + sh_ref[...]
    hb = h.astype(BF16)

    def seg(a, b):
        return jnp.dot(hb, wn_ref[:, a:b], preferred_element_type=F32)

    u_ref[...] = jax.nn.gelu(seg(WN_U, WN_V)).astype(BF16)
    v = jax.nn.gelu(seg(WN_V, WN_GA))
    mu = jnp.mean(v, axis=-1, keepdims=True)
    var = jnp.mean(jnp.square(v - mu), axis=-1, keepdims=True)
    v_ref[...] = (v - mu) * lax.rsqrt(var + EPS) * lng_ref[...] + lnb_ref[...]
    ga_ref[...] = jax.nn.sigmoid(seg(WN_GA, WN_GB)).astype(BF16)
    gb_ref[...] = jax.nn.sigmoid(seg(WN_GB, WN_KAUG)).astype(BF16)

    pos = lax.broadcasted_iota(I32, (tm, LANES), 0) + i * tm
    lane = lax.broadcasted_iota(I32, (tm, LANES), 1)
    aug = _pos_aug(pos, lane)
    for j in range(4):
        k = seg(WN_KAUG + j * LANES, WN_KAUG + (j + 1) * LANES)
        kaug_ref[j] = (k + aug).astype(BF16)
    if natural_kv:
        kv_ref[...] = seg(WN_KV, WN_COLS)

    zt = _dot_nt(wt_ref[...], hb)
    qt_ref[...] = (zt[WT_Q:WT_KV] * (HEAD_DIM ** -0.5)).astype(BF16)
    kvt = zt[WT_KV:WT_G]
    kvt_ref[...] = kvt
    two = N_KV * HEAD_DIM
    vt_ref[0:two, :] = kvt[3 * two:4 * two].astype(BF16)
    vt_ref[two:2 * two, :] = kvt[5 * two:6 * two].astype(BF16)
    gt_ref[...] = jax.nn.sigmoid(zt[WT_G:WT_ROWS])


def _in_proj(x, g, sc, sh, wn, wt, lng, lnb, tm, natural_kv):
    m = x.shape[0]
    mod_rows = sc.shape[0]
    mod_block = (1, D_MODEL) if mod_rows == 1 else (tm, D_MODEL)
    mod_map = (lambda i: (0, 0)) if mod_rows == 1 else (lambda i: (i, 0))
    row = lambda i: (i, 0)
    col = lambda i: (0, i)
    const = lambda i: (0, 0)
    out_shape = [
        jax.ShapeDtypeStruct((KV_COLS, m), F32),
        jax.ShapeDtypeStruct((m, D_SG), BF16),
        jax.ShapeDtypeStruct((m, D_SG), F32),
        jax.ShapeDtypeStruct((m, D_MODEL), BF16),
        jax.ShapeDtypeStruct((m, D_MODEL), BF16),
        jax.ShapeDtypeStruct((4, m, LANES), BF16),
        jax.ShapeDtypeStruct((D_NSA, m), BF16),
        jax.ShapeDtypeStruct((4 * HEAD_DIM, m), BF16),
        jax.ShapeDtypeStruct((32, m), F32),
    ]
    out_specs = [
        pl.BlockSpec((KV_COLS, tm), col),
        pl.BlockSpec((tm, D_SG), row),
        pl.BlockSpec((tm, D_SG), row),
        pl.BlockSpec((tm, D_MODEL), row),
        pl.BlockSpec((tm, D_MODEL), row),
        pl.BlockSpec((4, tm, LANES), lambda i: (0, i, 0)),
        pl.BlockSpec((D_NSA, tm), col),
        pl.BlockSpec((4 * HEAD_DIM, tm), col),
        pl.BlockSpec((32, tm), col),
    ]
    if natural_kv:
        out_shape.append(jax.ShapeDtypeStruct((m, KV_COLS), F32))
        out_specs.append(pl.BlockSpec((tm, KV_COLS), row))
    return pl.pallas_call(
        functools.partial(_in_proj_kernel, tm=tm, natural_kv=natural_kv),
        grid=(m // tm,),
        in_specs=[pl.BlockSpec((tm, D_MODEL), row),
                  pl.BlockSpec((1, D_MODEL), const),
                  pl.BlockSpec(mod_block, mod_map),
                  pl.BlockSpec(mod_block, mod_map),
                  pl.BlockSpec((D_MODEL, WN_COLS), const),
                  pl.BlockSpec((WT_ROWS, D_MODEL), const),
                  pl.BlockSpec((1, D_SG), const),
                  pl.BlockSpec((1, D_SG), const)],
        out_specs=tuple(out_specs),
        out_shape=tuple(out_shape),
        compiler_params=_cparams(("arbitrary",)),
        name="in_proj",
    )(x, g, sc, sh, wn, wt, lng, lnb)


def _pack_w_in(w):
    kv = w[:, OFF_KV:OFF_G]

    def kvcol(j, g):
        return kv[:, (j * N_KV + g) * HEAD_DIM:(j * N_KV + g + 1) * HEAD_DIM]

    zpad = jnp.zeros((D_MODEL, LANES - HEAD_DIM), w.dtype)
    kaug = [jnp.concatenate([kvcol(j, g), zpad], axis=1) for j in (2, 4) for g in range(N_KV)]
    wn = jnp.concatenate([w[:, OFF_U:OFF_V], w[:, OFF_V:OFF_GA], w[:, OFF_GA:OFF_GB],
                          w[:, OFF_GB:IN_COLS]] + kaug + [kv], axis=1)
    gpad = jnp.zeros((D_MODEL, 32 - 3 * N_HEADS), w.dtype)
    wt = jnp.concatenate([w[:, :OFF_KV], kv, w[:, OFF_G:OFF_U], gpad], axis=1).T
    return wn.astype(BF16), wt.astype(BF16)


def _cmp_block_end(nc):
    r = lax.broadcasted_iota(I32, (nc, LANES), 0)
    half = nc // 2
    c = 2 * (r % half) + r // half
    return c * CMP_BLOCK + (CMP_BLOCK - 1)


def _compress_kernel(xk_ref, xv_ref, pe_ref, w1_ref, w2k_ref, w2vt_ref, kc_ref, vct_ref, *, nc):
    xk = xk_ref[0, 0] + pe_ref[0]
    hk = _silu(_bdot(xk, w1_ref[0]))
    kc = _bdot(hk, w2k_ref[...])
    lane = lax.broadcasted_iota(I32, (nc, LANES), 1)
    kc_ref[0] = (kc + _pos_aug(_cmp_block_end(nc), lane)).astype(BF16)
    xv = xv_ref[0, 0] + pe_ref[1]
    hv = _silu(_bdot(xv, w1_ref[1]))
    vct_ref[0] = _dot_nt(w2vt_ref[...], hv).astype(BF16)


def _compress_prompt(kvt, pe, w1, w2):
    m = kvt.shape[1]
    nc = m // CMP_BLOCK
    half = nc // 2
    kvc = kvt[:2 * N_KV * HEAD_DIM].reshape(2, N_KV, HEAD_DIM, half, 2, CMP_BLOCK)
    x = kvc.transpose(0, 1, 4, 3, 5, 2).reshape(2, N_KV, nc, CMP_BLOCK * HEAD_DIM)
    pe_flat = pe.reshape(2, 1, CMP_BLOCK * HEAD_DIM)
    w2k = jnp.concatenate([w2[0], jnp.zeros((CMP_HIDDEN, LANES - HEAD_DIM), w2.dtype)], axis=1).astype(BF16)
    w2vt = w2[1].T.astype(BF16)
    kd = CMP_BLOCK * HEAD_DIM
    return pl.pallas_call(
        functools.partial(_compress_kernel, nc=nc),
        grid=(N_KV,),
        in_specs=[pl.BlockSpec((1, 1, nc, kd), lambda g: (0, g, 0, 0)),
                  pl.BlockSpec((1, 1, nc, kd), lambda g: (1, g, 0, 0)),
                  pl.BlockSpec((2, 1, kd), lambda g: (0, 0, 0)),
                  pl.BlockSpec((2, kd, CMP_HIDDEN), lambda g: (0, 0, 0)),
                  pl.BlockSpec((CMP_HIDDEN, LANES), lambda g: (0, 0)),
                  pl.BlockSpec((HEAD_DIM, CMP_HIDDEN), lambda g: (0, 0))],
        out_specs=(pl.BlockSpec((1, nc, LANES), lambda g: (g, 0, 0)),
                   pl.BlockSpec((1, HEAD_DIM, nc), lambda g: (g, 0, 0))),
        out_shape=(jax.ShapeDtypeStruct((N_KV, nc, LANES), BF16),
                   jax.ShapeDtypeStruct((N_KV, HEAD_DIM, nc), BF16)),
        compiler_params=_cparams(("arbitrary",)),
        name="compress_prompt",
    )(x, x, pe_flat, w1.astype(BF16), w2k, w2vt)


def _topk_mask_rows(imp, k):
    rows = lax.broadcasted_iota(I32, imp.shape, 0)
    big = jnp.int32(2 ** 30)

    def body(_, v):
        m = jnp.max(v, axis=0, keepdims=True)
        first = jnp.min(jnp.where(v == m, rows, big), axis=0, keepdims=True)
        return jnp.where(rows == first, REMOVED, v)

    return jnp.where(lax.fori_loop(0, k, body, imp) == REMOVED, 1.0, 0.0)


def _visited_blocks(sel, limit):
    ns = sel.shape[0]
    ones_q = jnp.ones((SUBLANES, sel.shape[1]), BF16)
    blk = lax.broadcasted_iota(I32, (SUBLANES, ns), 1)
    flags = ((_dot_nt(ones_q, sel) > 0.5) & (blk < limit)).astype(F32)
    r = lax.broadcasted_iota(I32, (ns, ns), 0)
    c = lax.broadcasted_iota(I32, (ns, ns), 1)
    upper = jnp.where(r <= c, 1.0, 0.0)
    prefix = _bdot(flags, upper)
    before = jnp.where(prefix[0:1, :] <= r.astype(F32), 1.0, 0.0)
    ids = _dot_nt(jnp.ones((SUBLANES, ns), BF16), before)
    ids = jnp.minimum(ids, ns - 1.0).astype(I32)
    total = jnp.broadcast_to(prefix[:, ns - 1:ns], (SUBLANES, ns))
    return ids, total


def _select_kernel(qt_ref, kc_ref, vct_ref, mask_ref, ids_ref, cnt_ref, ocmp_ref, *, nc):
    i = pl.program_id(0)
    ns = nc // 2
    n = GROUP * Q_BLOCK
    q0 = i * Q_BLOCK
    cend = _cmp_block_end(nc)[:, :1]
    qpos = q0 + lax.broadcasted_iota(I32, (1, n), 1) % Q_BLOCK
    valid = cend <= qpos
    blk = lax.broadcasted_iota(I32, (ns, Q_BLOCK), 0)
    cur = (q0 + lax.broadcasted_iota(I32, (ns, Q_BLOCK), 1)) // SEL_BLOCK
    forced = (blk == 0) | (blk == cur) | (blk == cur - 1)
    imps = []
    for g in range(N_KV):
        qa = _qt_aug(qt_ref.at[g * GROUP * HEAD_DIM:(g + 1) * GROUP * HEAD_DIM], g, i)
        s = jnp.dot(kc_ref[g], qa, preferred_element_type=F32)
        s = jnp.where(valid, s, NEG)
        m = jnp.max(s, axis=0, keepdims=True)
        p = jnp.where(valid, jnp.exp(s - m), 0.0)
        p = p / jnp.maximum(jnp.sum(p, axis=0, keepdims=True), 1e-30)
        ocmp_ref[0, g] = jnp.dot(vct_ref[g], p.astype(BF16), preferred_element_type=F32)
        ph = p[:, 0:Q_BLOCK]
        for h in range(1, GROUP):
            ph = ph + p[:, h * Q_BLOCK:(h + 1) * Q_BLOCK]
        imp = ph[:ns] + ph[ns:]
        imp = jnp.where(forced, FORCED, imp)
        imps.append(jnp.where(blk <= cur, imp, NEG))
    for g in range(N_KV):
        sel = jnp.where(blk <= cur, _topk_mask_rows(imps[g], min(SEL_TOPK, ns)), 0.0)
        mask_ref[0, g] = sel
        ids_ref[0, g], cnt_ref[0, g] = _visited_blocks(sel, 2 * i)


def _select(qt, kc, vct):
    m = qt.shape[1]
    nqb = m // Q_BLOCK
    nc = m // CMP_BLOCK
    ns = nc // 2
    n = GROUP * Q_BLOCK
    return pl.pallas_call(
        functools.partial(_select_kernel, nc=nc),
        grid=(nqb,),
        in_specs=[pl.BlockSpec((D_NSA, Q_BLOCK), lambda i: (0, i)),
                  pl.BlockSpec((N_KV, nc, LANES), lambda i: (0, 0, 0)),
                  pl.BlockSpec((N_KV, HEAD_DIM, nc), lambda i: (0, 0, 0))],
        out_specs=(pl.BlockSpec((1, N_KV, ns, Q_BLOCK), lambda i: (i, 0, 0, 0)),
                   pl.BlockSpec((1, N_KV, SUBLANES, ns), lambda i: (i, 0, 0, 0)),
                   pl.BlockSpec((1, N_KV, SUBLANES, ns), lambda i: (i, 0, 0, 0)),
                   pl.BlockSpec((1, N_KV, HEAD_DIM, n), lambda i: (i, 0, 0, 0))),
        out_shape=(jax.ShapeDtypeStruct((nqb, N_KV, ns, Q_BLOCK), F32),
                   jax.ShapeDtypeStruct((nqb, N_KV, SUBLANES, ns), I32),
                   jax.ShapeDtypeStruct((nqb, N_KV, SUBLANES, ns), F32),
                   jax.ShapeDtypeStruct((nqb, N_KV, HEAD_DIM, n), F32)),
        compiler_params=_cparams(("arbitrary",)),
        name="nsa_select",
    )(qt, kc, vct)


SEL_PER_STEP = 8
WIN_TILES_PER_STEP = 4


def _softmax_step(state, s, valid=None):
    m, l, _ = state
    if valid is not None:
        s = jnp.where(valid, s, NEG)
    m_new = jnp.maximum(m, jnp.max(s, axis=0, keepdims=True))
    alpha = jnp.exp(m - m_new)
    p = jnp.exp(s - m_new)
    if valid is not None:
        p = jnp.where(valid, p, 0.0)
    return m_new, alpha * l + jnp.sum(p, axis=0, keepdims=True), alpha, p.astype(BF16)


def _pv_lane_tiles(vt_tiles):
    vt = jnp.concatenate(vt_tiles, axis=1)
    return lambda pb: jnp.dot(vt, pb, preferred_element_type=F32)


def _pv_row_tiles(v_tiles):
    v = jnp.concatenate(v_tiles, axis=0)
    return lambda pb: lax.dot_general(v, pb, (((0,), (0,)), ((), ())), preferred_element_type=F32)


def _attend_kernel(cnt_ref, ids_ref, qt_ref, ksel_ref, vsel_ref, kwin_ref, vwin_ref, mask_ref,
                   ocmp_ref, gt_ref, o_ref, *, nqb, ns):
    g = pl.program_id(0)
    i = pl.program_id(1)
    q0 = i * Q_BLOCK
    qa = _qt_aug(qt_ref, g, i)
    qa_heads = [qa[:, h * Q_BLOCK:(h + 1) * Q_BLOCK] for h in range(GROUP)]
    empty = (jnp.full((1, Q_BLOCK), NEG, F32), jnp.zeros((1, Q_BLOCK), F32), jnp.zeros((HEAD_DIM, Q_BLOCK), F32))
    init = (empty,) * GROUP

    def heads_update(states, keys, pv_fn, bias=None, valid=None):
        scores = [jnp.dot(keys, qa_heads[h], preferred_element_type=F32) for h in range(GROUP)]
        if bias is not None:
            scores = [s + bias for s in scores]
        parts = [_softmax_step(states[h], scores[h], valid) for h in range(GROUP)]
        pvs = [pv_fn(p) for _, _, _, p in parts]
        return tuple((m, l, alpha * states[h][2] + pvs[h]) for h, (m, l, alpha, _) in enumerate(parts))

    slot = g * nqb + i
    count = cnt_ref[slot]
    key_off = lax.broadcasted_iota(I32, (Q_BLOCK, 1), 0)
    query_off = lax.broadcasted_iota(I32, (1, Q_BLOCK), 1)
    causal = key_off <= query_off

    own = mask_ref[0, 0, pl.ds(pl.multiple_of(2 * i, 2), 2), :]
    chosen = jnp.concatenate([jnp.broadcast_to(own[0:1] > 0.5, (SEL_BLOCK, Q_BLOCK)),
                              jnp.broadcast_to(own[1:2] > 0.5, (SEL_BLOCK, Q_BLOCK))], axis=0)
    k_own = ksel_ref[0, pl.ds(pl.multiple_of(q0, Q_BLOCK), Q_BLOCK), :]
    first = heads_update(init, k_own, _pv_row_tiles([vsel_ref[0, 2 * i], vsel_ref[0, 2 * i + 1]]),
                         valid=chosen & causal)

    def sel_body(t, states):
        ks, vts, biases = [], [], []
        for u in range(SEL_PER_STEP):
            e = t * SEL_PER_STEP + u
            b = ids_ref[slot * ns + e]
            ks.append(ksel_ref[0, pl.ds(pl.multiple_of(b * SEL_BLOCK, SEL_BLOCK), SEL_BLOCK), :])
            row = mask_ref[0, 0, pl.ds(b, 1), :]
            live = (row > 0.5) & (e < count)
            biases.append(jnp.broadcast_to(jnp.where(live, 0.0, NEG), (SEL_BLOCK, Q_BLOCK)))
            vts.append(vsel_ref[0, b])
        return heads_update(states, jnp.concatenate(ks, axis=0), _pv_row_tiles(vts),
                            bias=jnp.concatenate(biases, axis=0))

    sel_states = lax.fori_loop(0, (count + (SEL_PER_STEP - 1)) // SEL_PER_STEP, sel_body, first)

    k_own = kwin_ref[0, pl.ds(pl.multiple_of(q0, Q_BLOCK), Q_BLOCK), :]
    win_states = heads_update(init, k_own, _pv_lane_tiles([vwin_ref[0, i]]), valid=causal)
    n_old = WINDOW // Q_BLOCK
    oldest_in_window = key_off > query_off
    for j0 in range(0, n_old, WIN_TILES_PER_STEP):
        ks, vts, biases = [], [], []
        for j in range(j0, min(j0 + WIN_TILES_PER_STEP, n_old)):
            tile = i - n_old + j
            held = jnp.maximum(tile, 0)
            ks.append(kwin_ref[0, pl.ds(pl.multiple_of(held * Q_BLOCK, Q_BLOCK), Q_BLOCK), :])
            inside = tile >= 0
            live = (oldest_in_window & inside) if j == 0 else inside
            biases.append(jnp.broadcast_to(jnp.where(live, 0.0, NEG), (Q_BLOCK, Q_BLOCK)))
            vts.append(vwin_ref[0, held])
        win_states = heads_update(win_states, jnp.concatenate(ks, axis=0), _pv_lane_tiles(vts),
                                  bias=jnp.concatenate(biases, axis=0))

    def head_out(h):
        def gate(branch):
            return gt_ref[pl.ds(branch * N_HEADS + g * GROUP + h, 1), :]
        _, l_s, acc_s = sel_states[h]
        _, l_w, acc_w = win_states[h]
        return (ocmp_ref[0, 0, :, h * Q_BLOCK:(h + 1) * Q_BLOCK] * gate(0)
                + acc_s / jnp.maximum(l_s, 1e-30) * gate(1) + acc_w / jnp.maximum(l_w, 1e-30) * gate(2))

    for hp in range(GROUP // 2):
        pair = jnp.concatenate([head_out(2 * hp), head_out(2 * hp + 1)], axis=0)
        o_ref[:, hp * LANES:(hp + 1) * LANES] = pair.T.astype(o_ref.dtype)


def _attend(cnt, ids, qt, kaug, vt_blocks_sel, vt_blocks_win, mask, ocmp, gt):
    m = qt.shape[1]
    nqb = m // Q_BLOCK
    ns = m // SEL_BLOCK
    n = GROUP * Q_BLOCK
    gh = GROUP * HEAD_DIM
    grid_spec = pltpu.PrefetchScalarGridSpec(
        num_scalar_prefetch=2,
        grid=(N_KV, nqb),
        in_specs=[pl.BlockSpec((gh, Q_BLOCK), lambda g, i, c, d: (g, i)),
                  pl.BlockSpec((1, m, LANES), lambda g, i, c, d: (g, 0, 0)),
                  pl.BlockSpec((1, ns, SEL_BLOCK, HEAD_DIM), lambda g, i, c, d: (g, 0, 0, 0)),
                  pl.BlockSpec((1, m, LANES), lambda g, i, c, d: (N_KV + g, 0, 0)),
                  pl.BlockSpec((1, nqb, HEAD_DIM, Q_BLOCK), lambda g, i, c, d: (g, 0, 0, 0)),
                  pl.BlockSpec((1, 1, ns, Q_BLOCK), lambda g, i, c, d: (i, g, 0, 0)),
                  pl.BlockSpec((1, 1, HEAD_DIM, n), lambda g, i, c, d: (i, g, 0, 0)),
                  pl.BlockSpec((32, Q_BLOCK), lambda g, i, c, d: (0, i))],
        out_specs=pl.BlockSpec((Q_BLOCK, gh), lambda g, i, c, d: (i, g)),
    )
    return pl.pallas_call(
        functools.partial(_attend_kernel, nqb=nqb, ns=ns),
        grid_spec=grid_spec,
        out_shape=jax.ShapeDtypeStruct((m, D_NSA), BF16),
        compiler_params=_cparams(("arbitrary", "arbitrary")),
        name="nsa_attend",
    )(cnt, ids, qt, kaug, vt_blocks_sel, kaug, vt_blocks_win, mask, ocmp, gt)


def _nsa_prompt(kvt, kaug, qt, vt, gt, pe, w1, w2):
    m = kvt.shape[1]
    nqb = m // Q_BLOCK
    ns = m // SEL_BLOCK
    kc, vct = _compress_prompt(kvt, pe, w1, w2)
    mask, ids8, cnt8, ocmp = _select(qt, kc, vct)
    ids = ids8[:, :, 0, :].transpose(1, 0, 2).reshape(-1)
    cnt = cnt8[:, :, 0, 0].T.astype(I32).reshape(-1)
    vt4 = vt.reshape(4, HEAD_DIM, m)
    vsel = vt4[:N_KV].reshape(N_KV, HEAD_DIM, ns, SEL_BLOCK).transpose(0, 2, 3, 1)
    vwin = vt4[N_KV:].reshape(N_KV, HEAD_DIM, nqb, Q_BLOCK).transpose(0, 2, 1, 3)
    return _attend(cnt, ids, qt, kaug, vsel, vwin, mask, ocmp, gt)


def _mix_kernel(x_ref, onsa_ref, u_ref, v_ref, ga_ref, gb_ref, g1_ref, wa_ref, wb_ref, wo_ref,
                sgw_ref, sgb_ref, o_ref, osg_ref, *, tm, chunked):
    if chunked:
        lane = lax.broadcasted_iota(I32, (CHUNK, LANES), 1)
        for c in range(tm // CHUNK):
            rows = slice(c * CHUNK, (c + 1) * CHUNK)
            for pr in range(N_SG // 2):
                cols = slice(pr * LANES, (pr + 1) * LANES)
                vp = v_ref[rows, cols].astype(BF16)
                a = jnp.dot(sgw_ref[2 * pr], vp, preferred_element_type=F32)
                b = jnp.dot(sgw_ref[2 * pr + 1], vp, preferred_element_type=F32)
                mix = jnp.where(lane < SG_DIM, a, b) + sgb_ref[:, cols]
                osg_ref[rows, cols] = (u_ref[rows, cols].astype(F32) * mix).astype(BF16)
    else:
        mix = v_ref[...] * sgw_ref[...] + sgb_ref[...]
        osg_ref[...] = (u_ref[...].astype(F32) * mix).astype(BF16)
    a = jnp.dot(onsa_ref[...], wa_ref[...], preferred_element_type=F32)
    b = jnp.dot(osg_ref[...], wb_ref[...], preferred_element_type=F32)
    merged = ga_ref[...].astype(F32) * a + gb_ref[...].astype(F32) * b
    y = jnp.dot(merged.astype(BF16), wo_ref[...], preferred_element_type=F32)
    o_ref[...] = x_ref[...] + g1_ref[...] * y


def _mix_out(x, onsa, u, v, ga, gb, g1, wa, wb, wo, sgw, sgb, tm, chunked):
    m = x.shape[0]
    mod_rows = g1.shape[0]
    mod_block = (1, D_MODEL) if mod_rows == 1 else (tm, D_MODEL)
    mod_map = (lambda i: (0, 0)) if mod_rows == 1 else (lambda i: (i, 0))
    row = lambda i: (i, 0)
    const2 = lambda i: (0, 0)
    if chunked:
        sg_specs = [pl.BlockSpec((N_SG, CHUNK, CHUNK), lambda i: (0, 0, 0)),
                    pl.BlockSpec((CHUNK, D_SG), const2)]
    else:
        sg_specs = [pl.BlockSpec((1, D_SG), const2), pl.BlockSpec((1, D_SG), const2)]
    return pl.pallas_call(
        functools.partial(_mix_kernel, tm=tm, chunked=chunked),
        grid=(m // tm,),
        in_specs=[pl.BlockSpec((tm, D_MODEL), row),
                  pl.BlockSpec((tm, D_NSA), row),
                  pl.BlockSpec((tm, D_SG), row),
                  pl.BlockSpec((tm, D_SG), row),
                  pl.BlockSpec((tm, D_MODEL), row),
                  pl.BlockSpec((tm, D_MODEL), row),
                  pl.BlockSpec(mod_block, mod_map),
                  pl.BlockSpec((D_NSA, D_MODEL), const2),
                  pl.BlockSpec((D_SG, D_MODEL), const2),
                  pl.BlockSpec((D_MODEL, D_MODEL), const2)] + sg_specs,
        out_specs=pl.BlockSpec((tm, D_MODEL), row),
        out_shape=jax.ShapeDtypeStruct((m, D_MODEL), F32),
        scratch_shapes=[pltpu.VMEM((tm, D_SG), BF16)],
        compiler_params=_cparams(("arbitrary",)),
        name="mix_out",
    )(x, onsa, u, v, ga, gb, g1, wa, wb, wo, sgw, sgb)


def _top2_combine(logits):
    lane = lax.broadcasted_iota(I32, logits.shape, 1)
    big = jnp.int32(2 ** 30)
    z = jnp.where(lane < N_EXPERTS, logits, -jnp.inf)
    t1 = jnp.max(z, axis=-1, keepdims=True)
    i1 = jnp.min(jnp.where(z == t1, lane, big), axis=-1, keepdims=True)
    z2 = jnp.where(lane == i1, -jnp.inf, z)
    t2 = jnp.max(z2, axis=-1, keepdims=True)
    i2 = jnp.min(jnp.where(z2 == t2, lane, big), axis=-1, keepdims=True)
    e = jnp.exp(t2 - t1)
    den = 1.0 + e
    return jnp.where(lane == i1, 1.0 / den, 0.0) + jnp.where(lane == i2, e / den, 0.0)


def _ffn_kernel(x_ref, g_ref, sc_ref, sh_ref, g2_ref, wa_ref, wb_ref, wd_ref, rw_ref, rb_ref, gf_ref,
                o_ref, h_ref, acc_ref, comb_ref, *, routed, final_norm, n_chunks):
    e = pl.program_id(1)

    @pl.when(e == 0)
    def _():
        x = x_ref[...]
        ms = jnp.mean(x * x, axis=-1, keepdims=True)
        h = x * lax.rsqrt(ms + EPS) * g_ref[...]
        h = h * (1.0 + sc_ref[...]) + sh_ref[...]
        h_ref[...] = h.astype(BF16)
        acc_ref[...] = jnp.zeros_like(acc_ref)
        if routed:
            logits = jnp.dot(h, rw_ref[...], precision=HIGHEST, preferred_element_type=F32) + rb_ref[...]
            comb = _top2_combine(logits)
            for ee in range(N_EXPERTS):
                comb_ref[ee] = jnp.broadcast_to(comb[:, ee:ee + 1], comb.shape)

    hb = h_ref[...]
    a = jnp.dot(hb, wa_ref[0], preferred_element_type=F32)
    b = jnp.dot(hb, wb_ref[0], preferred_element_type=F32)
    t = (_silu(a) * b).astype(BF16)
    y = jnp.dot(t, wd_ref[0], preferred_element_type=F32)
    if routed:
        w = comb_ref[e]
        for c in range(D_MODEL // LANES):
            cols = slice(c * LANES, (c + 1) * LANES)
            acc_ref[:, cols] += w * y[:, cols]
    else:
        acc_ref[...] += y

    @pl.when(e == n_chunks - 1)
    def _():
        out = x_ref[...] + g2_ref[...] * acc_ref[...]
        if final_norm:
            ms = jnp.mean(out * out, axis=-1, keepdims=True)
            out = out * lax.rsqrt(ms + EPS) * gf_ref[...]
        o_ref[...] = out


def _ffn(x, g, sc, sh, g2, wa, wb, wd, rw, rb, gf, tm, routed, final_norm):
    m = x.shape[0]
    n_chunks = wa.shape[0]
    mod_rows = sc.shape[0]
    mod_block = (1, D_MODEL) if mod_rows == 1 else (tm, D_MODEL)
    mod_map = (lambda i, e: (0, 0)) if mod_rows == 1 else (lambda i, e: (i, 0))
    row = lambda i, e: (i, 0)
    const2 = lambda i, e: (0, 0)
    return pl.pallas_call(
        functools.partial(_ffn_kernel, routed=routed, final_norm=final_norm, n_chunks=n_chunks),
        grid=(m // tm, n_chunks),
        in_specs=[pl.BlockSpec((tm, D_MODEL), row),
                  pl.BlockSpec((1, D_MODEL), const2),
                  pl.BlockSpec(mod_block, mod_map),
                  pl.BlockSpec(mod_block, mod_map),
                  pl.BlockSpec(mod_block, mod_map),
                  pl.BlockSpec((1, D_MODEL, D_FF_CHUNK), lambda i, e: (e, 0, 0)),
                  pl.BlockSpec((1, D_MODEL, D_FF_CHUNK), lambda i, e: (e, 0, 0)),
                  pl.BlockSpec((1, D_FF_CHUNK, D_MODEL), lambda i, e: (e, 0, 0)),
                  pl.BlockSpec((D_MODEL, LANES), const2),
                  pl.BlockSpec((1, LANES), const2),
                  pl.BlockSpec((1, D_MODEL), const2)],
        out_specs=pl.BlockSpec((tm, D_MODEL), row),
        out_shape=jax.ShapeDtypeStruct((m, D_MODEL), F32),
        scratch_shapes=[pltpu.VMEM((tm, D_MODEL), BF16),
                        pltpu.VMEM((tm, D_MODEL), F32),
                        pltpu.VMEM((N_EXPERTS, tm, LANES), F32)],
        compiler_params=_cparams(("arbitrary", "arbitrary")),
        name="ffn",
    )(x, g, sc, sh, g2, wa, wb, wd, rw, rb, gf)


def _pack_dense_ffn(w_gu, w_down):
    d_ff = w_down.shape[0]
    n = d_ff // D_FF_CHUNK
    wa = w_gu[:, :d_ff].reshape(D_MODEL, n, D_FF_CHUNK).transpose(1, 0, 2)
    wb = w_gu[:, d_ff:].reshape(D_MODEL, n, D_FF_CHUNK).transpose(1, 0, 2)
    wd = w_down.reshape(n, D_FF_CHUNK, D_MODEL)
    return wa.astype(BF16), wb.astype(BF16), wd.astype(BF16)


def _pack_moe_ffn(w_gu, w_down):
    return (w_gu[:, :, :D_FF_CHUNK].astype(BF16), w_gu[:, :, D_FF_CHUNK:].astype(BF16),
            w_down.astype(BF16))


PAGES_PER_STEP = 32
BLOCKS_PER_PAGE = PAGE_SIZE // CMP_BLOCK


def _compress_sample_kernel(pt_ref, *refs, n_steps):
    del pt_ref
    pages = refs[:2 * PAGES_PER_STEP]
    pe_ref, w1_ref, w2_ref, kc_ref, vc_ref, rows_ref, x_ref = refs[2 * PAGES_PER_STEP:]
    s = pl.program_id(1)
    nb = PAGES_PER_STEP * BLOCKS_PER_PAGE
    half = nb // 2
    for j in range(2):
        for p in range(PAGES_PER_STEP):
            tile = pages[j * PAGES_PER_STEP + p][...].reshape(N_KV * HEAD_DIM, PAGE_SIZE)
            rows_ref[j, p * PAGE_SIZE:(p + 1) * PAGE_SIZE, :] = tile.T
    for t in range(CMP_BLOCK):
        for j in range(2):
            even = rows_ref[j, pl.ds(t, half, stride=2 * CMP_BLOCK), :]
            odd = rows_ref[j, pl.ds(CMP_BLOCK + t, half, stride=2 * CMP_BLOCK), :]
            xj = jnp.concatenate([even, odd], axis=0) + pe_ref[t, j]
            x_ref[j, :, t * LANES:(t + 1) * LANES] = xj.astype(BF16)
    row = lax.broadcasted_iota(I32, (nb, LANES), 0)
    lane = lax.broadcasted_iota(I32, (nb, LANES), 1)
    c = s * nb + 2 * (row % half) + row // half
    aug = _pos_aug(c * CMP_BLOCK + (CMP_BLOCK - 1), lane)
    hid_k = jnp.dot(x_ref[0], w1_ref[0], preferred_element_type=F32)
    hid_v = jnp.dot(x_ref[1], w1_ref[1], preferred_element_type=F32)
    for g in range(N_KV):
        cols = slice(g * CMP_HIDDEN, (g + 1) * CMP_HIDDEN)
        hk = _silu(hid_k[:, cols]).astype(BF16)
        kc = (jnp.dot(hk, w2_ref[0], preferred_element_type=F32) + aug).astype(BF16)
        kc_ref[0, g, 0] = kc[:half]
        kc_ref[0, g, 1] = kc[half:]
        hv = _silu(hid_v[:, cols]).astype(BF16)
        vc = jnp.dot(hv, w2_ref[1], preferred_element_type=F32).astype(BF16)
        vc_ref[0, g, 0] = vc[:half]
        vc_ref[0, g, 1] = vc[half:]


def _compress_sample(cache_t, layer, page_table, pe, w1, w2):
    n_seq, n_pages = page_table.shape
    n_steps = n_pages // PAGES_PER_STEP
    nb = PAGES_PER_STEP * BLOCKS_PER_PAGE
    half = nb // 2
    nc_half = n_pages * BLOCKS_PER_PAGE // 2
    pe4 = jnp.concatenate([pe, pe], axis=-1).transpose(1, 0, 2).reshape(CMP_BLOCK, 2, 1, LANES)
    w1r = w1.reshape(2, CMP_BLOCK, HEAD_DIM, CMP_HIDDEN)
    w1p = jnp.einsum("gh,jtdn->jtgdhn", jnp.eye(N_KV, dtype=w1.dtype), w1r)
    w1p = w1p.reshape(2, CMP_BLOCK * LANES, N_KV * CMP_HIDDEN).astype(BF16)
    w2p = jnp.pad(w2, ((0, 0), (0, 0), (0, LANES - HEAD_DIM))).astype(BF16)

    def page_map(j, k):
        return lambda b, s, pt: (layer, pt[b * n_pages + s * PAGES_PER_STEP + k], j, 0, 0, 0)

    grid_spec = pltpu.PrefetchScalarGridSpec(
        num_scalar_prefetch=1,
        grid=(n_seq, n_steps),
        in_specs=[pl.BlockSpec((None, None, None, N_KV, HEAD_DIM, PAGE_SIZE), page_map(j, k))
                  for j in range(2) for k in range(PAGES_PER_STEP)] + [
            pl.BlockSpec((CMP_BLOCK, 2, 1, LANES), lambda b, s, pt: (0, 0, 0, 0)),
            pl.BlockSpec((2, CMP_BLOCK * LANES, N_KV * CMP_HIDDEN), lambda b, s, pt: (0, 0, 0)),
            pl.BlockSpec((2, CMP_HIDDEN, LANES), lambda b, s, pt: (0, 0, 0))],
        out_specs=(pl.BlockSpec((1, N_KV, 2, half, LANES), lambda b, s, pt: (b, 0, 0, s, 0)),
                   pl.BlockSpec((1, N_KV, 2, half, LANES), lambda b, s, pt: (b, 0, 0, s, 0))),
        scratch_shapes=[pltpu.VMEM((2, PAGES_PER_STEP * PAGE_SIZE, LANES), F32),
                        pltpu.VMEM((2, nb, CMP_BLOCK * LANES), BF16)],
    )
    kc, vc = pl.pallas_call(
        functools.partial(_compress_sample_kernel, n_steps=n_steps),
        grid_spec=grid_spec,
        out_shape=(jax.ShapeDtypeStruct((n_seq, N_KV, 2, nc_half, LANES), BF16),
                   jax.ShapeDtypeStruct((n_seq, N_KV, 2, nc_half, LANES), BF16)),
        compiler_params=_cparams(("arbitrary", "arbitrary")),
        name="compress_sample",
    )(jnp.pad(page_table, ((0, 1), (0, 0))).reshape(-1),
      *([cache_t] * (2 * PAGES_PER_STEP)), pe4, w1p, w2p)
    return (kc.reshape(n_seq, N_KV, 2 * nc_half, LANES), vc.reshape(n_seq, N_KV, 2 * nc_half, LANES))


def _row_slopes(shape):
    head = lax.broadcasted_iota(I32, shape, 0)
    out = jnp.zeros(shape, F32)
    for h in range(N_HEADS):
        out = jnp.where(head == h, 2.0 ** (-8.0 * (h + 1) / N_HEADS), out)
    return out


def _sample_select_kernel(q_ref, kc_ref, vc_ref, ocmp_ref, imp_ref, *, nc):
    q = q_ref[0]
    ns = nc // 2
    rowgroup = lax.broadcasted_iota(I32, (N_HEADS, 1), 0) // GROUP
    o = jnp.zeros((N_HEADS, LANES), F32)
    for g in range(N_KV):
        s = _dot_nt(q, kc_ref[0, g])
        m = jnp.max(s, axis=-1, keepdims=True)
        p = jnp.exp(s - m)
        p = p / jnp.maximum(jnp.sum(p, axis=-1, keepdims=True), 1e-30)
        og = jnp.dot(p.astype(BF16), vc_ref[0, g], preferred_element_type=F32)
        mine = rowgroup == g
        o = jnp.where(mine, og, o)
        ph = jnp.sum(jnp.where(mine, p, 0.0), axis=0, keepdims=True)
        imp_ref[0, pl.ds(g, 1), :] = ph[:, :ns] + ph[:, ns:]
    ocmp_ref[0] = o


def _sample_select(q_aug, kc, vc):
    n_seq, _, nc, _ = kc.shape
    ns = nc // 2
    return pl.pallas_call(
        functools.partial(_sample_select_kernel, nc=nc),
        grid=(n_seq,),
        in_specs=[pl.BlockSpec((1, N_HEADS, LANES), lambda b: (b, 0, 0)),
                  pl.BlockSpec((1, N_KV, nc, LANES), lambda b: (b, 0, 0, 0)),
                  pl.BlockSpec((1, N_KV, nc, LANES), lambda b: (b, 0, 0, 0))],
        out_specs=(pl.BlockSpec((1, N_HEADS, LANES), lambda b: (b, 0, 0)),
                   pl.BlockSpec((1, N_KV, ns), lambda b: (b, 0, 0))),
        out_shape=(jax.ShapeDtypeStruct((n_seq, N_HEADS, LANES), F32),
                   jax.ShapeDtypeStruct((n_seq, N_KV, ns), F32)),
        compiler_params=_cparams(("arbitrary",)),
        name="sample_select",
    )(q_aug, kc, vc)


def _sample_topk_kernel(imp_ref, ids_ref, *, ns, k):
    imp = imp_ref[...]
    lane = lax.broadcasted_iota(I32, imp.shape, 1)
    out_lane = lax.broadcasted_iota(I32, ids_ref.shape, 1)
    big = jnp.int32(2 ** 30)
    v = jnp.where((lane == 0) | (lane == ns - 1), FORCED, imp)
    ids = jnp.zeros(ids_ref.shape, I32)
    for t in range(k):
        m = jnp.max(v, axis=-1, keepdims=True)
        first = jnp.min(jnp.where(v == m, lane, big), axis=-1, keepdims=True)
        v = jnp.where(lane == first, REMOVED, v)
        ids = jnp.where(out_lane == t, first, ids)
    ids_ref[...] = ids


def _sample_topk(imp, k):
    rows, ns = imp.shape
    return pl.pallas_call(
        functools.partial(_sample_topk_kernel, ns=ns, k=k),
        out_shape=jax.ShapeDtypeStruct((rows, LANES), I32),
        name="sample_topk",
    )(imp)


SEL_PAST = SEL_TOPK - 1


def _sample_attend_kernel(pt_ref, ids_ref, *refs, past_len):
    del pt_ref
    nblk = N_KV * SEL_PAST
    blocks = refs[:nblk]
    win_ref, new_ref, q_ref, ocmp_ref, gate_ref, o_ref = refs[nblk:]
    b = pl.program_id(0)
    slope = _row_slopes((N_HEADS, 1))
    rowgroup = lax.broadcasted_iota(I32, (N_HEADS, 1), 0) // GROUP
    gates = gate_ref[0]
    w_buf = win_ref.shape[-1]
    q = q_ref[0]
    qf = q.astype(F32)
    lane = lax.broadcasted_iota(I32, (1, PAGE_SIZE), 1)
    o = jnp.zeros((N_HEADS, HEAD_DIM), F32)
    for g in range(N_KV):
        def new_row(j):
            return new_ref[0, j * N_KV + g:j * N_KV + g + 1, :].astype(BF16).astype(F32)

        s_new = jnp.sum(qf * new_row(2), axis=-1, keepdims=True)
        scores, oks = [], []
        for t in range(SEL_PAST):
            blk = ids_ref[(b * N_KV + g) * SEL_TOPK + t]
            kt = blocks[g * SEL_PAST + t][0].astype(BF16)
            in_blk = (lane // SEL_BLOCK) == (blk % 2)
            pos = (blk // 2) * PAGE_SIZE + lane
            s = jnp.dot(q, kt, preferred_element_type=F32) - slope * (past_len - pos).astype(F32)
            scores.append(jnp.where(in_blk, s, NEG))
            oks.append(in_blk)
        m = s_new
        for s in scores:
            m = jnp.maximum(m, jnp.max(s, axis=-1, keepdims=True))
        p_new = jnp.exp(s_new - m)
        den = p_new
        acc = p_new * new_row(3)
        for t in range(SEL_PAST):
            p = jnp.where(oks[t], jnp.exp(scores[t] - m), 0.0)
            den = den + jnp.sum(p, axis=-1, keepdims=True)
            acc = acc + _dot_nt(p, blocks[g * SEL_PAST + t][1])
        o_sel = acc / jnp.maximum(den, 1e-30)
        kwt = win_ref[0, g].astype(BF16)
        dist = w_buf - lax.broadcasted_iota(I32, (1, w_buf), 1)
        ok = dist < WINDOW
        s = jnp.where(ok, jnp.dot(q, kwt, preferred_element_type=F32) - slope * dist.astype(F32), NEG)
        s_new = jnp.sum(qf * new_row(4), axis=-1, keepdims=True)
        m = jnp.maximum(s_new, jnp.max(s, axis=-1, keepdims=True))
        p = jnp.where(ok, jnp.exp(s - m), 0.0)
        p_new = jnp.exp(s_new - m)
        den = p_new + jnp.sum(p, axis=-1, keepdims=True)
        o_win = (p_new * new_row(5) + _dot_nt(p, win_ref[1, g])) / jnp.maximum(den, 1e-30)
        og = gates[:, 0:1] * ocmp_ref[0][:, :HEAD_DIM] + gates[:, 1:2] * o_sel + gates[:, 2:3] * o_win
        o = jnp.where(rowgroup == g, og, o)
    o_ref[0] = o


def _sample_attend(cache_t, win_t, layer, page_table, ids, kv_new, q, ocmp, gates, past_len):
    n_seq, n_pages = page_table.shape
    w_buf = win_t.shape[-1]
    ids3 = ids.reshape(n_seq, N_KV, SEL_TOPK)
    phys = jnp.take_along_axis(page_table[:, None, :], ids3 // 2, axis=-1)
    phys = jnp.pad(phys, ((0, 1), (0, 0), (0, 0))).astype(I32)

    def blk_map(g, t):
        return lambda b, ph, idr: (layer, ph[(b * N_KV + g) * SEL_TOPK + t], 1, g, 0, 0)

    grid_spec = pltpu.PrefetchScalarGridSpec(
        num_scalar_prefetch=2,
        grid=(n_seq,),
        in_specs=[pl.BlockSpec((None, None, 2, None, HEAD_DIM, PAGE_SIZE), blk_map(g, t))
                  for g in range(N_KV) for t in range(SEL_PAST)] + [
            pl.BlockSpec((None, None, 2, N_KV, HEAD_DIM, w_buf), lambda b, pt, idr: (layer, b, 0, 0, 0, 0)),
            pl.BlockSpec((1, 6 * N_KV, HEAD_DIM), lambda b, pt, idr: (b, 0, 0)),
            pl.BlockSpec((1, N_HEADS, HEAD_DIM), lambda b, pt, idr: (b, 0, 0)),
            pl.BlockSpec((1, N_HEADS, LANES), lambda b, pt, idr: (b, 0, 0)),
            pl.BlockSpec((1, N_HEADS, LANES), lambda b, pt, idr: (b, 0, 0))],
        out_specs=pl.BlockSpec((1, N_HEADS, HEAD_DIM), lambda b, pt, idr: (b, 0, 0)),
    )
    return pl.pallas_call(
        functools.partial(_sample_attend_kernel, past_len=past_len),
        grid_spec=grid_spec,
        out_shape=jax.ShapeDtypeStruct((n_seq, N_HEADS, HEAD_DIM), F32),
        compiler_params=_cparams(("arbitrary",)),
        name="sample_attend",
    )(phys.reshape(-1), ids.reshape(-1), *([cache_t] * (N_KV * SEL_PAST)), win_t, kv_new, q, ocmp, gates)


def _nsa_sample(cache_t, win_t, layer, page_table, kv_new, qt, gt, pe, w1, w2):
    n_seq, n_pages = page_table.shape
    past_len = n_pages * PAGE_SIZE
    kc, vc = _compress_sample(cache_t, layer, page_table, pe, w1, w2)
    q = qt.T.reshape(n_seq, N_HEADS, HEAD_DIM)
    slopes = (2.0 ** (-8.0 * jnp.arange(1, N_HEADS + 1, dtype=F32) / N_HEADS)).reshape(1, N_HEADS, 1)
    aug = jnp.concatenate([slopes * 128.0, slopes, -(slopes * 128.0) * (past_len // 128),
                           jnp.zeros((1, N_HEADS, LANES - HEAD_DIM - 3), F32)], axis=-1)
    q_aug = jnp.concatenate([q, jnp.broadcast_to(aug, (n_seq, N_HEADS, LANES - HEAD_DIM)).astype(BF16)], axis=-1)
    ocmp, imp = _sample_select(q_aug, kc, vc)
    ids = _sample_topk(imp.reshape(n_seq * N_KV, -1), SEL_PAST)[:, :SEL_TOPK]
    gates = gt[:3 * N_HEADS].reshape(3, N_HEADS, n_seq).transpose(2, 1, 0)
    gates = jnp.pad(gates, ((0, 0), (0, 0), (0, LANES - 3)))
    o = _sample_attend(cache_t, win_t, layer, page_table, ids, kv_new.reshape(n_seq, 6 * N_KV, HEAD_DIM), q,
                       ocmp, gates, past_len)
    return o.reshape(n_seq, D_NSA).astype(BF16)


def _sg_chunk_params(sg_w, sg_b):
    w = jnp.tril(sg_w).astype(BF16)
    bias = jnp.repeat(sg_b.T, SG_DIM, axis=1)
    return w, bias


def _pad_rows(a, mult):
    pad = (-a.shape[0]) % mult
    return jnp.pad(a, ((0, pad),) + ((0, 0),) * (a.ndim - 1)) if pad else a


def _prep_weights(norm_mix_g, norm_ffn_g, norm_final_g, w_in, cmp_pe, cmp_w1, cmp_w2, sg_norm_g, sg_norm_b,
                  sg_w, sg_b, w_branch_nsa, w_branch_sg, w_out, ffn_w_gu, ffn_w_down, router_w, router_b,
                  moe_w_gu, moe_w_down):
    depth = w_in.shape[0]
    layers = []
    for i in range(depth):
        wn, wt = _pack_w_in(w_in[i])
        sgw_chunk, sgb_chunk = _sg_chunk_params(sg_w[i], sg_b[i])
        lw = {
            "norm_mix": norm_mix_g[i].reshape(1, D_MODEL), "norm_ffn": norm_ffn_g[i].reshape(1, D_MODEL),
            "norm_final": norm_final_g.reshape(1, D_MODEL), "wn": wn, "wt": wt,
            "sg_norm_g": sg_norm_g[i].reshape(1, D_SG), "sg_norm_b": sg_norm_b[i].reshape(1, D_SG),
            "cmp_pe": cmp_pe[i], "cmp_w1": cmp_w1[i], "cmp_w2": cmp_w2[i],
            "wa": w_branch_nsa[i].astype(BF16), "wb": w_branch_sg[i].astype(BF16), "wo": w_out[i].astype(BF16),
            "sgw_chunk": sgw_chunk, "sgb_chunk": sgb_chunk,
            "sgw_first": jnp.repeat(sg_w[i][:, 0, 0], SG_DIM).reshape(1, D_SG),
            "sgb_first": jnp.repeat(sg_b[i][:, 0], SG_DIM).reshape(1, D_SG),
            "routed": i % 2 == 1, "final": i == depth - 1,
        }
        if i % 2 == 0:
            lw["ffn_a"], lw["ffn_b"], lw["ffn_d"] = _pack_dense_ffn(ffn_w_gu[i // 2], ffn_w_down[i // 2])
            lw["router_w"] = jnp.zeros((D_MODEL, LANES), F32)
            lw["router_b"] = jnp.zeros((1, LANES), F32)
        else:
            lw["ffn_a"], lw["ffn_b"], lw["ffn_d"] = _pack_moe_ffn(moe_w_gu[i // 2], moe_w_down[i // 2])
            lw["router_w"] = jnp.pad(router_w[i // 2], ((0, 0), (0, LANES - N_EXPERTS)))
            lw["router_b"] = jnp.pad(router_b[i // 2], (0, LANES - N_EXPERTS)).reshape(1, LANES)
        layers.append(lw)
    return layers


def _prompt_layer(x, mod, lw, tm):
    sh1, sc1, g1, sh2, sc2, g2 = mod
    kvt, u, v, ga, gb, kaug, qt, vt, gt = _in_proj(x, lw["norm_mix"], sc1, sh1, lw["wn"], lw["wt"],
                                                    lw["sg_norm_g"], lw["sg_norm_b"], tm, False)
    onsa = _nsa_prompt(kvt, kaug, qt, vt, gt, lw["cmp_pe"], lw["cmp_w1"], lw["cmp_w2"])
    x = _mix_out(x, onsa, u, v, ga, gb, g1, lw["wa"], lw["wb"], lw["wo"], lw["sgw_chunk"], lw["sgb_chunk"],
                 tm, True)
    x = _ffn(x, lw["norm_ffn"], sc2, sh2, g2, lw["ffn_a"], lw["ffn_b"], lw["ffn_d"], lw["router_w"],
             lw["router_b"], lw["norm_final"], lw["ffn_tm"], lw["routed"], lw["final"])
    return x, kvt


def _sample_layer(x, mod, lw, cache_t, win_t, layer, page_table):
    sh1, sc1, g1, sh2, sc2, g2 = mod
    tm = x.shape[0]
    kvt, u, v, ga, gb, _, qt, _, gt, kv = _in_proj(x, lw["norm_mix"], sc1, sh1, lw["wn"], lw["wt"],
                                                   lw["sg_norm_g"], lw["sg_norm_b"], tm, True)
    onsa = _nsa_sample(cache_t, win_t, layer, page_table, kv, qt, gt, lw["cmp_pe"], lw["cmp_w1"], lw["cmp_w2"])
    x = _mix_out(x, onsa, u, v, ga, gb, g1, lw["wa"], lw["wb"], lw["wo"], lw["sgw_first"], lw["sgb_first"],
                 tm, False)
    x = _ffn(x, lw["norm_ffn"], sc2, sh2, g2, lw["ffn_a"], lw["ffn_b"], lw["ffn_d"], lw["router_w"],
             lw["router_b"], lw["norm_final"], tm, lw["routed"], lw["final"])
    return x, kv, kvt, v


PROMPT_ROW_TILE = 512
PROMPT_FFN_ROW_TILE = 512


def kernel(x_prompt, x_sample, cache_kv, state_win, page_table, c_prompt, c_sample, norm_mix_g, norm_ffn_g,
           norm_final_g, w_ada, b_ada, w_in, cmp_pe, cmp_w1, cmp_w2, sg_norm_g, sg_norm_b, sg_w, sg_b,
           w_branch_nsa, w_branch_sg, w_out, ffn_w_gu, ffn_w_down, router_w, router_b, moe_w_gu, moe_w_down):
    batch, seq, _ = x_prompt.shape
    n_seq, dec_seq, _ = x_sample.shape
    depth = w_in.shape[0]
    assert batch == 1 and dec_seq == 1
    assert seq % PROMPT_FFN_ROW_TILE == 0 and seq // Q_BLOCK <= 256
    past_len = page_table.shape[1] * PAGE_SIZE
    assert past_len % CHUNK == 0 and past_len // LANES <= 256
    assert state_win.shape[2] == WINDOW and page_table.shape[1] % PAGES_PER_STEP == 0

    layers = _prep_weights(norm_mix_g, norm_ffn_g, norm_final_g, w_in, cmp_pe, cmp_w1, cmp_w2, sg_norm_g,
                           sg_norm_b, sg_w, sg_b, w_branch_nsa, w_branch_sg, w_out, ffn_w_gu, ffn_w_down,
                           router_w, router_b, moe_w_gu, moe_w_down)
    c_all = _pad_rows(jnp.concatenate([c_prompt, c_sample], axis=0), SUBLANES)
    mods = _ada(c_all, w_ada, b_ada)

    cache_t = cache_kv.transpose(0, 1, 3, 4, 5, 2)
    win_t = state_win.transpose(0, 1, 3, 4, 5, 2)

    xp = x_prompt[0]
    xs = x_sample[:, 0]
    kv_p, kv_s, win_p, win_new, sgv_s = [], [], [], [], []
    w_keep = min(WINDOW, seq)
    kv_rows = 4 * N_KV * HEAD_DIM
    for i in range(depth):
        lw = dict(layers[i])
        lw["ffn_tm"] = PROMPT_FFN_ROW_TILE
        mod_p = tuple(mods[i, 0:1, j * D_MODEL:(j + 1) * D_MODEL] for j in range(6))
        mod_s = tuple(mods[i, 1:1 + n_seq, j * D_MODEL:(j + 1) * D_MODEL] for j in range(6))
        xp, kvt_p = _prompt_layer(xp, mod_p, lw, PROMPT_ROW_TILE)
        xs, kvs, kvt_s, v_s = _sample_layer(xs, mod_s, lw, cache_t, win_t, i, page_table)
        kv_p.append(kvt_p[:kv_rows].reshape(4, N_KV, HEAD_DIM, seq))
        win_p.append(kvt_p[kv_rows:, seq - w_keep:].reshape(2, N_KV, HEAD_DIM, w_keep))
        kv_s.append(kvs[:, :kv_rows].reshape(n_seq, 1, 4, N_KV, HEAD_DIM))
        win_new.append(kvt_s[kv_rows:].reshape(2, N_KV, HEAD_DIM, n_seq).transpose(3, 0, 1, 2)[..., None])
        sgv_s.append(v_s.reshape(n_seq, 1, D_SG))
    kv_prompt = jnp.stack(kv_p).transpose(0, 4, 1, 2, 3)[:, None]
    win_prompt = jnp.stack(win_p).transpose(0, 4, 1, 2, 3)[:, None]
    win_sample = jnp.concatenate([win_t[..., 1:], jnp.stack(win_new)], axis=-1).transpose(0, 1, 5, 2, 3, 4)
    return (xp[None], xs[:, None], kv_prompt, jnp.stack(kv_s), win_prompt, win_sample, jnp.stack(sgv_s))
```

```python
import functools

import jax
import jax.numpy as jnp
from jax import lax
from jax.experimental import pallas as pl
from jax.experimental.pallas import tpu as pltpu

F32 = jnp.float32
BF16 = jnp.bfloat16
I32 = jnp.int32
HIGHEST = lax.Precision.HIGHEST

LANES = 128
SUBLANES = 8
VMEM_LIMIT_BYTES = 56 * 1024 * 1024

D_MODEL = 1024
N_HEADS = 8
HEAD_DIM = 64
N_KV = 2
GROUP = N_HEADS // N_KV
D_NSA = N_HEADS * HEAD_DIM
CMP_BLOCK = 32
CMP_HIDDEN = 256
SEL_BLOCK = 64
SEL_TOPK = 16
WINDOW = 512
Q_BLOCK = 128
N_SG = 8
SG_DIM = 64
D_SG = N_SG * SG_DIM
CHUNK = 128
N_EXPERTS = 8
D_FF_CHUNK = 1408
PAGE_SIZE = 128
EPS = 1e-6
NEG = -1e30
FORCED = 1e9
REMOVED = -3e38

KV_COLS = 6 * N_KV * HEAD_DIM
OFF_KV = D_NSA
OFF_G = OFF_KV + KV_COLS
OFF_U = OFF_G + 3 * N_HEADS
OFF_V = OFF_U + D_SG
OFF_GA = OFF_V + D_SG
OFF_GB = OFF_GA + D_MODEL
IN_COLS = OFF_GB + D_MODEL

WN_U = 0
WN_V = WN_U + D_SG
WN_GA = WN_V + D_SG
WN_GB = WN_GA + D_MODEL
WN_KAUG = WN_GB + D_MODEL
WN_KV = WN_KAUG + 4 * LANES
WN_COLS = WN_KV + KV_COLS
WT_Q = 0
WT_KV = D_NSA
WT_G = WT_KV + KV_COLS
WT_ROWS = WT_G + 32

POS_HI_LANE = HEAD_DIM
POS_LO_LANE = HEAD_DIM + 1
ONE_LANE = HEAD_DIM + 2


def _cparams(sem):
    return pltpu.CompilerParams(dimension_semantics=sem, vmem_limit_bytes=VMEM_LIMIT_BYTES)


def _bdot(a, b):
    return jnp.dot(a.astype(BF16), b.astype(BF16), preferred_element_type=F32)


def _dot_nt(a, b):
    return lax.dot_general(a.astype(BF16), b.astype(BF16), (((1,), (1,)), ((), ())),
                           preferred_element_type=F32)


def _mm(a, b, precise):
    if precise:
        return jnp.dot(a.astype(F32), b.astype(F32), precision=HIGHEST, preferred_element_type=F32)
    return _bdot(a, b)


def _mm_nt(a, b, precise):
    if precise:
        return lax.dot_general(a.astype(F32), b.astype(F32), (((1,), (1,)), ((), ())), precision=HIGHEST,
                               preferred_element_type=F32)
    return _dot_nt(a, b)


def _silu(x):
    return x * jax.nn.sigmoid(x)


def _pos_aug(pos, lane):
    hi = (pos >> 7).astype(F32)
    lo = (pos & 127).astype(F32)
    return jnp.where(lane == POS_HI_LANE, hi,
                     jnp.where(lane == POS_LO_LANE, lo,
                               jnp.where(lane == ONE_LANE, 1.0, 0.0)))


def _group_slopes(g, lane_head):
    out = jnp.zeros(lane_head.shape, F32)
    for gg in range(N_KV):
        for hh in range(GROUP):
            s = 2.0 ** (-8.0 * (gg * GROUP + hh + 1) / N_HEADS)
            out = jnp.where((lane_head == hh) & (g == gg), s, out)
    return out


def _qt_aug(qt_ref, g, q0_blocks):
    heads = [qt_ref[h * HEAD_DIM:(h + 1) * HEAD_DIM, :] for h in range(GROUP)]
    q = jnp.concatenate(heads, axis=1)
    n = GROUP * Q_BLOCK
    row = lax.broadcasted_iota(I32, (LANES - HEAD_DIM, n), 0)
    lane_head = lax.broadcasted_iota(I32, (LANES - HEAD_DIM, n), 1) // Q_BLOCK
    slope = _group_slopes(g, lane_head)
    q0f = (q0_blocks).astype(F32)
    aug = jnp.where(row == 0, slope * 128.0,
                    jnp.where(row == 1, slope,
                              jnp.where(row == 2, -(slope * 128.0) * q0f, 0.0)))
    return jnp.concatenate([q.astype(BF16), aug.astype(BF16)], axis=0)


def _ada_kernel(c_ref, w_ref, b_ref, o_ref):
    c = c_ref[...]
    o_ref[0] = jnp.dot(_silu(c), w_ref[0], precision=HIGHEST, preferred_element_type=F32) + b_ref[0]


def _ada(c_all, w_ada, b_ada):
    depth = w_ada.shape[0]
    rows = c_all.shape[0]
    tn = 1024
    n = w_ada.shape[2]
    return pl.pallas_call(
        _ada_kernel,
        grid=(depth, n // tn),
        in_specs=[pl.BlockSpec((rows, D_MODEL), lambda l, j: (0, 0)),
                  pl.BlockSpec((1, D_MODEL, tn), lambda l, j: (l, 0, j)),
                  pl.BlockSpec((1, 1, tn), lambda l, j: (l, 0, j))],
        out_specs=pl.BlockSpec((1, rows, tn), lambda l, j: (l, 0, j)),
        out_shape=jax.ShapeDtypeStruct((depth, rows, n), F32),
        compiler_params=_cparams(("arbitrary", "arbitrary")),
        name="ada",
    )(c_all, w_ada, b_ada.reshape(depth, 1, n))


def _in_proj_kernel(x_ref, g_ref, sc_ref, sh_ref, wn_ref, wt_ref, lng_ref, lnb_ref, *out_refs, tm, natural_kv,
                    precise):
    if natural_kv:
        kvt_ref, u_ref, v_ref, ga_ref, gb_ref, kaug_ref, qt_ref, vt_ref, gt_ref, kv_ref = out_refs
    else:
        kvt_ref, u_ref, v_ref, ga_ref, gb_ref, kaug_ref, qt_ref, vt_ref, gt_ref = out_refs
    i = pl.program_id(0)
    x = x_ref[...]
    ms = jnp.mean(x * x, axis=-1, keepdims=True)
    h = x * lax.rsqrt(ms + EPS) * g_ref[...]
    h = h * (1.0 + sc_ref[...]) + sh_ref[...]
    hb = h if precise else h.astype(BF16)

    def seg(a, b):
        return _mm(hb, wn_ref[:, a:b], precise)

    u_ref[...] = jax.nn.gelu(seg(WN_U, WN_V)).astype(u_ref.dtype)
    v = jax.nn.gelu(seg(WN_V, WN_GA))
    mu = jnp.mean(v, axis=-1, keepdims=True)
    var = jnp.mean(jnp.square(v - mu), axis=-1, keepdims=True)
    v_ref[...] = (v - mu) * lax.rsqrt(var + EPS) * lng_ref[...] + lnb_ref[...]
    ga_ref[...] = jax.nn.sigmoid(seg(WN_GA, WN_GB)).astype(ga_ref.dtype)
    gb_ref[...] = jax.nn.sigmoid(seg(WN_GB, WN_KAUG)).astype(gb_ref.dtype)

    pos = lax.broadcasted_iota(I32, (tm, LANES), 0) + i * tm
    lane = lax.broadcasted_iota(I32, (tm, LANES), 1)
    aug = _pos_aug(pos, lane)
    for j in range(4):
        k = seg(WN_KAUG + j * LANES, WN_KAUG + (j + 1) * LANES)
        kaug_ref[j] = (k + aug).astype(BF16)
    if natural_kv:
        kv_ref[...] = seg(WN_KV, WN_COLS)

    zt = _mm_nt(wt_ref[...], hb, precise)
    qt_ref[...] = (zt[WT_Q:WT_KV] * (HEAD_DIM ** -0.5)).astype(qt_ref.dtype)
    kvt = zt[WT_KV:WT_G]
    kvt_ref[...] = kvt
    two = N_KV * HEAD_DIM
    vt_ref[0:two, :] = kvt[3 * two:4 * two].astype(BF16)
    vt_ref[two:2 * two, :] = kvt[5 * two:6 * two].astype(BF16)
    gt_ref[...] = jax.nn.sigmoid(zt[WT_G:WT_ROWS])


def _in_proj(x, g, sc, sh, wn, wt, lng, lnb, tm, natural_kv, precise):
    m = x.shape[0]
    act = F32 if precise else BF16
    mod_rows = sc.shape[0]
    mod_block = (1, D_MODEL) if mod_rows == 1 else (tm, D_MODEL)
    mod_map = (lambda i: (0, 0)) if mod_rows == 1 else (lambda i: (i, 0))
    row = lambda i: (i, 0)
    col = lambda i: (0, i)
    const = lambda i: (0, 0)
    out_shape = [
        jax.ShapeDtypeStruct((KV_COLS, m), F32),
        jax.ShapeDtypeStruct((m, D_SG), act),
        jax.ShapeDtypeStruct((m, D_SG), F32),
        jax.ShapeDtypeStruct((m, D_MODEL), act),
        jax.ShapeDtypeStruct((m, D_MODEL), act),
        jax.ShapeDtypeStruct((4, m, LANES), BF16),
        jax.ShapeDtypeStruct((D_NSA, m), act),
        jax.ShapeDtypeStruct((4 * HEAD_DIM, m), BF16),
        jax.ShapeDtypeStruct((32, m), F32),
    ]
    out_specs = [
        pl.BlockSpec((KV_COLS, tm), col),
        pl.BlockSpec((tm, D_SG), row),
        pl.BlockSpec((tm, D_SG), row),
        pl.BlockSpec((tm, D_MODEL), row),
        pl.BlockSpec((tm, D_MODEL), row),
        pl.BlockSpec((4, tm, LANES), lambda i: (0, i, 0)),
        pl.BlockSpec((D_NSA, tm), col),
        pl.BlockSpec((4 * HEAD_DIM, tm), col),
        pl.BlockSpec((32, tm), col),
    ]
    if natural_kv:
        out_shape.append(jax.ShapeDtypeStruct((m, KV_COLS), F32))
        out_specs.append(pl.BlockSpec((tm, KV_COLS), row))
    return pl.pallas_call(
        functools.partial(_in_proj_kernel, tm=tm, natural_kv=natural_kv, precise=precise),
        grid=(m // tm,),
        in_specs=[pl.BlockSpec((tm, D_MODEL), row),
                  pl.BlockSpec((1, D_MODEL), const),
                  pl.BlockSpec(mod_block, mod_map),
                  pl.BlockSpec(mod_block, mod_map),
                  pl.BlockSpec((D_MODEL, WN_COLS), const),
                  pl.BlockSpec((WT_ROWS, D_MODEL), const),
                  pl.BlockSpec((1, D_SG), const),
                  pl.BlockSpec((1, D_SG), const)],
        out_specs=tuple(out_specs),
        out_shape=tuple(out_shape),
        compiler_params=_cparams(("arbitrary",)),
        name="in_proj",
    )(x, g, sc, sh, wn, wt, lng, lnb)


def _pack_w_in(w):
    kv = w[:, OFF_KV:OFF_G]

    def kvcol(j, g):
        return kv[:, (j * N_KV + g) * HEAD_DIM:(j * N_KV + g + 1) * HEAD_DIM]

    zpad = jnp.zeros((D_MODEL, LANES - HEAD_DIM), w.dtype)
    kaug = [jnp.concatenate([kvcol(j, g), zpad], axis=1) for j in (2, 4) for g in range(N_KV)]
    wn = jnp.concatenate([w[:, OFF_U:OFF_V], w[:, OFF_V:OFF_GA], w[:, OFF_GA:OFF_GB],
                          w[:, OFF_GB:IN_COLS]] + kaug + [kv], axis=1)
    gpad = jnp.zeros((D_MODEL, 32 - 3 * N_HEADS), w.dtype)
    wt = jnp.concatenate([w[:, :OFF_KV], kv, w[:, OFF_G:OFF_U], gpad], axis=1).T
    return wn, wt


def _cmp_block_end(nc):
    r = lax.broadcasted_iota(I32, (nc, LANES), 0)
    half = nc // 2
    c = 2 * (r % half) + r // half
    return c * CMP_BLOCK + (CMP_BLOCK - 1)


def _compress_kernel(xk_ref, xv_ref, pe_ref, w1_ref, w2k_ref, w2vt_ref, kc_ref, vct_ref, *, nc):
    xk = xk_ref[0, 0] + pe_ref[0]
    hk = _silu(_bdot(xk, w1_ref[0]))
    kc = _bdot(hk, w2k_ref[...])
    lane = lax.broadcasted_iota(I32, (nc, LANES), 1)
    kc_ref[0] = (kc + _pos_aug(_cmp_block_end(nc), lane)).astype(BF16)
    xv = xv_ref[0, 0] + pe_ref[1]
    hv = _silu(_bdot(xv, w1_ref[1]))
    vct_ref[0] = _dot_nt(w2vt_ref[...], hv).astype(BF16)


def _compress_prompt(kvt, pe, w1, w2):
    m = kvt.shape[1]
    nc = m // CMP_BLOCK
    half = nc // 2
    kvc = kvt[:2 * N_KV * HEAD_DIM].reshape(2, N_KV, HEAD_DIM, half, 2, CMP_BLOCK)
    x = kvc.transpose(0, 1, 4, 3, 5, 2).reshape(2, N_KV, nc, CMP_BLOCK * HEAD_DIM)
    pe_flat = pe.reshape(2, 1, CMP_BLOCK * HEAD_DIM)
    w2k = jnp.concatenate([w2[0], jnp.zeros((CMP_HIDDEN, LANES - HEAD_DIM), w2.dtype)], axis=1).astype(BF16)
    w2vt = w2[1].T.astype(BF16)
    kd = CMP_BLOCK * HEAD_DIM
    return pl.pallas_call(
        functools.partial(_compress_kernel, nc=nc),
        grid=(N_KV,),
        in_specs=[pl.BlockSpec((1, 1, nc, kd), lambda g: (0, g, 0, 0)),
                  pl.BlockSpec((1, 1, nc, kd), lambda g: (1, g, 0, 0)),
                  pl.BlockSpec((2, 1, kd), lambda g: (0, 0, 0)),
                  pl.BlockSpec((2, kd, CMP_HIDDEN), lambda g: (0, 0, 0)),
                  pl.BlockSpec((CMP_HIDDEN, LANES), lambda g: (0, 0)),
                  pl.BlockSpec((HEAD_DIM, CMP_HIDDEN), lambda g: (0, 0))],
        out_specs=(pl.BlockSpec((1, nc, LANES), lambda g: (g, 0, 0)),
                   pl.BlockSpec((1, HEAD_DIM, nc), lambda g: (g, 0, 0))),
        out_shape=(jax.ShapeDtypeStruct((N_KV, nc, LANES), BF16),
                   jax.ShapeDtypeStruct((N_KV, HEAD_DIM, nc), BF16)),
        compiler_params=_cparams(("arbitrary",)),
        name="compress_prompt",
    )(x, x, pe_flat, w1.astype(BF16), w2k, w2vt)


def _topk_mask_rows(imp, k):
    rows = lax.broadcasted_iota(I32, imp.shape, 0)
    big = jnp.int32(2 ** 30)

    def body(_, v):
        m = jnp.max(v, axis=0, keepdims=True)
        first = jnp.min(jnp.where(v == m, rows, big), axis=0, keepdims=True)
        return jnp.where(rows == first, REMOVED, v)

    return jnp.where(lax.fori_loop(0, k, body, imp) < 0.5 * REMOVED, 1.0, 0.0)


def _visited_blocks(sel, limit):
    ns = sel.shape[0]
    ones_q = jnp.ones((SUBLANES, sel.shape[1]), BF16)
    blk = lax.broadcasted_iota(I32, (SUBLANES, ns), 1)
    flags = ((_dot_nt(ones_q, sel) > 0.5) & (blk < limit)).astype(F32)
    r = lax.broadcasted_iota(I32, (ns, ns), 0)
    c = lax.broadcasted_iota(I32, (ns, ns), 1)
    upper = jnp.where(r <= c, 1.0, 0.0)
    prefix = _bdot(flags, upper)
    before = jnp.where(prefix[0:1, :] <= r.astype(F32), 1.0, 0.0)
    ids = _dot_nt(jnp.ones((SUBLANES, ns), BF16), before)
    ids = jnp.minimum(ids, ns - 1.0).astype(I32)
    total = jnp.broadcast_to(prefix[:, ns - 1:ns], (SUBLANES, ns))
    return ids, total


def _select_kernel(qt_ref, kc_ref, vct_ref, mask_ref, ids_ref, cnt_ref, ocmp_ref, *, nc):
    i = pl.program_id(0)
    ns = nc // 2
    n = GROUP * Q_BLOCK
    q0 = i * Q_BLOCK
    cend = _cmp_block_end(nc)[:, :1]
    qpos = q0 + lax.broadcasted_iota(I32, (1, n), 1) % Q_BLOCK
    valid = cend <= qpos
    blk = lax.broadcasted_iota(I32, (ns, Q_BLOCK), 0)
    cur = (q0 + lax.broadcasted_iota(I32, (ns, Q_BLOCK), 1)) // SEL_BLOCK
    forced = (blk == 0) | (blk == cur) | (blk == cur - 1)
    imps = []
    for g in range(N_KV):
        qa = _qt_aug(qt_ref.at[g * GROUP * HEAD_DIM:(g + 1) * GROUP * HEAD_DIM], g, i)
        s = jnp.dot(kc_ref[g], qa, preferred_element_type=F32)
        s = jnp.where(valid, s, NEG)
        m = jnp.max(s, axis=0, keepdims=True)
        p = jnp.where(valid, jnp.exp(s - m), 0.0)
        p = p / jnp.maximum(jnp.sum(p, axis=0, keepdims=True), 1e-30)
        ocmp_ref[0, g] = jnp.dot(vct_ref[g], p.astype(BF16), preferred_element_type=F32)
        ph = p[:, 0:Q_BLOCK]
        for h in range(1, GROUP):
            ph = ph + p[:, h * Q_BLOCK:(h + 1) * Q_BLOCK]
        imp = ph[:ns] + ph[ns:]
        imp = jnp.where(forced, FORCED, imp)
        imps.append(jnp.where(blk <= cur, imp, NEG))
    for g in range(N_KV):
        sel = jnp.where(blk <= cur, _topk_mask_rows(imps[g], min(SEL_TOPK, ns)), 0.0)
        mask_ref[0, g] = sel
        ids_ref[0, g], cnt_ref[0, g] = _visited_blocks(sel, 2 * i)


def _select(qt, kc, vct):
    m = qt.shape[1]
    nqb = m // Q_BLOCK
    nc = m // CMP_BLOCK
    ns = nc // 2
    n = GROUP * Q_BLOCK
    return pl.pallas_call(
        functools.partial(_select_kernel, nc=nc),
        grid=(nqb,),
        in_specs=[pl.BlockSpec((D_NSA, Q_BLOCK), lambda i: (0, i)),
                  pl.BlockSpec((N_KV, nc, LANES), lambda i: (0, 0, 0)),
                  pl.BlockSpec((N_KV, HEAD_DIM, nc), lambda i: (0, 0, 0))],
        out_specs=(pl.BlockSpec((1, N_KV, ns, Q_BLOCK), lambda i: (i, 0, 0, 0)),
                   pl.BlockSpec((1, N_KV, SUBLANES, ns), lambda i: (i, 0, 0, 0)),
                   pl.BlockSpec((1, N_KV, SUBLANES, ns), lambda i: (i, 0, 0, 0)),
                   pl.BlockSpec((1, N_KV, HEAD_DIM, n), lambda i: (i, 0, 0, 0))),
        out_shape=(jax.ShapeDtypeStruct((nqb, N_KV, ns, Q_BLOCK), F32),
                   jax.ShapeDtypeStruct((nqb, N_KV, SUBLANES, ns), I32),
                   jax.ShapeDtypeStruct((nqb, N_KV, SUBLANES, ns), F32),
                   jax.ShapeDtypeStruct((nqb, N_KV, HEAD_DIM, n), F32)),
        compiler_params=_cparams(("arbitrary",)),
        name="nsa_select",
    )(qt, kc, vct)


SEL_PER_STEP = 8
WIN_TILES_PER_STEP = 4


def _softmax_step(state, s, valid=None):
    m, l, _ = state
    if valid is not None:
        s = jnp.where(valid, s, NEG)
    m_new = jnp.maximum(m, jnp.max(s, axis=0, keepdims=True))
    alpha = jnp.exp(m - m_new)
    p = jnp.exp(s - m_new)
    if valid is not None:
        p = jnp.where(valid, p, 0.0)
    return m_new, alpha * l + jnp.sum(p, axis=0, keepdims=True), alpha, p.astype(BF16)


def _pv_lane_tiles(vt_tiles):
    vt = jnp.concatenate(vt_tiles, axis=1)
    return lambda pb: jnp.dot(vt, pb, preferred_element_type=F32)


def _pv_row_tiles(v_tiles):
    v = jnp.concatenate(v_tiles, axis=0)
    return lambda pb: lax.dot_general(v, pb, (((0,), (0,)), ((), ())), preferred_element_type=F32)


def _attend_kernel(cnt_ref, ids_ref, qt_ref, ksel_ref, vsel_ref, kwin_ref, vwin_ref, mask_ref,
                   ocmp_ref, gt_ref, o_ref, *, nqb, ns):
    g = pl.program_id(0)
    i = pl.program_id(1)
    q0 = i * Q_BLOCK
    qa = _qt_aug(qt_ref, g, i)
    qa_heads = [qa[:, h * Q_BLOCK:(h + 1) * Q_BLOCK] for h in range(GROUP)]
    empty = (jnp.full((1, Q_BLOCK), NEG, F32), jnp.zeros((1, Q_BLOCK), F32), jnp.zeros((HEAD_DIM, Q_BLOCK), F32))
    init = (empty,) * GROUP

    def heads_update(states, keys, pv_fn, bias=None, valid=None):
        scores = [jnp.dot(keys, qa_heads[h], preferred_element_type=F32) for h in range(GROUP)]
        if bias is not None:
            scores = [s + bias for s in scores]
        parts = [_softmax_step(states[h], scores[h], valid) for h in range(GROUP)]
        pvs = [pv_fn(p) for _, _, _, p in parts]
        return tuple((m, l, alpha * states[h][2] + pvs[h]) for h, (m, l, alpha, _) in enumerate(parts))

    slot = g * nqb + i
    count = cnt_ref[slot]
    key_off = lax.broadcasted_iota(I32, (Q_BLOCK, 1), 0)
    query_off = lax.broadcasted_iota(I32, (1, Q_BLOCK), 1)
    causal = key_off <= query_off

    own = mask_ref[0, 0, pl.ds(pl.multiple_of(2 * i, 2), 2), :]
    chosen = jnp.concatenate([jnp.broadcast_to(own[0:1] > 0.5, (SEL_BLOCK, Q_BLOCK)),
                              jnp.broadcast_to(own[1:2] > 0.5, (SEL_BLOCK, Q_BLOCK))], axis=0)
    k_own = ksel_ref[0, pl.ds(pl.multiple_of(q0, Q_BLOCK), Q_BLOCK), :]
    first = heads_update(init, k_own, _pv_row_tiles([vsel_ref[0, 2 * i], vsel_ref[0, 2 * i + 1]]),
                         valid=chosen & causal)

    def sel_body(t, states):
        ks, vts, biases = [], [], []
        for u in range(SEL_PER_STEP):
            e = t * SEL_PER_STEP + u
            b = ids_ref[slot * ns + e]
            ks.append(ksel_ref[0, pl.ds(pl.multiple_of(b * SEL_BLOCK, SEL_BLOCK), SEL_BLOCK), :])
            row = mask_ref[0, 0, pl.ds(b, 1), :]
            live = (row > 0.5) & (e < count)
            biases.append(jnp.broadcast_to(jnp.where(live, 0.0, NEG), (SEL_BLOCK, Q_BLOCK)))
            vts.append(vsel_ref[0, b])
        return heads_update(states, jnp.concatenate(ks, axis=0), _pv_row_tiles(vts),
                            bias=jnp.concatenate(biases, axis=0))

    sel_states = lax.fori_loop(0, (count + (SEL_PER_STEP - 1)) // SEL_PER_STEP, sel_body, first)

    k_own = kwin_ref[0, pl.ds(pl.multiple_of(q0, Q_BLOCK), Q_BLOCK), :]
    win_states = heads_update(init, k_own, _pv_lane_tiles([vwin_ref[0, i]]), valid=causal)
    n_old = WINDOW // Q_BLOCK
    oldest_in_window = key_off > query_off
    for j0 in range(0, n_old, WIN_TILES_PER_STEP):
        ks, vts, biases = [], [], []
        for j in range(j0, min(j0 + WIN_TILES_PER_STEP, n_old)):
            tile = i - n_old + j
            held = jnp.maximum(tile, 0)
            ks.append(kwin_ref[0, pl.ds(pl.multiple_of(held * Q_BLOCK, Q_BLOCK), Q_BLOCK), :])
            inside = tile >= 0
            live = (oldest_in_window & inside) if j == 0 else inside
            biases.append(jnp.broadcast_to(jnp.where(live, 0.0, NEG), (Q_BLOCK, Q_BLOCK)))
            vts.append(vwin_ref[0, held])
        win_states = heads_update(win_states, jnp.concatenate(ks, axis=0), _pv_lane_tiles(vts),
                                  bias=jnp.concatenate(biases, axis=0))

    def head_out(h):
        def gate(branch):
            return gt_ref[pl.ds(branch * N_HEADS + g * GROUP + h, 1), :]
        _, l_s, acc_s = sel_states[h]
        _, l_w, acc_w = win_states[h]
        return (ocmp_ref[0, 0, :, h * Q_BLOCK:(h + 1) * Q_BLOCK] * gate(0)
                + acc_s / jnp.maximum(l_s, 1e-30) * gate(1) + acc_w / jnp.maximum(l_w, 1e-30) * gate(2))

    for hp in range(GROUP // 2):
        pair = jnp.concatenate([head_out(2 * hp), head_out(2 * hp + 1)], axis=0)
        o_ref[:, hp * LANES:(hp + 1) * LANES] = pair.T.astype(o_ref.dtype)


def _attend(cnt, ids, qt, kaug, vt_blocks_sel, vt_blocks_win, mask, ocmp, gt):
    m = qt.shape[1]
    nqb = m // Q_BLOCK
    ns = m // SEL_BLOCK
    n = GROUP * Q_BLOCK
    gh = GROUP * HEAD_DIM
    grid_spec = pltpu.PrefetchScalarGridSpec(
        num_scalar_prefetch=2,
        grid=(N_KV, nqb),
        in_specs=[pl.BlockSpec((gh, Q_BLOCK), lambda g, i, c, d: (g, i)),
                  pl.BlockSpec((1, m, LANES), lambda g, i, c, d: (g, 0, 0)),
                  pl.BlockSpec((1, ns, SEL_BLOCK, HEAD_DIM), lambda g, i, c, d: (g, 0, 0, 0)),
                  pl.BlockSpec((1, m, LANES), lambda g, i, c, d: (N_KV + g, 0, 0)),
                  pl.BlockSpec((1, nqb, HEAD_DIM, Q_BLOCK), lambda g, i, c, d: (g, 0, 0, 0)),
                  pl.BlockSpec((1, 1, ns, Q_BLOCK), lambda g, i, c, d: (i, g, 0, 0)),
                  pl.BlockSpec((1, 1, HEAD_DIM, n), lambda g, i, c, d: (i, g, 0, 0)),
                  pl.BlockSpec((32, Q_BLOCK), lambda g, i, c, d: (0, i))],
        out_specs=pl.BlockSpec((Q_BLOCK, gh), lambda g, i, c, d: (i, g)),
    )
    return pl.pallas_call(
        functools.partial(_attend_kernel, nqb=nqb, ns=ns),
        grid_spec=grid_spec,
        out_shape=jax.ShapeDtypeStruct((m, D_NSA), BF16),
        compiler_params=_cparams(("arbitrary", "arbitrary")),
        name="nsa_attend",
    )(cnt, ids, qt, kaug, vt_blocks_sel, kaug, vt_blocks_win, mask, ocmp, gt)


def _nsa_prompt(kvt, kaug, qt, vt, gt, pe, w1, w2):
    m = kvt.shape[1]
    nqb = m // Q_BLOCK
    ns = m // SEL_BLOCK
    kc, vct = _compress_prompt(kvt, pe, w1, w2)
    mask, ids8, cnt8, ocmp = _select(qt, kc, vct)
    ids = ids8[:, :, 0, :].transpose(1, 0, 2).reshape(-1)
    cnt = cnt8[:, :, 0, 0].T.astype(I32).reshape(-1)
    vt4 = vt.reshape(4, HEAD_DIM, m)
    vsel = vt4[:N_KV].reshape(N_KV, HEAD_DIM, ns, SEL_BLOCK).transpose(0, 2, 3, 1)
    vwin = vt4[N_KV:].reshape(N_KV, HEAD_DIM, nqb, Q_BLOCK).transpose(0, 2, 1, 3)
    return _attend(cnt, ids, qt, kaug, vsel, vwin, mask, ocmp, gt)


def _mix_kernel(x_ref, onsa_ref, u_ref, v_ref, ga_ref, gb_ref, g1_ref, wa_ref, wb_ref, wo_ref,
                sgw_ref, sgb_ref, o_ref, osg_ref, *, tm, chunked, precise):
    if chunked:
        lane = lax.broadcasted_iota(I32, (CHUNK, LANES), 1)
        for c in range(tm // CHUNK):
            rows = slice(c * CHUNK, (c + 1) * CHUNK)
            for pr in range(N_SG // 2):
                cols = slice(pr * LANES, (pr + 1) * LANES)
                vp = v_ref[rows, cols].astype(BF16)
                a = jnp.dot(sgw_ref[2 * pr], vp, preferred_element_type=F32)
                b = jnp.dot(sgw_ref[2 * pr + 1], vp, preferred_element_type=F32)
                mix = jnp.where(lane < SG_DIM, a, b) + sgb_ref[:, cols]
                osg_ref[rows, cols] = (u_ref[rows, cols].astype(F32) * mix).astype(BF16)
    else:
        mix = v_ref[...] * sgw_ref[...] + sgb_ref[...]
        osg_ref[...] = (u_ref[...].astype(F32) * mix).astype(osg_ref.dtype)
    a = _mm(onsa_ref[...], wa_ref[...], precise)
    b = _mm(osg_ref[...], wb_ref[...], precise)
    merged = ga_ref[...].astype(F32) * a + gb_ref[...].astype(F32) * b
    y = _mm(merged, wo_ref[...], precise)
    o_ref[...] = x_ref[...] + g1_ref[...] * y


def _mix_out(x, onsa, u, v, ga, gb, g1, wa, wb, wo, sgw, sgb, tm, chunked, precise):
    m = x.shape[0]
    mod_rows = g1.shape[0]
    mod_block = (1, D_MODEL) if mod_rows == 1 else (tm, D_MODEL)
    mod_map = (lambda i: (0, 0)) if mod_rows == 1 else (lambda i: (i, 0))
    row = lambda i: (i, 0)
    const2 = lambda i: (0, 0)
    if chunked:
        sg_specs = [pl.BlockSpec((N_SG, CHUNK, CHUNK), lambda i: (0, 0, 0)),
                    pl.BlockSpec((CHUNK, D_SG), const2)]
    else:
        sg_specs = [pl.BlockSpec((1, D_SG), const2), pl.BlockSpec((1, D_SG), const2)]
    return pl.pallas_call(
        functools.partial(_mix_kernel, tm=tm, chunked=chunked, precise=precise),
        grid=(m // tm,),
        in_specs=[pl.BlockSpec((tm, D_MODEL), row),
                  pl.BlockSpec((tm, D_NSA), row),
                  pl.BlockSpec((tm, D_SG), row),
                  pl.BlockSpec((tm, D_SG), row),
                  pl.BlockSpec((tm, D_MODEL), row),
                  pl.BlockSpec((tm, D_MODEL), row),
                  pl.BlockSpec(mod_block, mod_map),
                  pl.BlockSpec((D_NSA, D_MODEL), const2),
                  pl.BlockSpec((D_SG, D_MODEL), const2),
                  pl.BlockSpec((D_MODEL, D_MODEL), const2)] + sg_specs,
        out_specs=pl.BlockSpec((tm, D_MODEL), row),
        out_shape=jax.ShapeDtypeStruct((m, D_MODEL), F32),
        scratch_shapes=[pltpu.VMEM((tm, D_SG), F32 if precise else BF16)],
        compiler_params=_cparams(("arbitrary",)),
        name="mix_out",
    )(x, onsa, u, v, ga, gb, g1, wa, wb, wo, sgw, sgb)


def _top2_combine(logits):
    lane = lax.broadcasted_iota(I32, logits.shape, 1)
    big = jnp.int32(2 ** 30)
    z = jnp.where(lane < N_EXPERTS, logits, -jnp.inf)
    t1 = jnp.max(z, axis=-1, keepdims=True)
    i1 = jnp.min(jnp.where(z == t1, lane, big), axis=-1, keepdims=True)
    z2 = jnp.where(lane == i1, -jnp.inf, z)
    t2 = jnp.max(z2, axis=-1, keepdims=True)
    i2 = jnp.min(jnp.where(z2 == t2, lane, big), axis=-1, keepdims=True)
    e = jnp.exp(t2 - t1)
    den = 1.0 + e
    return jnp.where(lane == i1, 1.0 / den, 0.0) + jnp.where(lane == i2, e / den, 0.0)


def _ffn_kernel(x_ref, g_ref, sc_ref, sh_ref, g2_ref, wa_ref, wb_ref, wd_ref, rw_ref, rb_ref, gf_ref,
                o_ref, h_ref, acc_ref, comb_ref, *, routed, final_norm, n_chunks, precise):
    e = pl.program_id(1)

    @pl.when(e == 0)
    def _():
        x = x_ref[...]
        ms = jnp.mean(x * x, axis=-1, keepdims=True)
        h = x * lax.rsqrt(ms + EPS) * g_ref[...]
        h = h * (1.0 + sc_ref[...]) + sh_ref[...]
        h_ref[...] = h.astype(h_ref.dtype)
        acc_ref[...] = jnp.zeros_like(acc_ref)
        if routed:
            logits = jnp.dot(h, rw_ref[...], precision=HIGHEST, preferred_element_type=F32) + rb_ref[...]
            comb = _top2_combine(logits)
            for ee in range(N_EXPERTS):
                comb_ref[ee] = jnp.broadcast_to(comb[:, ee:ee + 1], comb.shape)

    hb = h_ref[...]
    a = _mm(hb, wa_ref[0], precise)
    b = _mm(hb, wb_ref[0], precise)
    y = _mm(_silu(a) * b, wd_ref[0], precise)
    if routed:
        w = comb_ref[e]
        for c in range(D_MODEL // LANES):
            cols = slice(c * LANES, (c + 1) * LANES)
            acc_ref[:, cols] += w * y[:, cols]
    else:
        acc_ref[...] += y

    @pl.when(e == n_chunks - 1)
    def _():
        out = x_ref[...] + g2_ref[...] * acc_ref[...]
        if final_norm:
            ms = jnp.mean(out * out, axis=-1, keepdims=True)
            out = out * lax.rsqrt(ms + EPS) * gf_ref[...]
        o_ref[...] = out


def _ffn(x, g, sc, sh, g2, gu, wd, rw, rb, gf, tm, routed, final_norm, precise):
    m = x.shape[0]
    n_chunks = wd.shape[0]
    if routed:
        a_map, b_map = (lambda i, e: (e, 0, 0)), (lambda i, e: (e, 0, 1))
    else:
        a_map, b_map = (lambda i, e: (0, 0, e)), (lambda i, e: (0, 0, n_chunks + e))
    mod_rows = sc.shape[0]
    mod_block = (1, D_MODEL) if mod_rows == 1 else (tm, D_MODEL)
    mod_map = (lambda i, e: (0, 0)) if mod_rows == 1 else (lambda i, e: (i, 0))
    row = lambda i, e: (i, 0)
    const2 = lambda i, e: (0, 0)
    return pl.pallas_call(
        functools.partial(_ffn_kernel, routed=routed, final_norm=final_norm, n_chunks=n_chunks,
                          precise=precise),
        grid=(m // tm, n_chunks),
        in_specs=[pl.BlockSpec((tm, D_MODEL), row),
                  pl.BlockSpec((1, D_MODEL), const2),
                  pl.BlockSpec(mod_block, mod_map),
                  pl.BlockSpec(mod_block, mod_map),
                  pl.BlockSpec(mod_block, mod_map),
                  pl.BlockSpec((1, D_MODEL, D_FF_CHUNK), a_map),
                  pl.BlockSpec((1, D_MODEL, D_FF_CHUNK), b_map),
                  pl.BlockSpec((1, D_FF_CHUNK, D_MODEL), lambda i, e: (e, 0, 0)),
                  pl.BlockSpec((D_MODEL, LANES), const2),
                  pl.BlockSpec((1, LANES), const2),
                  pl.BlockSpec((1, D_MODEL), const2)],
        out_specs=pl.BlockSpec((tm, D_MODEL), row),
        out_shape=jax.ShapeDtypeStruct((m, D_MODEL), F32),
        scratch_shapes=[pltpu.VMEM((tm, D_MODEL), F32 if precise else BF16),
                        pltpu.VMEM((tm, D_MODEL), F32),
                        pltpu.VMEM((N_EXPERTS, tm, LANES), F32)],
        compiler_params=_cparams(("arbitrary", "arbitrary")),
        name="ffn",
    )(x, g, sc, sh, g2, gu, gu, wd, rw, rb, gf)


PAGES_PER_STEP = 32
BLOCKS_PER_PAGE = PAGE_SIZE // CMP_BLOCK


def _compress_sample_kernel(pt_ref, *refs, n_steps):
    del pt_ref
    pages = refs[:PAGES_PER_STEP]
    pe_ref, w1_ref, w2_ref, kc_ref, vc_ref, rows_ref, x_ref = refs[PAGES_PER_STEP:]
    s = pl.program_id(1)
    nb = PAGES_PER_STEP * BLOCKS_PER_PAGE
    half = nb // 2
    for j in range(2):
        for p in range(PAGES_PER_STEP):
            tile = pages[p][j].reshape(N_KV * HEAD_DIM, PAGE_SIZE)
            rows_ref[j, p * PAGE_SIZE:(p + 1) * PAGE_SIZE, :] = tile.T
    for t in range(CMP_BLOCK):
        for j in range(2):
            even = rows_ref[j, pl.ds(t, half, stride=2 * CMP_BLOCK), :]
            odd = rows_ref[j, pl.ds(CMP_BLOCK + t, half, stride=2 * CMP_BLOCK), :]
            xj = jnp.concatenate([even, odd], axis=0) + pe_ref[t, j]
            x_ref[j, :, t * LANES:(t + 1) * LANES] = xj.astype(BF16)
    row = lax.broadcasted_iota(I32, (nb, LANES), 0)
    lane = lax.broadcasted_iota(I32, (nb, LANES), 1)
    c = s * nb + 2 * (row % half) + row // half
    aug = _pos_aug(c * CMP_BLOCK + (CMP_BLOCK - 1), lane)
    hid_k = jnp.dot(x_ref[0], w1_ref[0], preferred_element_type=F32)
    hid_v = jnp.dot(x_ref[1], w1_ref[1], preferred_element_type=F32)
    for g in range(N_KV):
        cols = slice(g * CMP_HIDDEN, (g + 1) * CMP_HIDDEN)
        hk = _silu(hid_k[:, cols]).astype(BF16)
        kc = (jnp.dot(hk, w2_ref[0], preferred_element_type=F32) + aug).astype(BF16)
        kc_ref[0, g, 0] = kc[:half]
        kc_ref[0, g, 1] = kc[half:]
        hv = _silu(hid_v[:, cols]).astype(BF16)
        vc = jnp.dot(hv, w2_ref[1], preferred_element_type=F32).astype(BF16)
        vc_ref[0, g, 0] = vc[:half]
        vc_ref[0, g, 1] = vc[half:]


def _compress_sample(cache_t, layer, page_table, pe, w1, w2):
    n_seq, n_pages = page_table.shape
    n_steps = n_pages // PAGES_PER_STEP
    nb = PAGES_PER_STEP * BLOCKS_PER_PAGE
    half = nb // 2
    nc_half = n_pages * BLOCKS_PER_PAGE // 2
    pe4 = jnp.concatenate([pe, pe], axis=-1).transpose(1, 0, 2).reshape(CMP_BLOCK, 2, 1, LANES)
    w1r = w1.reshape(2, CMP_BLOCK, HEAD_DIM, CMP_HIDDEN)
    w1p = jnp.einsum("gh,jtdn->jtgdhn", jnp.eye(N_KV, dtype=w1.dtype), w1r)
    w1p = w1p.reshape(2, CMP_BLOCK * LANES, N_KV * CMP_HIDDEN).astype(BF16)
    w2p = jnp.pad(w2, ((0, 0), (0, 0), (0, LANES - HEAD_DIM))).astype(BF16)

    def page_map(k):
        return lambda b, s, pt: (layer, pt[b * n_pages + s * PAGES_PER_STEP + k], 0, 0, 0, 0)

    grid_spec = pltpu.PrefetchScalarGridSpec(
        num_scalar_prefetch=1,
        grid=(n_seq, n_steps),
        in_specs=[pl.BlockSpec((None, None, 2, N_KV, HEAD_DIM, PAGE_SIZE), page_map(k))
                  for k in range(PAGES_PER_STEP)] + [
            pl.BlockSpec((CMP_BLOCK, 2, 1, LANES), lambda b, s, pt: (0, 0, 0, 0)),
            pl.BlockSpec((2, CMP_BLOCK * LANES, N_KV * CMP_HIDDEN), lambda b, s, pt: (0, 0, 0)),
            pl.BlockSpec((2, CMP_HIDDEN, LANES), lambda b, s, pt: (0, 0, 0))],
        out_specs=(pl.BlockSpec((1, N_KV, 2, half, LANES), lambda b, s, pt: (b, 0, 0, s, 0)),
                   pl.BlockSpec((1, N_KV, 2, half, LANES), lambda b, s, pt: (b, 0, 0, s, 0))),
        scratch_shapes=[pltpu.VMEM((2, PAGES_PER_STEP * PAGE_SIZE, LANES), F32),
                        pltpu.VMEM((2, nb, CMP_BLOCK * LANES), BF16)],
    )
    kc, vc = pl.pallas_call(
        functools.partial(_compress_sample_kernel, n_steps=n_steps),
        grid_spec=grid_spec,
        out_shape=(jax.ShapeDtypeStruct((n_seq, N_KV, 2, nc_half, LANES), BF16),
                   jax.ShapeDtypeStruct((n_seq, N_KV, 2, nc_half, LANES), BF16)),
        compiler_params=_cparams(("arbitrary", "arbitrary")),
        name="compress_sample",
    )(jnp.pad(page_table, ((0, 1), (0, 0))).reshape(-1),
      *([cache_t] * PAGES_PER_STEP), pe4, w1p, w2p)
    return (kc.reshape(n_seq, N_KV, 2 * nc_half, LANES), vc.reshape(n_seq, N_KV, 2 * nc_half, LANES))


def _row_slopes(shape):
    head = lax.broadcasted_iota(I32, shape, 0)
    out = jnp.zeros(shape, F32)
    for h in range(N_HEADS):
        out = jnp.where(head == h, 2.0 ** (-8.0 * (h + 1) / N_HEADS), out)
    return out


def _sample_select_kernel(q_ref, kc_ref, vc_ref, ocmp_ref, imp_ref, *, nc):
    q = q_ref[0]
    ns = nc // 2
    rowgroup = lax.broadcasted_iota(I32, (N_HEADS, 1), 0) // GROUP
    o = jnp.zeros((N_HEADS, LANES), F32)
    for g in range(N_KV):
        s = _dot_nt(q, kc_ref[0, g])
        m = jnp.max(s, axis=-1, keepdims=True)
        p = jnp.exp(s - m)
        p = p / jnp.maximum(jnp.sum(p, axis=-1, keepdims=True), 1e-30)
        og = jnp.dot(p.astype(BF16), vc_ref[0, g], preferred_element_type=F32)
        mine = rowgroup == g
        o = jnp.where(mine, og, o)
        ph = jnp.sum(jnp.where(mine, p, 0.0), axis=0, keepdims=True)
        imp_ref[0, pl.ds(g, 1), :] = ph[:, :ns] + ph[:, ns:]
    ocmp_ref[0] = o


def _sample_select(q_aug, kc, vc):
    n_seq, _, nc, _ = kc.shape
    ns = nc // 2
    return pl.pallas_call(
        functools.partial(_sample_select_kernel, nc=nc),
        grid=(n_seq,),
        in_specs=[pl.BlockSpec((1, N_HEADS, LANES), lambda b: (b, 0, 0)),
                  pl.BlockSpec((1, N_KV, nc, LANES), lambda b: (b, 0, 0, 0)),
                  pl.BlockSpec((1, N_KV, nc, LANES), lambda b: (b, 0, 0, 0))],
        out_specs=(pl.BlockSpec((1, N_HEADS, LANES), lambda b: (b, 0, 0)),
                   pl.BlockSpec((1, N_KV, ns), lambda b: (b, 0, 0))),
        out_shape=(jax.ShapeDtypeStruct((n_seq, N_HEADS, LANES), F32),
                   jax.ShapeDtypeStruct((n_seq, N_KV, ns), F32)),
        compiler_params=_cparams(("arbitrary",)),
        name="sample_select",
    )(q_aug, kc, vc)


def _sample_topk_kernel(imp_ref, ids_ref, *, ns, k):
    imp = imp_ref[...]
    lane = lax.broadcasted_iota(I32, imp.shape, 1)
    out_lane = lax.broadcasted_iota(I32, ids_ref.shape, 1)
    big = jnp.int32(2 ** 30)
    v = jnp.where((lane == 0) | (lane == ns - 1), FORCED, imp)
    ids = jnp.zeros(ids_ref.shape, I32)
    for t in range(k):
        m = jnp.max(v, axis=-1, keepdims=True)
        first = jnp.min(jnp.where(v == m, lane, big), axis=-1, keepdims=True)
        v = jnp.where(lane == first, REMOVED, v)
        ids = jnp.where(out_lane == t, first, ids)
    ids_ref[...] = ids


def _sample_topk(imp, k):
    rows, ns = imp.shape
    return pl.pallas_call(
        functools.partial(_sample_topk_kernel, ns=ns, k=k),
        out_shape=jax.ShapeDtypeStruct((rows, LANES), I32),
        name="sample_topk",
    )(imp)


SEL_PAST = SEL_TOPK - 1


def _sample_attend_kernel(pt_ref, ids_ref, *refs, past_len):
    del pt_ref
    nblk = N_KV * SEL_PAST
    blocks = refs[:nblk]
    win_ref, new_ref, q_ref, ocmp_ref, gate_ref, o_ref = refs[nblk:]
    b = pl.program_id(0)
    slope = _row_slopes((N_HEADS, 1))
    rowgroup = lax.broadcasted_iota(I32, (N_HEADS, 1), 0) // GROUP
    gates = gate_ref[0]
    w_buf = win_ref.shape[-1]
    q = q_ref[0]
    qf = q.astype(F32)
    lane = lax.broadcasted_iota(I32, (1, PAGE_SIZE), 1)
    o = jnp.zeros((N_HEADS, HEAD_DIM), F32)
    for g in range(N_KV):
        def new_row(j):
            return new_ref[0, j * N_KV + g:j * N_KV + g + 1, :].astype(BF16).astype(F32)

        s_new = jnp.sum(qf * new_row(2), axis=-1, keepdims=True)
        scores, oks = [], []
        for t in range(SEL_PAST):
            blk = ids_ref[(b * N_KV + g) * SEL_TOPK + t]
            kt = blocks[g * SEL_PAST + t][0].astype(BF16)
            in_blk = (lane // SEL_BLOCK) == (blk % 2)
            pos = (blk // 2) * PAGE_SIZE + lane
            s = jnp.dot(q, kt, preferred_element_type=F32) - slope * (past_len - pos).astype(F32)
            scores.append(jnp.where(in_blk, s, NEG))
            oks.append(in_blk)
        m = s_new
        for s in scores:
            m = jnp.maximum(m, jnp.max(s, axis=-1, keepdims=True))
        p_new = jnp.exp(s_new - m)
        den = p_new
        acc = p_new * new_row(3)
        for t in range(SEL_PAST):
            p = jnp.where(oks[t], jnp.exp(scores[t] - m), 0.0)
            den = den + jnp.sum(p, axis=-1, keepdims=True)
            acc = acc + _dot_nt(p, blocks[g * SEL_PAST + t][1])
        o_sel = acc / jnp.maximum(den, 1e-30)
        kwt = win_ref[0, g].astype(BF16)
        dist = w_buf - lax.broadcasted_iota(I32, (1, w_buf), 1)
        ok = dist < WINDOW
        s = jnp.where(ok, jnp.dot(q, kwt, preferred_element_type=F32) - slope * dist.astype(F32), NEG)
        s_new = jnp.sum(qf * new_row(4), axis=-1, keepdims=True)
        m = jnp.maximum(s_new, jnp.max(s, axis=-1, keepdims=True))
        p = jnp.where(ok, jnp.exp(s - m), 0.0)
        p_new = jnp.exp(s_new - m)
        den = p_new + jnp.sum(p, axis=-1, keepdims=True)
        o_win = (p_new * new_row(5) + _dot_nt(p, win_ref[1, g])) / jnp.maximum(den, 1e-30)
        og = gates[:, 0:1] * ocmp_ref[0][:, :HEAD_DIM] + gates[:, 1:2] * o_sel + gates[:, 2:3] * o_win
        o = jnp.where(rowgroup == g, og, o)
    o_ref[0] = o


def _sample_attend(cache_t, win_t, layer, page_table, ids, kv_new, q, ocmp, gates, past_len):
    n_seq, n_pages = page_table.shape
    w_buf = win_t.shape[-1]
    ids3 = ids.reshape(n_seq, N_KV, SEL_TOPK)
    phys = jnp.take_along_axis(page_table[:, None, :], ids3 // 2, axis=-1)
    phys = jnp.pad(phys, ((0, 1), (0, 0), (0, 0))).astype(I32)

    def blk_map(g, t):
        return lambda b, ph, idr: (layer, ph[(b * N_KV + g) * SEL_TOPK + t], 1, g, 0, 0)

    grid_spec = pltpu.PrefetchScalarGridSpec(
        num_scalar_prefetch=2,
        grid=(n_seq,),
        in_specs=[pl.BlockSpec((None, None, 2, None, HEAD_DIM, PAGE_SIZE), blk_map(g, t))
                  for g in range(N_KV) for t in range(SEL_PAST)] + [
            pl.BlockSpec((None, None, 2, N_KV, HEAD_DIM, w_buf), lambda b, pt, idr: (layer, b, 0, 0, 0, 0)),
            pl.BlockSpec((1, 6 * N_KV, HEAD_DIM), lambda b, pt, idr: (b, 0, 0)),
            pl.BlockSpec((1, N_HEADS, HEAD_DIM), lambda b, pt, idr: (b, 0, 0)),
            pl.BlockSpec((1, N_HEADS, LANES), lambda b, pt, idr: (b, 0, 0)),
            pl.BlockSpec((1, N_HEADS, LANES), lambda b, pt, idr: (b, 0, 0))],
        out_specs=pl.BlockSpec((1, N_HEADS, HEAD_DIM), lambda b, pt, idr: (b, 0, 0)),
    )
    return pl.pallas_call(
        functools.partial(_sample_attend_kernel, past_len=past_len),
        grid_spec=grid_spec,
        out_shape=jax.ShapeDtypeStruct((n_seq, N_HEADS, HEAD_DIM), F32),
        compiler_params=_cparams(("arbitrary",)),
        name="sample_attend",
    )(phys.reshape(-1), ids.reshape(-1), *([cache_t] * (N_KV * SEL_PAST)), win_t, kv_new, q, ocmp, gates)


def _nsa_sample(cache_t, win_t, layer, page_table, kv_new, qt, gt, pe, w1, w2):
    n_seq, n_pages = page_table.shape
    past_len = n_pages * PAGE_SIZE
    kc, vc = _compress_sample(cache_t, layer, page_table, pe, w1, w2)
    q = qt.T.reshape(n_seq, N_HEADS, HEAD_DIM).astype(BF16)
    slopes = (2.0 ** (-8.0 * jnp.arange(1, N_HEADS + 1, dtype=F32) / N_HEADS)).reshape(1, N_HEADS, 1)
    aug = jnp.concatenate([slopes * 128.0, slopes, -(slopes * 128.0) * (past_len // 128),
                           jnp.zeros((1, N_HEADS, LANES - HEAD_DIM - 3), F32)], axis=-1)
    q_aug = jnp.concatenate([q, jnp.broadcast_to(aug, (n_seq, N_HEADS, LANES - HEAD_DIM)).astype(BF16)], axis=-1)
    ocmp, imp = _sample_select(q_aug, kc, vc)
    ids = _sample_topk(imp.reshape(n_seq * N_KV, -1), SEL_PAST)[:, :SEL_TOPK]
    gates = gt[:3 * N_HEADS].reshape(3, N_HEADS, n_seq).transpose(2, 1, 0)
    gates = jnp.pad(gates, ((0, 0), (0, 0), (0, LANES - 3)))
    o = _sample_attend(cache_t, win_t, layer, page_table, ids, kv_new.reshape(n_seq, 6 * N_KV, HEAD_DIM), q,
                       ocmp, gates, past_len)
    return o.reshape(n_seq, D_NSA)


def _sg_chunk_params(sg_w, sg_b):
    w = jnp.tril(sg_w).astype(BF16)
    bias = jnp.repeat(sg_b.T, SG_DIM, axis=1)
    return w, bias


def _pad_rows(a, mult):
    pad = (-a.shape[0]) % mult
    return jnp.pad(a, ((0, pad),) + ((0, 0),) * (a.ndim - 1)) if pad else a


def _prep_weights(norm_mix_g, norm_ffn_g, norm_final_g, w_in, cmp_pe, cmp_w1, cmp_w2, sg_norm_g, sg_norm_b,
                  sg_w, sg_b, w_branch_nsa, w_branch_sg, w_out, ffn_w_gu, ffn_w_down, router_w, router_b,
                  moe_w_gu, moe_w_down):
    depth = w_in.shape[0]
    layers = []
    for i in range(depth):
        wn, wt = _pack_w_in(w_in[i])
        sgw_chunk, sgb_chunk = _sg_chunk_params(sg_w[i], sg_b[i])
        lw = {
            "norm_mix": norm_mix_g[i].reshape(1, D_MODEL), "norm_ffn": norm_ffn_g[i].reshape(1, D_MODEL),
            "norm_final": norm_final_g.reshape(1, D_MODEL),
            "sg_norm_g": sg_norm_g[i].reshape(1, D_SG), "sg_norm_b": sg_norm_b[i].reshape(1, D_SG),
            "cmp_pe": cmp_pe[i], "cmp_w1": cmp_w1[i], "cmp_w2": cmp_w2[i],
            "f32": {"wn": wn, "wt": wt, "wa": w_branch_nsa[i], "wb": w_branch_sg[i], "wo": w_out[i]},
            "bf16": {"wn": wn.astype(BF16), "wt": wt.astype(BF16), "wa": w_branch_nsa[i].astype(BF16),
                     "wb": w_branch_sg[i].astype(BF16), "wo": w_out[i].astype(BF16)},
            "sgw_chunk": sgw_chunk, "sgb_chunk": sgb_chunk,
            "sgw_first": jnp.repeat(sg_w[i][:, 0, 0], SG_DIM).reshape(1, D_SG),
            "sgb_first": jnp.repeat(sg_b[i][:, 0], SG_DIM).reshape(1, D_SG),
            "routed": i % 2 == 1, "final": i == depth - 1,
        }
        if i % 2 == 0:
            gu = ffn_w_gu[i // 2][None]
            wd = ffn_w_down[i // 2].reshape(-1, D_FF_CHUNK, D_MODEL)
            lw["router_w"] = jnp.zeros((D_MODEL, LANES), F32)
            lw["router_b"] = jnp.zeros((1, LANES), F32)
        else:
            gu, wd = moe_w_gu[i // 2], moe_w_down[i // 2]
            lw["router_w"] = jnp.pad(router_w[i // 2], ((0, 0), (0, LANES - N_EXPERTS)))
            lw["router_b"] = jnp.pad(router_b[i // 2], (0, LANES - N_EXPERTS)).reshape(1, LANES)
        lw["f32"].update(gu=gu, wd=wd)
        lw["bf16"].update(gu=gu.astype(BF16), wd=wd.astype(BF16))
        layers.append(lw)
    return layers


def _prompt_layer(x, mod, lw, tm):
    sh1, sc1, g1, sh2, sc2, g2 = mod
    w = lw["bf16"]
    kvt, u, v, ga, gb, kaug, qt, vt, gt = _in_proj(x, lw["norm_mix"], sc1, sh1, w["wn"], w["wt"],
                                                    lw["sg_norm_g"], lw["sg_norm_b"], tm, False, False)
    onsa = _nsa_prompt(kvt, kaug, qt, vt, gt, lw["cmp_pe"], lw["cmp_w1"], lw["cmp_w2"])
    x = _mix_out(x, onsa, u, v, ga, gb, g1, w["wa"], w["wb"], w["wo"], lw["sgw_chunk"], lw["sgb_chunk"],
                 tm, True, False)
    x = _ffn(x, lw["norm_ffn"], sc2, sh2, g2, w["gu"], w["wd"], lw["router_w"], lw["router_b"],
             lw["norm_final"], lw["ffn_tm"], lw["routed"], lw["final"], False)
    return x, kvt


def _sample_layer(x, mod, lw, cache_t, win_t, layer, page_table):
    sh1, sc1, g1, sh2, sc2, g2 = mod
    tm = x.shape[0]
    w = lw["f32"]
    kvt, u, v, ga, gb, _, qt, _, gt, kv = _in_proj(x, lw["norm_mix"], sc1, sh1, w["wn"], w["wt"],
                                                   lw["sg_norm_g"], lw["sg_norm_b"], tm, True, True)
    onsa = _nsa_sample(cache_t, win_t, layer, page_table, kv, qt, gt, lw["cmp_pe"], lw["cmp_w1"], lw["cmp_w2"])
    x = _mix_out(x, onsa, u, v, ga, gb, g1, w["wa"], w["wb"], w["wo"], lw["sgw_first"], lw["sgb_first"],
                 tm, False, True)
    x = _ffn(x, lw["norm_ffn"], sc2, sh2, g2, w["gu"], w["wd"], lw["router_w"], lw["router_b"],
             lw["norm_final"], tm, lw["routed"], lw["final"], True)
    return x, kv, kvt, v


PROMPT_ROW_TILE = 512
PROMPT_FFN_ROW_TILE = 512


def kernel(x_prompt, x_sample, cache_kv, state_win, page_table, c_prompt, c_sample, norm_mix_g, norm_ffn_g,
           norm_final_g, w_ada, b_ada, w_in, cmp_pe, cmp_w1, cmp_w2, sg_norm_g, sg_norm_b, sg_w, sg_b,
           w_branch_nsa, w_branch_sg, w_out, ffn_w_gu, ffn_w_down, router_w, router_b, moe_w_gu, moe_w_down):
    batch, seq, _ = x_prompt.shape
    n_seq, dec_seq, _ = x_sample.shape
    depth = w_in.shape[0]
    assert batch == 1 and dec_seq == 1
    assert seq % PROMPT_FFN_ROW_TILE == 0 and seq // Q_BLOCK <= 256
    past_len = page_table.shape[1] * PAGE_SIZE
    assert past_len % CHUNK == 0 and past_len // LANES <= 256
    assert state_win.shape[2] == WINDOW and page_table.shape[1] % PAGES_PER_STEP == 0

    layers = _prep_weights(norm_mix_g, norm_ffn_g, norm_final_g, w_in, cmp_pe, cmp_w1, cmp_w2, sg_norm_g,
                           sg_norm_b, sg_w, sg_b, w_branch_nsa, w_branch_sg, w_out, ffn_w_gu, ffn_w_down,
                           router_w, router_b, moe_w_gu, moe_w_down)
    c_all = _pad_rows(jnp.concatenate([c_prompt, c_sample], axis=0), SUBLANES)
    mods = _ada(c_all, w_ada, b_ada)

    cache_t = cache_kv.transpose(0, 1, 3, 4, 5, 2)
    win_t = state_win.transpose(0, 1, 3, 4, 5, 2)

    xp = x_prompt[0]
    xs = x_sample[:, 0]
    kv_p, kv_s, win_p, win_new, sgv_s = [], [], [], [], []
    w_keep = min(WINDOW, seq)
    kv_rows = 4 * N_KV * HEAD_DIM
    for i in range(depth):
        lw = dict(layers[i])
        lw["ffn_tm"] = PROMPT_FFN_ROW_TILE
        mod_p = tuple(mods[i, 0:1, j * D_MODEL:(j + 1) * D_MODEL] for j in range(6))
        mod_s = tuple(mods[i, 1:1 + n_seq, j * D_MODEL:(j + 1) * D_MODEL] for j in range(6))
        xp, kvt_p = _prompt_layer(xp, mod_p, lw, PROMPT_ROW_TILE)
        xs, kvs, kvt_s, v_s = _sample_layer(xs, mod_s, lw, cache_t, win_t, i, page_table)
        kv_p.append(kvt_p[:kv_rows].reshape(4, N_KV, HEAD_DIM, seq))
        win_p.append(kvt_p[kv_rows:, seq - w_keep:].reshape(2, N_KV, HEAD_DIM, w_keep))
        kv_s.append(kvs[:, :kv_rows].reshape(n_seq, 1, 4, N_KV, HEAD_DIM))
        win_new.append(kvt_s[kv_rows:].reshape(2, N_KV, HEAD_DIM, n_seq).transpose(3, 0, 1, 2)[..., None])
        sgv_s.append(v_s.reshape(n_seq, 1, D_SG))
    kv_prompt = jnp.stack(kv_p).transpose(0, 4, 1, 2, 3)[:, None]
    win_prompt = jnp.stack(win_p).transpose(0, 4, 1, 2, 3)[:, None]
    win_sample = jnp.concatenate([win_t[..., 1:], jnp.stack(win_new)], axis=-1).transpose(0, 1, 5, 2, 3, 4)
    return (xp[None], xs[:, None], kv_prompt, jnp.stack(kv_s), win_prompt, win_sample, jnp.stack(sgv_s))
```

```python
import functools

import jax
import jax.numpy as jnp
from jax import lax
from jax.experimental import pallas as pl
from jax.experimental.pallas import tpu as pltpu

F32 = jnp.float32
BF16 = jnp.bfloat16
I32 = jnp.int32
HIGHEST = lax.Precision.HIGHEST

LANES = 128
SUBLANES = 8
VMEM_LIMIT_BYTES = 56 * 1024 * 1024

D_MODEL = 1024
N_HEADS = 8
HEAD_DIM = 64
N_KV = 2
GROUP = N_HEADS // N_KV
D_NSA = N_HEADS * HEAD_DIM
CMP_BLOCK = 32
CMP_HIDDEN = 256
SEL_BLOCK = 64
SEL_TOPK = 16
WINDOW = 512
Q_BLOCK = 128
N_SG = 8
SG_DIM = 64
D_SG = N_SG * SG_DIM
CHUNK = 128
N_EXPERTS = 8
D_FF_CHUNK = 1408
PAGE_SIZE = 128
EPS = 1e-6
NEG = -1e30
FORCED = 1e9
REMOVED = -3e38

KV_COLS = 6 * N_KV * HEAD_DIM
OFF_KV = D_NSA
OFF_G = OFF_KV + KV_COLS
OFF_U = OFF_G + 3 * N_HEADS
OFF_V = OFF_U + D_SG
OFF_GA = OFF_V + D_SG
OFF_GB = OFF_GA + D_MODEL
IN_COLS = OFF_GB + D_MODEL

WN_U = 0
WN_V = WN_U + D_SG
WN_GA = WN_V + D_SG
WN_GB = WN_GA + D_MODEL
WN_KAUG = WN_GB + D_MODEL
WN_KV = WN_KAUG + 4 * LANES
WN_COLS = WN_KV + KV_COLS
WT_Q = 0
WT_KV = D_NSA
WT_G = WT_KV + KV_COLS
WT_ROWS = WT_G + 32

POS_HI_LANE = HEAD_DIM
POS_LO_LANE = HEAD_DIM + 1
ONE_LANE = HEAD_DIM + 2


def _cparams(sem):
    return pltpu.CompilerParams(dimension_semantics=sem, vmem_limit_bytes=VMEM_LIMIT_BYTES)


def _bdot(a, b):
    return jnp.dot(a.astype(BF16), b.astype(BF16), preferred_element_type=F32)


def _dot_nt(a, b):
    return lax.dot_general(a.astype(BF16), b.astype(BF16), (((1,), (1,)), ((), ())),
                           preferred_element_type=F32)


def _mm(a, b, precise):
    if precise:
        return jnp.dot(a.astype(F32), b.astype(F32), precision=HIGHEST, preferred_element_type=F32)
    return _bdot(a, b)


def _mm_nt(a, b, precise):
    if precise:
        return lax.dot_general(a.astype(F32), b.astype(F32), (((1,), (1,)), ((), ())), precision=HIGHEST,
                               preferred_element_type=F32)
    return _dot_nt(a, b)


def _silu(x):
    return x * jax.nn.sigmoid(x)


def _pos_aug(pos, lane):
    hi = (pos >> 7).astype(F32)
    lo = (pos & 127).astype(F32)
    return jnp.where(lane == POS_HI_LANE, hi,
                     jnp.where(lane == POS_LO_LANE, lo,
                               jnp.where(lane == ONE_LANE, 1.0, 0.0)))


def _group_slopes(g, lane_head):
    out = jnp.zeros(lane_head.shape, F32)
    for gg in range(N_KV):
        for hh in range(GROUP):
            s = 2.0 ** (-8.0 * (gg * GROUP + hh + 1) / N_HEADS)
            out = jnp.where((lane_head == hh) & (g == gg), s, out)
    return out


def _qt_aug(qt_ref, g, q0_blocks):
    heads = [qt_ref[h * HEAD_DIM:(h + 1) * HEAD_DIM, :] for h in range(GROUP)]
    q = jnp.concatenate(heads, axis=1)
    n = GROUP * Q_BLOCK
    row = lax.broadcasted_iota(I32, (LANES - HEAD_DIM, n), 0)
    lane_head = lax.broadcasted_iota(I32, (LANES - HEAD_DIM, n), 1) // Q_BLOCK
    slope = _group_slopes(g, lane_head)
    q0f = (q0_blocks).astype(F32)
    aug = jnp.where(row == 0, slope * 128.0,
                    jnp.where(row == 1, slope,
                              jnp.where(row == 2, -(slope * 128.0) * q0f, 0.0)))
    return jnp.concatenate([q.astype(BF16), aug.astype(BF16)], axis=0)


def _ada_kernel(c_ref, w_ref, b_ref, o_ref):
    c = c_ref[...]
    o_ref[0] = jnp.dot(_silu(c), w_ref[0], precision=HIGHEST, preferred_element_type=F32) + b_ref[0]


def _ada(c_all, w_ada, b_ada):
    depth = w_ada.shape[0]
    rows = c_all.shape[0]
    tn = 1024
    n = w_ada.shape[2]
    return pl.pallas_call(
        _ada_kernel,
        grid=(depth, n // tn),
        in_specs=[pl.BlockSpec((rows, D_MODEL), lambda l, j: (0, 0)),
                  pl.BlockSpec((1, D_MODEL, tn), lambda l, j: (l, 0, j)),
                  pl.BlockSpec((1, 1, tn), lambda l, j: (l, 0, j))],
        out_specs=pl.BlockSpec((1, rows, tn), lambda l, j: (l, 0, j)),
        out_shape=jax.ShapeDtypeStruct((depth, rows, n), F32),
        compiler_params=_cparams(("arbitrary", "arbitrary")),
        name="ada",
    )(c_all, w_ada, b_ada.reshape(depth, 1, n))


def _in_proj_kernel(x_ref, g_ref, sc_ref, sh_ref, wn_ref, wt_ref, lng_ref, lnb_ref, *out_refs, tm, natural_kv,
                    precise):
    kvt_ref, u_ref, v_ref, ga_ref, gb_ref, kaug_ref, qt_ref, vt_ref, gt_ref, kv_ref = out_refs
    i = pl.program_id(0)
    x = x_ref[...]
    ms = jnp.mean(x * x, axis=-1, keepdims=True)
    h = x * lax.rsqrt(ms + EPS) * g_ref[...]
    h = h * (1.0 + sc_ref[...]) + sh_ref[...]
    hb = h if precise else h.astype(BF16)

    def seg(a, b):
        return _mm(hb, wn_ref[:, a:b], precise)

    u_ref[...] = jax.nn.gelu(seg(WN_U, WN_V)).astype(u_ref.dtype)
    v = jax.nn.gelu(seg(WN_V, WN_GA))
    mu = jnp.mean(v, axis=-1, keepdims=True)
    var = jnp.mean(jnp.square(v - mu), axis=-1, keepdims=True)
    v_ref[...] = (v - mu) * lax.rsqrt(var + EPS) * lng_ref[...] + lnb_ref[...]
    ga_ref[...] = jax.nn.sigmoid(seg(WN_GA, WN_GB)).astype(ga_ref.dtype)
    gb_ref[...] = jax.nn.sigmoid(seg(WN_GB, WN_KAUG)).astype(gb_ref.dtype)

    pos = lax.broadcasted_iota(I32, (tm, LANES), 0) + i * tm
    lane = lax.broadcasted_iota(I32, (tm, LANES), 1)
    aug = _pos_aug(pos, lane)
    for j in range(4):
        k = seg(WN_KAUG + j * LANES, WN_KAUG + (j + 1) * LANES)
        kaug_ref[j] = (k + aug).astype(BF16)
    if natural_kv == "all":
        kv_ref[...] = seg(WN_KV, WN_COLS)
    else:
        for j in range(2):
            kv_ref[j] = seg(WN_KV + j * LANES, WN_KV + (j + 1) * LANES)

    zt = _mm_nt(wt_ref[...], hb, precise)
    qt_ref[...] = (zt[WT_Q:WT_KV] * (HEAD_DIM ** -0.5)).astype(qt_ref.dtype)
    kvt = zt[WT_KV:WT_G]
    kvt_ref[...] = kvt
    two = N_KV * HEAD_DIM
    vt_ref[0:two, :] = kvt[3 * two:4 * two].astype(BF16)
    vt_ref[two:2 * two, :] = kvt[5 * two:6 * two].astype(BF16)
    gt_ref[...] = jax.nn.sigmoid(zt[WT_G:WT_ROWS])


def _in_proj(x, g, sc, sh, wn, wt, lng, lnb, tm, natural_kv, precise):
    m = x.shape[0]
    act = F32 if precise else BF16
    mod_rows = sc.shape[0]
    mod_block = (1, D_MODEL) if mod_rows == 1 else (tm, D_MODEL)
    mod_map = (lambda i: (0, 0)) if mod_rows == 1 else (lambda i: (i, 0))
    row = lambda i: (i, 0)
    col = lambda i: (0, i)
    const = lambda i: (0, 0)
    out_shape = [
        jax.ShapeDtypeStruct((KV_COLS, m), F32),
        jax.ShapeDtypeStruct((m, D_SG), act),
        jax.ShapeDtypeStruct((m, D_SG), F32),
        jax.ShapeDtypeStruct((m, D_MODEL), act),
        jax.ShapeDtypeStruct((m, D_MODEL), act),
        jax.ShapeDtypeStruct((4, m, LANES), BF16),
        jax.ShapeDtypeStruct((D_NSA, m), act),
        jax.ShapeDtypeStruct((4 * HEAD_DIM, m), BF16),
        jax.ShapeDtypeStruct((32, m), F32),
    ]
    out_specs = [
        pl.BlockSpec((KV_COLS, tm), col),
        pl.BlockSpec((tm, D_SG), row),
        pl.BlockSpec((tm, D_SG), row),
        pl.BlockSpec((tm, D_MODEL), row),
        pl.BlockSpec((tm, D_MODEL), row),
        pl.BlockSpec((4, tm, LANES), lambda i: (0, i, 0)),
        pl.BlockSpec((D_NSA, tm), col),
        pl.BlockSpec((4 * HEAD_DIM, tm), col),
        pl.BlockSpec((32, tm), col),
    ]
    if natural_kv == "all":
        out_shape.append(jax.ShapeDtypeStruct((m, KV_COLS), F32))
        out_specs.append(pl.BlockSpec((tm, KV_COLS), row))
    else:
        out_shape.append(jax.ShapeDtypeStruct((2, m, LANES), F32))
        out_specs.append(pl.BlockSpec((2, tm, LANES), lambda i: (0, i, 0)))
    return pl.pallas_call(
        functools.partial(_in_proj_kernel, tm=tm, natural_kv=natural_kv, precise=precise),
        grid=(m // tm,),
        in_specs=[pl.BlockSpec((tm, D_MODEL), row),
                  pl.BlockSpec((1, D_MODEL), const),
                  pl.BlockSpec(mod_block, mod_map),
                  pl.BlockSpec(mod_block, mod_map),
                  pl.BlockSpec((D_MODEL, WN_COLS), const),
                  pl.BlockSpec((WT_ROWS, D_MODEL), const),
                  pl.BlockSpec((1, D_SG), const),
                  pl.BlockSpec((1, D_SG), const)],
        out_specs=tuple(out_specs),
        out_shape=tuple(out_shape),
        compiler_params=_cparams(("arbitrary",)),
        name="in_proj",
    )(x, g, sc, sh, wn, wt, lng, lnb)


def _pack_w_in(w):
    kv = w[:, OFF_KV:OFF_G]

    def kvcol(j, g):
        return kv[:, (j * N_KV + g) * HEAD_DIM:(j * N_KV + g + 1) * HEAD_DIM]

    zpad = jnp.zeros((D_MODEL, LANES - HEAD_DIM), w.dtype)
    kaug = [jnp.concatenate([kvcol(j, g), zpad], axis=1) for j in (2, 4) for g in range(N_KV)]
    wn = jnp.concatenate([w[:, OFF_U:OFF_V], w[:, OFF_V:OFF_GA], w[:, OFF_GA:OFF_GB],
                          w[:, OFF_GB:IN_COLS]] + kaug + [kv], axis=1)
    gpad = jnp.zeros((D_MODEL, 32 - 3 * N_HEADS), w.dtype)
    wt = jnp.concatenate([w[:, :OFF_KV], kv, w[:, OFF_G:OFF_U], gpad], axis=1).T
    return wn, wt


def _cmp_block_end(nc):
    r = lax.broadcasted_iota(I32, (nc, LANES), 0)
    half = nc // 2
    c = 2 * (r % half) + r // half
    return c * CMP_BLOCK + (CMP_BLOCK - 1)


def _compress_weights(pe, w1, w2):
    kd = CMP_BLOCK * LANES
    pe_flat = jnp.concatenate([pe, pe], axis=-1).reshape(2, 1, kd)
    w1r = w1.reshape(2, CMP_BLOCK, HEAD_DIM, CMP_HIDDEN)
    w1p = jnp.einsum("gh,jtdn->jtgdhn", jnp.eye(N_KV, dtype=w1.dtype), w1r)
    w1p = w1p.reshape(2, kd, N_KV * CMP_HIDDEN).astype(BF16)
    w2p = jnp.pad(w2, ((0, 0), (0, 0), (0, LANES - HEAD_DIM))).astype(BF16)
    return pe_flat, w1p, w2p


def _compress_kernel(x_ref, pe_ref, w1_ref, w2_ref, w2vt_ref, kc_ref, vct_ref, *, half):
    parity = pl.program_id(0)
    row = lax.broadcasted_iota(I32, (half, LANES), 0)
    lane = lax.broadcasted_iota(I32, (half, LANES), 1)
    aug = _pos_aug((2 * row + parity) * CMP_BLOCK + (CMP_BLOCK - 1), lane)
    hid_k = _bdot(x_ref[0] + pe_ref[0], w1_ref[0])
    hid_v = _bdot(x_ref[1] + pe_ref[1], w1_ref[1])
    for g in range(N_KV):
        cols = slice(g * CMP_HIDDEN, (g + 1) * CMP_HIDDEN)
        kc_ref[g] = (_bdot(_silu(hid_k[:, cols]), w2_ref[0]) + aug).astype(BF16)
        vct_ref[g] = _dot_nt(w2vt_ref[...], _silu(hid_v[:, cols])).astype(BF16)


def _compress_prompt(kvc, pe, w1, w2):
    m = kvc.shape[1]
    nc = m // CMP_BLOCK
    half = nc // 2
    kd = CMP_BLOCK * LANES
    x = kvc.reshape(2, half, 2 * kd)
    pe_flat, w1p, w2p = _compress_weights(pe, w1, w2)
    w2vt = w2[1].T.astype(BF16)
    return pl.pallas_call(
        functools.partial(_compress_kernel, half=half),
        grid=(2,),
        in_specs=[pl.BlockSpec((2, half, kd), lambda p: (0, 0, p)),
                  pl.BlockSpec((2, 1, kd), lambda p: (0, 0, 0)),
                  pl.BlockSpec((2, kd, N_KV * CMP_HIDDEN), lambda p: (0, 0, 0)),
                  pl.BlockSpec((2, CMP_HIDDEN, LANES), lambda p: (0, 0, 0)),
                  pl.BlockSpec((HEAD_DIM, CMP_HIDDEN), lambda p: (0, 0))],
        out_specs=(pl.BlockSpec((N_KV, half, LANES), lambda p: (0, p, 0)),
                   pl.BlockSpec((N_KV, HEAD_DIM, half), lambda p: (0, 0, p))),
        out_shape=(jax.ShapeDtypeStruct((N_KV, nc, LANES), BF16),
                   jax.ShapeDtypeStruct((N_KV, HEAD_DIM, nc), BF16)),
        compiler_params=_cparams(("arbitrary",)),
        name="compress_prompt",
    )(x, pe_flat, w1p, w2p, w2vt)


def _topk_mask_rows(imp, k):
    rows = lax.broadcasted_iota(I32, imp.shape, 0)
    big = jnp.int32(2 ** 30)

    def body(_, v):
        m = jnp.max(v, axis=0, keepdims=True)
        first = jnp.min(jnp.where(v == m, rows, big), axis=0, keepdims=True)
        return jnp.where(rows == first, REMOVED, v)

    return jnp.where(lax.fori_loop(0, k, body, imp) < 0.5 * REMOVED, 1.0, 0.0)


def _visited_blocks(sel, limit):
    ns = sel.shape[0]
    ones_q = jnp.ones((SUBLANES, sel.shape[1]), BF16)
    blk = lax.broadcasted_iota(I32, (SUBLANES, ns), 1)
    flags = ((_dot_nt(ones_q, sel) > 0.5) & (blk < limit)).astype(F32)
    r = lax.broadcasted_iota(I32, (ns, ns), 0)
    c = lax.broadcasted_iota(I32, (ns, ns), 1)
    upper = jnp.where(r <= c, 1.0, 0.0)
    prefix = _bdot(flags, upper)
    before = jnp.where(prefix[0:1, :] <= r.astype(F32), 1.0, 0.0)
    ids = _dot_nt(jnp.ones((SUBLANES, ns), BF16), before)
    ids = jnp.minimum(ids, ns - 1.0).astype(I32)
    total = jnp.broadcast_to(prefix[:, ns - 1:ns], (SUBLANES, ns))
    return ids, total


def _select_kernel(qt_ref, kc_ref, vct_ref, mask_ref, ids_ref, cnt_ref, ocmp_ref, *, nc):
    i = pl.program_id(0)
    ns = nc // 2
    n = GROUP * Q_BLOCK
    q0 = i * Q_BLOCK
    cend = _cmp_block_end(nc)[:, :1]
    qpos = q0 + lax.broadcasted_iota(I32, (1, n), 1) % Q_BLOCK
    valid = cend <= qpos
    blk = lax.broadcasted_iota(I32, (ns, Q_BLOCK), 0)
    cur = (q0 + lax.broadcasted_iota(I32, (ns, Q_BLOCK), 1)) // SEL_BLOCK
    forced = (blk == 0) | (blk == cur) | (blk == cur - 1)
    imps = []
    for g in range(N_KV):
        qa = _qt_aug(qt_ref.at[g * GROUP * HEAD_DIM:(g + 1) * GROUP * HEAD_DIM], g, i)
        s = jnp.dot(kc_ref[g], qa, preferred_element_type=F32)
        s = jnp.where(valid, s, NEG)
        m = jnp.max(s, axis=0, keepdims=True)
        p = jnp.where(valid, jnp.exp(s - m), 0.0)
        p = p / jnp.maximum(jnp.sum(p, axis=0, keepdims=True), 1e-30)
        ocmp_ref[0, g] = jnp.dot(vct_ref[g], p.astype(BF16), preferred_element_type=F32)
        ph = p[:, 0:Q_BLOCK]
        for h in range(1, GROUP):
            ph = ph + p[:, h * Q_BLOCK:(h + 1) * Q_BLOCK]
        imp = ph[:ns] + ph[ns:]
        imp = jnp.where(forced, FORCED, imp)
        imps.append(jnp.where(blk <= cur, imp, NEG))
    for g in range(N_KV):
        sel = jnp.where(blk <= cur, _topk_mask_rows(imps[g], min(SEL_TOPK, ns)), 0.0)
        mask_ref[0, g] = sel
        ids_ref[0, g], cnt_ref[0, g] = _visited_blocks(sel, 2 * i)


def _select(qt, kc, vct):
    m = qt.shape[1]
    nqb = m // Q_BLOCK
    nc = m // CMP_BLOCK
    ns = nc // 2
    n = GROUP * Q_BLOCK
    return pl.pallas_call(
        functools.partial(_select_kernel, nc=nc),
        grid=(nqb,),
        in_specs=[pl.BlockSpec((D_NSA, Q_BLOCK), lambda i: (0, i)),
                  pl.BlockSpec((N_KV, nc, LANES), lambda i: (0, 0, 0)),
                  pl.BlockSpec((N_KV, HEAD_DIM, nc), lambda i: (0, 0, 0))],
        out_specs=(pl.BlockSpec((1, N_KV, ns, Q_BLOCK), lambda i: (i, 0, 0, 0)),
                   pl.BlockSpec((1, N_KV, SUBLANES, ns), lambda i: (i, 0, 0, 0)),
                   pl.BlockSpec((1, N_KV, SUBLANES, ns), lambda i: (i, 0, 0, 0)),
                   pl.BlockSpec((1, N_KV, HEAD_DIM, n), lambda i: (i, 0, 0, 0))),
        out_shape=(jax.ShapeDtypeStruct((nqb, N_KV, ns, Q_BLOCK), F32),
                   jax.ShapeDtypeStruct((nqb, N_KV, SUBLANES, ns), I32),
                   jax.ShapeDtypeStruct((nqb, N_KV, SUBLANES, ns), F32),
                   jax.ShapeDtypeStruct((nqb, N_KV, HEAD_DIM, n), F32)),
        compiler_params=_cparams(("arbitrary",)),
        name="nsa_select",
    )(qt, kc, vct)


SEL_PER_STEP = 8
WIN_TILES_PER_STEP = 4


def _softmax_step(state, s, valid=None):
    m, l, _ = state
    if valid is not None:
        s = jnp.where(valid, s, NEG)
    m_new = jnp.maximum(m, jnp.max(s, axis=0, keepdims=True))
    alpha = jnp.exp(m - m_new)
    p = jnp.exp(s - m_new)
    if valid is not None:
        p = jnp.where(valid, p, 0.0)
    return m_new, alpha * l + jnp.sum(p, axis=0, keepdims=True), alpha, p.astype(BF16)


def _pv_lane_tiles(vt_tiles):
    vt = jnp.concatenate(vt_tiles, axis=1)
    return lambda pb: jnp.dot(vt, pb, preferred_element_type=F32)


def _pv_row_tiles(v_tiles):
    v = jnp.concatenate(v_tiles, axis=0)
    return lambda pb: lax.dot_general(v, pb, (((0,), (0,)), ((), ())), preferred_element_type=F32)


def _attend_kernel(cnt_ref, ids_ref, qt_ref, ksel_ref, vsel_ref, kwin_ref, vwin_ref, mask_ref,
                   ocmp_ref, gt_ref, o_ref, *, nqb, ns):
    g = pl.program_id(0)
    i = pl.program_id(1)
    q0 = i * Q_BLOCK
    qa = _qt_aug(qt_ref, g, i)
    qa_heads = [qa[:, h * Q_BLOCK:(h + 1) * Q_BLOCK] for h in range(GROUP)]
    empty = (jnp.full((1, Q_BLOCK), NEG, F32), jnp.zeros((1, Q_BLOCK), F32), jnp.zeros((HEAD_DIM, Q_BLOCK), F32))
    init = (empty,) * GROUP

    def heads_update(states, keys, pv_fn, bias=None, valid=None):
        scores = [jnp.dot(keys, qa_heads[h], preferred_element_type=F32) for h in range(GROUP)]
        if bias is not None:
            scores = [s + bias for s in scores]
        parts = [_softmax_step(states[h], scores[h], valid) for h in range(GROUP)]
        pvs = [pv_fn(p) for _, _, _, p in parts]
        return tuple((m, l, alpha * states[h][2] + pvs[h]) for h, (m, l, alpha, _) in enumerate(parts))

    slot = g * nqb + i
    count = cnt_ref[slot]
    key_off = lax.broadcasted_iota(I32, (Q_BLOCK, 1), 0)
    query_off = lax.broadcasted_iota(I32, (1, Q_BLOCK), 1)
    causal = key_off <= query_off

    own = mask_ref[0, 0, pl.ds(pl.multiple_of(2 * i, 2), 2), :]
    chosen = jnp.concatenate([jnp.broadcast_to(own[0:1] > 0.5, (SEL_BLOCK, Q_BLOCK)),
                              jnp.broadcast_to(own[1:2] > 0.5, (SEL_BLOCK, Q_BLOCK))], axis=0)
    k_own = ksel_ref[0, pl.ds(pl.multiple_of(q0, Q_BLOCK), Q_BLOCK), :]
    first = heads_update(init, k_own, _pv_row_tiles([vsel_ref[0, 2 * i], vsel_ref[0, 2 * i + 1]]),
                         valid=chosen & causal)

    def sel_body(t, states):
        ks, vts, biases = [], [], []
        for u in range(SEL_PER_STEP):
            e = t * SEL_PER_STEP + u
            b = ids_ref[slot * ns + e]
            ks.append(ksel_ref[0, pl.ds(pl.multiple_of(b * SEL_BLOCK, SEL_BLOCK), SEL_BLOCK), :])
            row = mask_ref[0, 0, pl.ds(b, 1), :]
            live = (row > 0.5) & (e < count)
            biases.append(jnp.broadcast_to(jnp.where(live, 0.0, NEG), (SEL_BLOCK, Q_BLOCK)))
            vts.append(vsel_ref[0, b])
        return heads_update(states, jnp.concatenate(ks, axis=0), _pv_row_tiles(vts),
                            bias=jnp.concatenate(biases, axis=0))

    sel_states = lax.fori_loop(0, (count + (SEL_PER_STEP - 1)) // SEL_PER_STEP, sel_body, first)

    k_own = kwin_ref[0, pl.ds(pl.multiple_of(q0, Q_BLOCK), Q_BLOCK), :]
    win_states = heads_update(init, k_own, _pv_lane_tiles([vwin_ref[0, i]]), valid=causal)
    n_old = WINDOW // Q_BLOCK
    oldest_in_window = key_off > query_off
    for j0 in range(0, n_old, WIN_TILES_PER_STEP):
        ks, vts, biases = [], [], []
        for j in range(j0, min(j0 + WIN_TILES_PER_STEP, n_old)):
            tile = i - n_old + j
            held = jnp.maximum(tile, 0)
            ks.append(kwin_ref[0, pl.ds(pl.multiple_of(held * Q_BLOCK, Q_BLOCK), Q_BLOCK), :])
            inside = tile >= 0
            live = (oldest_in_window & inside) if j == 0 else inside
            biases.append(jnp.broadcast_to(jnp.where(live, 0.0, NEG), (Q_BLOCK, Q_BLOCK)))
            vts.append(vwin_ref[0, held])
        win_states = heads_update(win_states, jnp.concatenate(ks, axis=0), _pv_lane_tiles(vts),
                                  bias=jnp.concatenate(biases, axis=0))

    def head_out(h):
        def gate(branch):
            return gt_ref[pl.ds(branch * N_HEADS + g * GROUP + h, 1), :]
        _, l_s, acc_s = sel_states[h]
        _, l_w, acc_w = win_states[h]
        return (ocmp_ref[0, 0, :, h * Q_BLOCK:(h + 1) * Q_BLOCK] * gate(0)
                + acc_s / jnp.maximum(l_s, 1e-30) * gate(1) + acc_w / jnp.maximum(l_w, 1e-30) * gate(2))

    for hp in range(GROUP // 2):
        pair = jnp.concatenate([head_out(2 * hp), head_out(2 * hp + 1)], axis=0)
        o_ref[:, hp * LANES:(hp + 1) * LANES] = pair.T.astype(o_ref.dtype)


def _attend(cnt, ids, qt, kaug, vt_blocks_sel, vt_blocks_win, mask, ocmp, gt):
    m = qt.shape[1]
    nqb = m // Q_BLOCK
    ns = m // SEL_BLOCK
    n = GROUP * Q_BLOCK
    gh = GROUP * HEAD_DIM
    grid_spec = pltpu.PrefetchScalarGridSpec(
        num_scalar_prefetch=2,
        grid=(N_KV, nqb),
        in_specs=[pl.BlockSpec((gh, Q_BLOCK), lambda g, i, c, d: (g, i)),
                  pl.BlockSpec((1, m, LANES), lambda g, i, c, d: (g, 0, 0)),
                  pl.BlockSpec((1, ns, SEL_BLOCK, HEAD_DIM), lambda g, i, c, d: (g, 0, 0, 0)),
                  pl.BlockSpec((1, m, LANES), lambda g, i, c, d: (N_KV + g, 0, 0)),
                  pl.BlockSpec((1, nqb, HEAD_DIM, Q_BLOCK), lambda g, i, c, d: (g, 0, 0, 0)),
                  pl.BlockSpec((1, 1, ns, Q_BLOCK), lambda g, i, c, d: (i, g, 0, 0)),
                  pl.BlockSpec((1, 1, HEAD_DIM, n), lambda g, i, c, d: (i, g, 0, 0)),
                  pl.BlockSpec((32, Q_BLOCK), lambda g, i, c, d: (0, i))],
        out_specs=pl.BlockSpec((Q_BLOCK, gh), lambda g, i, c, d: (i, g)),
    )
    return pl.pallas_call(
        functools.partial(_attend_kernel, nqb=nqb, ns=ns),
        grid_spec=grid_spec,
        out_shape=jax.ShapeDtypeStruct((m, D_NSA), BF16),
        compiler_params=_cparams(("arbitrary", "arbitrary")),
        name="nsa_attend",
    )(cnt, ids, qt, kaug, vt_blocks_sel, kaug, vt_blocks_win, mask, ocmp, gt)


def _nsa_prompt(kvc, kaug, qt, vt, gt, pe, w1, w2):
    m = kvc.shape[1]
    nqb = m // Q_BLOCK
    ns = m // SEL_BLOCK
    kc, vct = _compress_prompt(kvc, pe, w1, w2)
    mask, ids8, cnt8, ocmp = _select(qt, kc, vct)
    ids = ids8[:, :, 0, :].transpose(1, 0, 2).reshape(-1)
    cnt = cnt8[:, :, 0, 0].T.astype(I32).reshape(-1)
    vt4 = vt.reshape(4, HEAD_DIM, m)
    vsel = vt4[:N_KV].reshape(N_KV, HEAD_DIM, ns, SEL_BLOCK).transpose(0, 2, 3, 1)
    vwin = vt4[N_KV:].reshape(N_KV, HEAD_DIM, nqb, Q_BLOCK).transpose(0, 2, 1, 3)
    return _attend(cnt, ids, qt, kaug, vsel, vwin, mask, ocmp, gt)


def _mix_kernel(x_ref, onsa_ref, u_ref, v_ref, ga_ref, gb_ref, g1_ref, wa_ref, wb_ref, wo_ref,
                sgw_ref, sgb_ref, o_ref, osg_ref, *, tm, chunked, precise):
    if chunked:
        lane = lax.broadcasted_iota(I32, (CHUNK, LANES), 1)
        for c in range(tm // CHUNK):
            rows = slice(c * CHUNK, (c + 1) * CHUNK)
            for pr in range(N_SG // 2):
                cols = slice(pr * LANES, (pr + 1) * LANES)
                vp = v_ref[rows, cols].astype(BF16)
                a = jnp.dot(sgw_ref[2 * pr], vp, preferred_element_type=F32)
                b = jnp.dot(sgw_ref[2 * pr + 1], vp, preferred_element_type=F32)
                mix = jnp.where(lane < SG_DIM, a, b) + sgb_ref[:, cols]
                osg_ref[rows, cols] = (u_ref[rows, cols].astype(F32) * mix).astype(BF16)
    else:
        mix = v_ref[...] * sgw_ref[...] + sgb_ref[...]
        osg_ref[...] = (u_ref[...].astype(F32) * mix).astype(osg_ref.dtype)
    a = _mm(onsa_ref[...], wa_ref[...], precise)
    b = _mm(osg_ref[...], wb_ref[...], precise)
    merged = ga_ref[...].astype(F32) * a + gb_ref[...].astype(F32) * b
    y = _mm(merged, wo_ref[...], precise)
    o_ref[...] = x_ref[...] + g1_ref[...] * y


def _mix_out(x, onsa, u, v, ga, gb, g1, wa, wb, wo, sgw, sgb, tm, chunked, precise):
    m = x.shape[0]
    mod_rows = g1.shape[0]
    mod_block = (1, D_MODEL) if mod_rows == 1 else (tm, D_MODEL)
    mod_map = (lambda i: (0, 0)) if mod_rows == 1 else (lambda i: (i, 0))
    row = lambda i: (i, 0)
    const2 = lambda i: (0, 0)
    if chunked:
        sg_specs = [pl.BlockSpec((N_SG, CHUNK, CHUNK), lambda i: (0, 0, 0)),
                    pl.BlockSpec((CHUNK, D_SG), const2)]
    else:
        sg_specs = [pl.BlockSpec((1, D_SG), const2), pl.BlockSpec((1, D_SG), const2)]
    return pl.pallas_call(
        functools.partial(_mix_kernel, tm=tm, chunked=chunked, precise=precise),
        grid=(m // tm,),
        in_specs=[pl.BlockSpec((tm, D_MODEL), row),
                  pl.BlockSpec((tm, D_NSA), row),
                  pl.BlockSpec((tm, D_SG), row),
                  pl.BlockSpec((tm, D_SG), row),
                  pl.BlockSpec((tm, D_MODEL), row),
                  pl.BlockSpec((tm, D_MODEL), row),
                  pl.BlockSpec(mod_block, mod_map),
                  pl.BlockSpec((D_NSA, D_MODEL), const2),
                  pl.BlockSpec((D_SG, D_MODEL), const2),
                  pl.BlockSpec((D_MODEL, D_MODEL), const2)] + sg_specs,
        out_specs=pl.BlockSpec((tm, D_MODEL), row),
        out_shape=jax.ShapeDtypeStruct((m, D_MODEL), F32),
        scratch_shapes=[pltpu.VMEM((tm, D_SG), F32 if precise else BF16)],
        compiler_params=_cparams(("arbitrary",)),
        name="mix_out",
    )(x, onsa, u, v, ga, gb, g1, wa, wb, wo, sgw, sgb)


def _top2_combine(logits):
    lane = lax.broadcasted_iota(I32, logits.shape, 1)
    big = jnp.int32(2 ** 30)
    z = jnp.where(lane < N_EXPERTS, logits, -jnp.inf)
    t1 = jnp.max(z, axis=-1, keepdims=True)
    i1 = jnp.min(jnp.where(z == t1, lane, big), axis=-1, keepdims=True)
    z2 = jnp.where(lane == i1, -jnp.inf, z)
    t2 = jnp.max(z2, axis=-1, keepdims=True)
    i2 = jnp.min(jnp.where(z2 == t2, lane, big), axis=-1, keepdims=True)
    e = jnp.exp(t2 - t1)
    den = 1.0 + e
    return jnp.where(lane == i1, 1.0 / den, 0.0) + jnp.where(lane == i2, e / den, 0.0)


def _ffn_kernel(x_ref, g_ref, sc_ref, sh_ref, g2_ref, wa_ref, wb_ref, wd_ref, rw_ref, rb_ref, gf_ref,
                o_ref, h_ref, acc_ref, comb_ref, *, routed, final_norm, n_chunks, precise):
    e = pl.program_id(1)

    @pl.when(e == 0)
    def _():
        x = x_ref[...]
        ms = jnp.mean(x * x, axis=-1, keepdims=True)
        h = x * lax.rsqrt(ms + EPS) * g_ref[...]
        h = h * (1.0 + sc_ref[...]) + sh_ref[...]
        h_ref[...] = h.astype(h_ref.dtype)
        acc_ref[...] = jnp.zeros_like(acc_ref)
        if routed:
            logits = jnp.dot(h, rw_ref[...], precision=HIGHEST, preferred_element_type=F32) + rb_ref[...]
            comb = _top2_combine(logits)
            for ee in range(N_EXPERTS):
                comb_ref[ee] = jnp.broadcast_to(comb[:, ee:ee + 1], comb.shape)

    hb = h_ref[...]
    a = _mm(hb, wa_ref[0], precise)
    b = _mm(hb, wb_ref[0], precise)
    y = _mm(_silu(a) * b, wd_ref[0], precise)
    if routed:
        w = comb_ref[e]
        for c in range(D_MODEL // LANES):
            cols = slice(c * LANES, (c + 1) * LANES)
            acc_ref[:, cols] += w * y[:, cols]
    else:
        acc_ref[...] += y

    @pl.when(e == n_chunks - 1)
    def _():
        out = x_ref[...] + g2_ref[...] * acc_ref[...]
        if final_norm:
            ms = jnp.mean(out * out, axis=-1, keepdims=True)
            out = out * lax.rsqrt(ms + EPS) * gf_ref[...]
        o_ref[...] = out


def _ffn(x, g, sc, sh, g2, gu, wd, rw, rb, gf, tm, routed, final_norm, precise):
    m = x.shape[0]
    n_chunks = wd.shape[0]
    if routed:
        a_map, b_map = (lambda i, e: (e, 0, 0)), (lambda i, e: (e, 0, 1))
    else:
        a_map, b_map = (lambda i, e: (0, 0, e)), (lambda i, e: (0, 0, n_chunks + e))
    mod_rows = sc.shape[0]
    mod_block = (1, D_MODEL) if mod_rows == 1 else (tm, D_MODEL)
    mod_map = (lambda i, e: (0, 0)) if mod_rows == 1 else (lambda i, e: (i, 0))
    row = lambda i, e: (i, 0)
    const2 = lambda i, e: (0, 0)
    return pl.pallas_call(
        functools.partial(_ffn_kernel, routed=routed, final_norm=final_norm, n_chunks=n_chunks,
                          precise=precise),
        grid=(m // tm, n_chunks),
        in_specs=[pl.BlockSpec((tm, D_MODEL), row),
                  pl.BlockSpec((1, D_MODEL), const2),
                  pl.BlockSpec(mod_block, mod_map),
                  pl.BlockSpec(mod_block, mod_map),
                  pl.BlockSpec(mod_block, mod_map),
                  pl.BlockSpec((1, D_MODEL, D_FF_CHUNK), a_map),
                  pl.BlockSpec((1, D_MODEL, D_FF_CHUNK), b_map),
                  pl.BlockSpec((1, D_FF_CHUNK, D_MODEL), lambda i, e: (e, 0, 0)),
                  pl.BlockSpec((D_MODEL, LANES), const2),
                  pl.BlockSpec((1, LANES), const2),
                  pl.BlockSpec((1, D_MODEL), const2)],
        out_specs=pl.BlockSpec((tm, D_MODEL), row),
        out_shape=jax.ShapeDtypeStruct((m, D_MODEL), F32),
        scratch_shapes=[pltpu.VMEM((tm, D_MODEL), F32 if precise else BF16),
                        pltpu.VMEM((tm, D_MODEL), F32),
                        pltpu.VMEM((N_EXPERTS, tm, LANES), F32)],
        compiler_params=_cparams(("arbitrary", "arbitrary")),
        name="ffn",
    )(x, g, sc, sh, g2, gu, gu, wd, rw, rb, gf)


PAGES_PER_STEP = 32
BLOCKS_PER_PAGE = PAGE_SIZE // CMP_BLOCK


def _compress_sample_kernel(pt_ref, cache_ref, pe_ref, w1_ref, w2_ref, kc_ref, vc_ref, pages_ref, sem_ref,
                            rows_ref, x_ref, *, layer, n_steps, total_steps):
    s = pl.program_id(1)
    step = pl.program_id(0) * n_steps + s
    slot = step % 2
    nb = PAGES_PER_STEP * BLOCKS_PER_PAGE
    half = nb // 2

    def page_copy(at_step, at_slot, p):
        page = pt_ref[at_step * PAGES_PER_STEP + p]
        return pltpu.make_async_copy(cache_ref.at[layer, page, pl.ds(0, 2)], pages_ref.at[at_slot, p],
                                     sem_ref.at[at_slot])

    def start_pages(at_step, at_slot):
        for p in range(PAGES_PER_STEP):
            page_copy(at_step, at_slot, p).start(priority=p % 2)

    @pl.when(step == 0)
    def _():
        start_pages(step, slot)

    @pl.when(step + 1 < total_steps)
    def _():
        start_pages(step + 1, 1 - slot)

    for p in range(PAGES_PER_STEP):
        page_copy(step, slot, p).wait()

    for j in range(2):
        for p in range(PAGES_PER_STEP):
            tile = pages_ref[slot, p, j].reshape(N_KV * HEAD_DIM, PAGE_SIZE)
            rows_ref[j, p * PAGE_SIZE:(p + 1) * PAGE_SIZE, :] = tile.T
    for t in range(CMP_BLOCK):
        for j in range(2):
            even = rows_ref[j, pl.ds(t, half, stride=2 * CMP_BLOCK), :]
            odd = rows_ref[j, pl.ds(CMP_BLOCK + t, half, stride=2 * CMP_BLOCK), :]
            xj = jnp.concatenate([even, odd], axis=0) + pe_ref[t, j]
            x_ref[j, :, t * LANES:(t + 1) * LANES] = xj.astype(BF16)
    row = lax.broadcasted_iota(I32, (nb, LANES), 0)
    lane = lax.broadcasted_iota(I32, (nb, LANES), 1)
    c = s * nb + 2 * (row % half) + row // half
    aug = _pos_aug(c * CMP_BLOCK + (CMP_BLOCK - 1), lane)
    hid_k = jnp.dot(x_ref[0], w1_ref[0], preferred_element_type=F32)
    hid_v = jnp.dot(x_ref[1], w1_ref[1], preferred_element_type=F32)
    for g in range(N_KV):
        cols = slice(g * CMP_HIDDEN, (g + 1) * CMP_HIDDEN)
        hk = _silu(hid_k[:, cols]).astype(BF16)
        kc = (jnp.dot(hk, w2_ref[0], preferred_element_type=F32) + aug).astype(BF16)
        kc_ref[0, g, 0] = kc[:half]
        kc_ref[0, g, 1] = kc[half:]
        hv = _silu(hid_v[:, cols]).astype(BF16)
        vc = jnp.dot(hv, w2_ref[1], preferred_element_type=F32).astype(BF16)
        vc_ref[0, g, 0] = vc[:half]
        vc_ref[0, g, 1] = vc[half:]


def _compress_sample(cache_t, layer, page_table, pe, w1, w2):
    n_seq, n_pages = page_table.shape
    n_steps = n_pages // PAGES_PER_STEP
    nb = PAGES_PER_STEP * BLOCKS_PER_PAGE
    half = nb // 2
    nc_half = n_pages * BLOCKS_PER_PAGE // 2
    pe_flat, w1p, w2p = _compress_weights(pe, w1, w2)
    pe4 = pe_flat.reshape(2, CMP_BLOCK, LANES).transpose(1, 0, 2).reshape(CMP_BLOCK, 2, 1, LANES)

    grid_spec = pltpu.PrefetchScalarGridSpec(
        num_scalar_prefetch=1,
        grid=(n_seq, n_steps),
        in_specs=[
            pl.BlockSpec(memory_space=pl.ANY),
            pl.BlockSpec((CMP_BLOCK, 2, 1, LANES), lambda b, s, pt: (0, 0, 0, 0)),
            pl.BlockSpec((2, CMP_BLOCK * LANES, N_KV * CMP_HIDDEN), lambda b, s, pt: (0, 0, 0)),
            pl.BlockSpec((2, CMP_HIDDEN, LANES), lambda b, s, pt: (0, 0, 0))],
        out_specs=(pl.BlockSpec((1, N_KV, 2, half, LANES), lambda b, s, pt: (b, 0, 0, s, 0)),
                   pl.BlockSpec((1, N_KV, 2, half, LANES), lambda b, s, pt: (b, 0, 0, s, 0))),
        scratch_shapes=[pltpu.VMEM((2, PAGES_PER_STEP, 2, N_KV, HEAD_DIM, PAGE_SIZE), F32),
                        pltpu.SemaphoreType.DMA((2,)),
                        pltpu.VMEM((2, PAGES_PER_STEP * PAGE_SIZE, LANES), F32),
                        pltpu.VMEM((2, nb, CMP_BLOCK * LANES), BF16)],
    )
    kc, vc = pl.pallas_call(
        functools.partial(_compress_sample_kernel, layer=layer, n_steps=n_steps, total_steps=n_seq * n_steps),
        grid_spec=grid_spec,
        out_shape=(jax.ShapeDtypeStruct((n_seq, N_KV, 2, nc_half, LANES), BF16),
                   jax.ShapeDtypeStruct((n_seq, N_KV, 2, nc_half, LANES), BF16)),
        compiler_params=_cparams(("arbitrary", "arbitrary")),
        name="compress_sample",
    )(page_table.reshape(-1), cache_t, pe4, w1p, w2p)
    return (kc.reshape(n_seq, N_KV, 2 * nc_half, LANES), vc.reshape(n_seq, N_KV, 2 * nc_half, LANES))


def _row_slopes(shape):
    head = lax.broadcasted_iota(I32, shape, 0)
    out = jnp.zeros(shape, F32)
    for h in range(N_HEADS):
        out = jnp.where(head == h, 2.0 ** (-8.0 * (h + 1) / N_HEADS), out)
    return out


def _sample_select_kernel(q_ref, kc_ref, vc_ref, ocmp_ref, imp_ref, *, nc):
    q = q_ref[0]
    ns = nc // 2
    rowgroup = lax.broadcasted_iota(I32, (N_HEADS, 1), 0) // GROUP
    o = jnp.zeros((N_HEADS, LANES), F32)
    for g in range(N_KV):
        s = _dot_nt(q, kc_ref[0, g])
        m = jnp.max(s, axis=-1, keepdims=True)
        p = jnp.exp(s - m)
        p = p / jnp.maximum(jnp.sum(p, axis=-1, keepdims=True), 1e-30)
        og = jnp.dot(p.astype(BF16), vc_ref[0, g], preferred_element_type=F32)
        mine = rowgroup == g
        o = jnp.where(mine, og, o)
        ph = jnp.sum(jnp.where(mine, p, 0.0), axis=0, keepdims=True)
        imp_ref[0, pl.ds(g, 1), :] = ph[:, :ns] + ph[:, ns:]
    ocmp_ref[0] = o


def _sample_select(q_aug, kc, vc):
    n_seq, _, nc, _ = kc.shape
    ns = nc // 2
    return pl.pallas_call(
        functools.partial(_sample_select_kernel, nc=nc),
        grid=(n_seq,),
        in_specs=[pl.BlockSpec((1, N_HEADS, LANES), lambda b: (b, 0, 0)),
                  pl.BlockSpec((1, N_KV, nc, LANES), lambda b: (b, 0, 0, 0)),
                  pl.BlockSpec((1, N_KV, nc, LANES), lambda b: (b, 0, 0, 0))],
        out_specs=(pl.BlockSpec((1, N_HEADS, LANES), lambda b: (b, 0, 0)),
                   pl.BlockSpec((1, N_KV, ns), lambda b: (b, 0, 0))),
        out_shape=(jax.ShapeDtypeStruct((n_seq, N_HEADS, LANES), F32),
                   jax.ShapeDtypeStruct((n_seq, N_KV, ns), F32)),
        compiler_params=_cparams(("arbitrary",)),
        name="sample_select",
    )(q_aug, kc, vc)


def _sample_topk_kernel(imp_ref, ids_ref, *, ns, k):
    imp = imp_ref[...]
    lane = lax.broadcasted_iota(I32, imp.shape, 1)
    out_lane = lax.broadcasted_iota(I32, ids_ref.shape, 1)
    big = jnp.int32(2 ** 30)
    v = jnp.where((lane == 0) | (lane == ns - 1), FORCED, imp)
    ids = jnp.zeros(ids_ref.shape, I32)
    for t in range(k):
        m = jnp.max(v, axis=-1, keepdims=True)
        first = jnp.min(jnp.where(v == m, lane, big), axis=-1, keepdims=True)
        v = jnp.where(lane == first, REMOVED, v)
        ids = jnp.where(out_lane == t, first, ids)
    ids_ref[...] = ids


def _sample_topk(imp, k):
    rows, ns = imp.shape
    return pl.pallas_call(
        functools.partial(_sample_topk_kernel, ns=ns, k=k),
        out_shape=jax.ShapeDtypeStruct((rows, LANES), I32),
        name="sample_topk",
    )(imp)


SEL_PAST = SEL_TOPK - 1


def _sample_attend_kernel(pt_ref, ids_ref, *refs, past_len):
    del pt_ref
    nblk = N_KV * SEL_PAST
    blocks = refs[:nblk]
    win_ref, new_ref, q_ref, ocmp_ref, gate_ref, o_ref = refs[nblk:]
    b = pl.program_id(0)
    slope = _row_slopes((N_HEADS, 1))
    rowgroup = lax.broadcasted_iota(I32, (N_HEADS, 1), 0) // GROUP
    gates = gate_ref[0]
    w_buf = win_ref.shape[-1]
    q = q_ref[0]
    qf = q.astype(F32)
    lane = lax.broadcasted_iota(I32, (1, PAGE_SIZE), 1)
    o = jnp.zeros((N_HEADS, HEAD_DIM), F32)
    for g in range(N_KV):
        def new_row(j):
            return new_ref[0, j * N_KV + g:j * N_KV + g + 1, :].astype(BF16).astype(F32)

        s_new = jnp.sum(qf * new_row(2), axis=-1, keepdims=True)
        scores, oks = [], []
        for t in range(SEL_PAST):
            blk = ids_ref[(b * N_KV + g) * SEL_TOPK + t]
            kt = blocks[g * SEL_PAST + t][0].astype(BF16)
            in_blk = (lane // SEL_BLOCK) == (blk % 2)
            pos = (blk // 2) * PAGE_SIZE + lane
            s = jnp.dot(q, kt, preferred_element_type=F32) - slope * (past_len - pos).astype(F32)
            scores.append(jnp.where(in_blk, s, NEG))
            oks.append(in_blk)
        m = s_new
        for s in scores:
            m = jnp.maximum(m, jnp.max(s, axis=-1, keepdims=True))
        p_new = jnp.exp(s_new - m)
        den = p_new
        acc = p_new * new_row(3)
        for t in range(SEL_PAST):
            p = jnp.where(oks[t], jnp.exp(scores[t] - m), 0.0)
            den = den + jnp.sum(p, axis=-1, keepdims=True)
            acc = acc + _dot_nt(p, blocks[g * SEL_PAST + t][1])
        o_sel = acc / jnp.maximum(den, 1e-30)
        kwt = win_ref[0, g].astype(BF16)
        dist = w_buf - lax.broadcasted_iota(I32, (1, w_buf), 1)
        ok = dist < WINDOW
        s = jnp.where(ok, jnp.dot(q, kwt, preferred_element_type=F32) - slope * dist.astype(F32), NEG)
        s_new = jnp.sum(qf * new_row(4), axis=-1, keepdims=True)
        m = jnp.maximum(s_new, jnp.max(s, axis=-1, keepdims=True))
        p = jnp.where(ok, jnp.exp(s - m), 0.0)
        p_new = jnp.exp(s_new - m)
        den = p_new + jnp.sum(p, axis=-1, keepdims=True)
        o_win = (p_new * new_row(5) + _dot_nt(p, win_ref[1, g])) / jnp.maximum(den, 1e-30)
        og = gates[:, 0:1] * ocmp_ref[0][:, :HEAD_DIM] + gates[:, 1:2] * o_sel + gates[:, 2:3] * o_win
        o = jnp.where(rowgroup == g, og, o)
    o_ref[0] = o


def _sample_attend(cache_t, win_t, layer, page_table, ids, kv_new, q, ocmp, gates, past_len):
    n_seq, n_pages = page_table.shape
    w_buf = win_t.shape[-1]
    ids3 = ids.reshape(n_seq, N_KV, SEL_TOPK)
    phys = jnp.take_along_axis(page_table[:, None, :], ids3 // 2, axis=-1)
    phys = jnp.pad(phys, ((0, 1), (0, 0), (0, 0))).astype(I32)

    def blk_map(g, t):
        return lambda b, ph, idr: (layer, ph[(b * N_KV + g) * SEL_TOPK + t], 1, g, 0, 0)

    grid_spec = pltpu.PrefetchScalarGridSpec(
        num_scalar_prefetch=2,
        grid=(n_seq,),
        in_specs=[pl.BlockSpec((None, None, 2, None, HEAD_DIM, PAGE_SIZE), blk_map(g, t))
                  for g in range(N_KV) for t in range(SEL_PAST)] + [
            pl.BlockSpec((None, None, 2, N_KV, HEAD_DIM, w_buf), lambda b, pt, idr: (layer, b, 0, 0, 0, 0)),
            pl.BlockSpec((1, 6 * N_KV, HEAD_DIM), lambda b, pt, idr: (b, 0, 0)),
            pl.BlockSpec((1, N_HEADS, HEAD_DIM), lambda b, pt, idr: (b, 0, 0)),
            pl.BlockSpec((1, N_HEADS, LANES), lambda b, pt, idr: (b, 0, 0)),
            pl.BlockSpec((1, N_HEADS, LANES), lambda b, pt, idr: (b, 0, 0))],
        out_specs=pl.BlockSpec((1, N_HEADS, HEAD_DIM), lambda b, pt, idr: (b, 0, 0)),
    )
    return pl.pallas_call(
        functools.partial(_sample_attend_kernel, past_len=past_len),
        grid_spec=grid_spec,
        out_shape=jax.ShapeDtypeStruct((n_seq, N_HEADS, HEAD_DIM), F32),
        compiler_params=_cparams(("arbitrary",)),
        name="sample_attend",
    )(phys.reshape(-1), ids.reshape(-1), *([cache_t] * (N_KV * SEL_PAST)), win_t, kv_new, q, ocmp, gates)


def _nsa_sample(cache_t, win_t, layer, page_table, kv_new, qt, gt, pe, w1, w2):
    n_seq, n_pages = page_table.shape
    past_len = n_pages * PAGE_SIZE
    kc, vc = _compress_sample(cache_t, layer, page_table, pe, w1, w2)
    q = qt.T.reshape(n_seq, N_HEADS, HEAD_DIM).astype(BF16)
    slopes = (2.0 ** (-8.0 * jnp.arange(1, N_HEADS + 1, dtype=F32) / N_HEADS)).reshape(1, N_HEADS, 1)
    aug = jnp.concatenate([slopes * 128.0, slopes, -(slopes * 128.0) * (past_len // 128),
                           jnp.zeros((1, N_HEADS, LANES - HEAD_DIM - 3), F32)], axis=-1)
    q_aug = jnp.concatenate([q, jnp.broadcast_to(aug, (n_seq, N_HEADS, LANES - HEAD_DIM)).astype(BF16)], axis=-1)
    ocmp, imp = _sample_select(q_aug, kc, vc)
    ids = _sample_topk(imp.reshape(n_seq * N_KV, -1), SEL_PAST)[:, :SEL_TOPK]
    gates = gt[:3 * N_HEADS].reshape(3, N_HEADS, n_seq).transpose(2, 1, 0)
    gates = jnp.pad(gates, ((0, 0), (0, 0), (0, LANES - 3)))
    o = _sample_attend(cache_t, win_t, layer, page_table, ids, kv_new.reshape(n_seq, 6 * N_KV, HEAD_DIM), q,
                       ocmp, gates, past_len)
    return o.reshape(n_seq, D_NSA)


def _sg_chunk_params(sg_w, sg_b):
    w = jnp.tril(sg_w).astype(BF16)
    bias = jnp.repeat(sg_b.T, SG_DIM, axis=1)
    return w, bias


def _pad_rows(a, mult):
    pad = (-a.shape[0]) % mult
    return jnp.pad(a, ((0, pad),) + ((0, 0),) * (a.ndim - 1)) if pad else a


def _prep_weights(norm_mix_g, norm_ffn_g, norm_final_g, w_in, cmp_pe, cmp_w1, cmp_w2, sg_norm_g, sg_norm_b,
                  sg_w, sg_b, w_branch_nsa, w_branch_sg, w_out, ffn_w_gu, ffn_w_down, router_w, router_b,
                  moe_w_gu, moe_w_down):
    depth = w_in.shape[0]
    layers = []
    for i in range(depth):
        wn, wt = _pack_w_in(w_in[i])
        sgw_chunk, sgb_chunk = _sg_chunk_params(sg_w[i], sg_b[i])
        lw = {
            "norm_mix": norm_mix_g[i].reshape(1, D_MODEL), "norm_ffn": norm_ffn_g[i].reshape(1, D_MODEL),
            "norm_final": norm_final_g.reshape(1, D_MODEL),
            "sg_norm_g": sg_norm_g[i].reshape(1, D_SG), "sg_norm_b": sg_norm_b[i].reshape(1, D_SG),
            "cmp_pe": cmp_pe[i], "cmp_w1": cmp_w1[i], "cmp_w2": cmp_w2[i],
            "f32": {"wn": wn, "wt": wt, "wa": w_branch_nsa[i], "wb": w_branch_sg[i], "wo": w_out[i]},
            "bf16": {"wn": wn.astype(BF16), "wt": wt.astype(BF16), "wa": w_branch_nsa[i].astype(BF16),
                     "wb": w_branch_sg[i].astype(BF16), "wo": w_out[i].astype(BF16)},
            "sgw_chunk": sgw_chunk, "sgb_chunk": sgb_chunk,
            "sgw_first": jnp.repeat(sg_w[i][:, 0, 0], SG_DIM).reshape(1, D_SG),
            "sgb_first": jnp.repeat(sg_b[i][:, 0], SG_DIM).reshape(1, D_SG),
            "routed": i % 2 == 1, "final": i == depth - 1,
        }
        if i % 2 == 0:
            gu = ffn_w_gu[i // 2][None]
            wd = ffn_w_down[i // 2].reshape(-1, D_FF_CHUNK, D_MODEL)
            lw["router_w"] = jnp.zeros((D_MODEL, LANES), F32)
            lw["router_b"] = jnp.zeros((1, LANES), F32)
        else:
            gu, wd = moe_w_gu[i // 2], moe_w_down[i // 2]
            lw["router_w"] = jnp.pad(router_w[i // 2], ((0, 0), (0, LANES - N_EXPERTS)))
            lw["router_b"] = jnp.pad(router_b[i // 2], (0, LANES - N_EXPERTS)).reshape(1, LANES)
        lw["f32"].update(gu=gu, wd=wd)
        lw["bf16"].update(gu=gu.astype(BF16), wd=wd.astype(BF16))
        layers.append(lw)
    return layers


def _prompt_layer(x, mod, lw, tm):
    sh1, sc1, g1, sh2, sc2, g2 = mod
    w = lw["bf16"]
    kvt, u, v, ga, gb, kaug, qt, vt, gt, kvc = _in_proj(x, lw["norm_mix"], sc1, sh1, w["wn"], w["wt"],
                                                         lw["sg_norm_g"], lw["sg_norm_b"], tm, "cmp", False)
    onsa = _nsa_prompt(kvc, kaug, qt, vt, gt, lw["cmp_pe"], lw["cmp_w1"], lw["cmp_w2"])
    x = _mix_out(x, onsa, u, v, ga, gb, g1, w["wa"], w["wb"], w["wo"], lw["sgw_chunk"], lw["sgb_chunk"],
                 tm, True, False)
    x = _ffn(x, lw["norm_ffn"], sc2, sh2, g2, w["gu"], w["wd"], lw["router_w"], lw["router_b"],
             lw["norm_final"], lw["ffn_tm"], lw["routed"], lw["final"], False)
    return x, kvt


def _sample_layer(x, mod, lw, cache_t, win_t, layer, page_table):
    sh1, sc1, g1, sh2, sc2, g2 = mod
    tm = x.shape[0]
    w = lw["f32"]
    kvt, u, v, ga, gb, _, qt, _, gt, kv = _in_proj(x, lw["norm_mix"], sc1, sh1, w["wn"], w["wt"],
                                                   lw["sg_norm_g"], lw["sg_norm_b"], tm, "all", True)
    onsa = _nsa_sample(cache_t, win_t, layer, page_table, kv, qt, gt, lw["cmp_pe"], lw["cmp_w1"], lw["cmp_w2"])
    x = _mix_out(x, onsa, u, v, ga, gb, g1, w["wa"], w["wb"], w["wo"], lw["sgw_first"], lw["sgb_first"],
                 tm, False, True)
    x = _ffn(x, lw["norm_ffn"], sc2, sh2, g2, w["gu"], w["wd"], lw["router_w"], lw["router_b"],
             lw["norm_final"], tm, lw["routed"], lw["final"], True)
    return x, kv, kvt, v


PROMPT_ROW_TILE = 512
PROMPT_FFN_ROW_TILE = 512


def kernel(x_prompt, x_sample, cache_kv, state_win, page_table, c_prompt, c_sample, norm_mix_g, norm_ffn_g,
           norm_final_g, w_ada, b_ada, w_in, cmp_pe, cmp_w1, cmp_w2, sg_norm_g, sg_norm_b, sg_w, sg_b,
           w_branch_nsa, w_branch_sg, w_out, ffn_w_gu, ffn_w_down, router_w, router_b, moe_w_gu, moe_w_down):
    batch, seq, _ = x_prompt.shape
    n_seq, dec_seq, _ = x_sample.shape
    depth = w_in.shape[0]
    assert batch == 1 and dec_seq == 1
    assert seq % PROMPT_FFN_ROW_TILE == 0 and seq // Q_BLOCK <= 256
    past_len = page_table.shape[1] * PAGE_SIZE
    assert past_len % CHUNK == 0 and past_len // LANES <= 256
    assert state_win.shape[2] == WINDOW and page_table.shape[1] % PAGES_PER_STEP == 0

    layers = _prep_weights(norm_mix_g, norm_ffn_g, norm_final_g, w_in, cmp_pe, cmp_w1, cmp_w2, sg_norm_g,
                           sg_norm_b, sg_w, sg_b, w_branch_nsa, w_branch_sg, w_out, ffn_w_gu, ffn_w_down,
                           router_w, router_b, moe_w_gu, moe_w_down)
    c_all = _pad_rows(jnp.concatenate([c_prompt, c_sample], axis=0), SUBLANES)
    mods = _ada(c_all, w_ada, b_ada)

    cache_t = cache_kv.transpose(0, 1, 3, 4, 5, 2)
    win_t = state_win.transpose(0, 1, 3, 4, 5, 2)

    xp = x_prompt[0]
    xs = x_sample[:, 0]
    kv_p, kv_s, win_p, win_new, sgv_s = [], [], [], [], []
    w_keep = min(WINDOW, seq)
    kv_rows = 4 * N_KV * HEAD_DIM
    for i in range(depth):
        lw = dict(layers[i])
        lw["ffn_tm"] = PROMPT_FFN_ROW_TILE
        mod_p = tuple(mods[i, 0:1, j * D_MODEL:(j + 1) * D_MODEL] for j in range(6))
        mod_s = tuple(mods[i, 1:1 + n_seq, j * D_MODEL:(j + 1) * D_MODEL] for j in range(6))
        xp, kvt_p = _prompt_layer(xp, mod_p, lw, PROMPT_ROW_TILE)
        xs, kvs, kvt_s, v_s = _sample_layer(xs, mod_s, lw, cache_t, win_t, i, page_table)
        kv_p.append(kvt_p[:kv_rows].reshape(4, N_KV, HEAD_DIM, seq))
        win_p.append(kvt_p[kv_rows:, seq - w_keep:].reshape(2, N_KV, HEAD_DIM, w_keep))
        kv_s.append(kvs[:, :kv_rows].reshape(n_seq, 1, 4, N_KV, HEAD_DIM))
        win_new.append(kvt_s[kv_rows:].reshape(2, N_KV, HEAD_DIM, n_seq).transpose(3, 0, 1, 2)[..., None])
        sgv_s.append(v_s.reshape(n_seq, 1, D_SG))
    kv_prompt = jnp.stack(kv_p).transpose(0, 4, 1, 2, 3)[:, None]
    win_prompt = jnp.stack(win_p).transpose(0, 4, 1, 2, 3)[:, None]
    win_sample = jnp.concatenate([win_t[..., 1:], jnp.stack(win_new)], axis=-1).transpose(0, 1, 5, 2, 3, 4)
    return (xp[None], xs[:, None], kv_prompt, jnp.stack(kv_s), win_prompt, win_sample, jnp.stack(sgv_s))
```

```python
import functools

import jax
import jax.numpy as jnp
from jax import lax
from jax.experimental import pallas as pl
from jax.experimental.pallas import tpu as pltpu

F32 = jnp.float32
BF16 = jnp.bfloat16
I32 = jnp.int32
HIGHEST = lax.Precision.HIGHEST

LANES = 128
SUBLANES = 8
VMEM_LIMIT_BYTES = 56 * 1024 * 1024

D_MODEL = 1024
N_HEADS = 8
HEAD_DIM = 64
N_KV = 2
GROUP = N_HEADS // N_KV
D_NSA = N_HEADS * HEAD_DIM
CMP_BLOCK = 32
CMP_HIDDEN = 256
SEL_BLOCK = 64
SEL_TOPK = 16
WINDOW = 512
Q_BLOCK = 128
N_SG = 8
SG_DIM = 64
D_SG = N_SG * SG_DIM
CHUNK = 128
N_EXPERTS = 8
D_FF_CHUNK = 1408
PAGE_SIZE = 128
EPS = 1e-6
NEG = -1e30
FORCED = 1e9
REMOVED = -3e38

KV_COLS = 6 * N_KV * HEAD_DIM
OFF_KV = D_NSA
OFF_G = OFF_KV + KV_COLS
OFF_U = OFF_G + 3 * N_HEADS
OFF_V = OFF_U + D_SG
OFF_GA = OFF_V + D_SG
OFF_GB = OFF_GA + D_MODEL
IN_COLS = OFF_GB + D_MODEL

WN_U = 0
WN_V = WN_U + D_SG
WN_GA = WN_V + D_SG
WN_GB = WN_GA + D_MODEL
WN_KAUG = WN_GB + D_MODEL
WN_KV = WN_KAUG + 4 * LANES
WN_COLS = WN_KV + KV_COLS
WT_Q = 0
WT_KV = D_NSA
WT_G = WT_KV + KV_COLS
WT_ROWS = WT_G + 32

POS_HI_LANE = HEAD_DIM
POS_LO_LANE = HEAD_DIM + 1
ONE_LANE = HEAD_DIM + 2


def _cparams(sem):
    return pltpu.CompilerParams(dimension_semantics=sem, vmem_limit_bytes=VMEM_LIMIT_BYTES)


def _bdot(a, b):
    return jnp.dot(a.astype(BF16), b.astype(BF16), preferred_element_type=F32)


def _dot_nt(a, b):
    return lax.dot_general(a.astype(BF16), b.astype(BF16), (((1,), (1,)), ((), ())),
                           preferred_element_type=F32)


def _mm(a, b, precise):
    if precise:
        return jnp.dot(a.astype(F32), b.astype(F32), precision=HIGHEST, preferred_element_type=F32)
    return _bdot(a, b)


def _mm_nt(a, b, precise):
    if precise:
        return lax.dot_general(a.astype(F32), b.astype(F32), (((1,), (1,)), ((), ())), precision=HIGHEST,
                               preferred_element_type=F32)
    return _dot_nt(a, b)


def _silu(x):
    return x * jax.nn.sigmoid(x)


def _pos_aug(pos, lane):
    hi = (pos >> 7).astype(F32)
    lo = (pos & 127).astype(F32)
    return jnp.where(lane == POS_HI_LANE, hi,
                     jnp.where(lane == POS_LO_LANE, lo,
                               jnp.where(lane == ONE_LANE, 1.0, 0.0)))


def _group_slopes(g, lane_head):
    out = jnp.zeros(lane_head.shape, F32)
    for gg in range(N_KV):
        for hh in range(GROUP):
            s = 2.0 ** (-8.0 * (gg * GROUP + hh + 1) / N_HEADS)
            out = jnp.where((lane_head == hh) & (g == gg), s, out)
    return out


def _qt_aug(qt_ref, g, q0_blocks):
    heads = [qt_ref[h * HEAD_DIM:(h + 1) * HEAD_DIM, :] for h in range(GROUP)]
    q = jnp.concatenate(heads, axis=1)
    n = GROUP * Q_BLOCK
    row = lax.broadcasted_iota(I32, (LANES - HEAD_DIM, n), 0)
    lane_head = lax.broadcasted_iota(I32, (LANES - HEAD_DIM, n), 1) // Q_BLOCK
    slope = _group_slopes(g, lane_head)
    q0f = (q0_blocks).astype(F32)
    aug = jnp.where(row == 0, slope * 128.0,
                    jnp.where(row == 1, slope,
                              jnp.where(row == 2, -(slope * 128.0) * q0f, 0.0)))
    return jnp.concatenate([q.astype(BF16), aug.astype(BF16)], axis=0)


def _ada_kernel(c_ref, w_ref, b_ref, o_ref):
    c = c_ref[...]
    o_ref[0] = jnp.dot(_silu(c), w_ref[0], precision=HIGHEST, preferred_element_type=F32) + b_ref[0]


def _ada(c_all, w_ada, b_ada):
    depth = w_ada.shape[0]
    rows = c_all.shape[0]
    tn = 1024
    n = w_ada.shape[2]
    return pl.pallas_call(
        _ada_kernel,
        grid=(depth, n // tn),
        in_specs=[pl.BlockSpec((rows, D_MODEL), lambda l, j: (0, 0)),
                  pl.BlockSpec((1, D_MODEL, tn), lambda l, j: (l, 0, j)),
                  pl.BlockSpec((1, 1, tn), lambda l, j: (l, 0, j))],
        out_specs=pl.BlockSpec((1, rows, tn), lambda l, j: (l, 0, j)),
        out_shape=jax.ShapeDtypeStruct((depth, rows, n), F32),
        compiler_params=_cparams(("arbitrary", "arbitrary")),
        name="ada",
    )(c_all, w_ada, b_ada.reshape(depth, 1, n))


def _in_proj_kernel(x_ref, g_ref, sc_ref, sh_ref, wn_ref, wt_ref, lng_ref, lnb_ref, *out_refs, tm, natural_kv,
                    precise):
    kvt_ref, u_ref, v_ref, ga_ref, gb_ref, kaug_ref, qt_ref, vt_ref, gt_ref, kv_ref = out_refs
    i = pl.program_id(0)
    x = x_ref[...]
    ms = jnp.mean(x * x, axis=-1, keepdims=True)
    h = x * lax.rsqrt(ms + EPS) * g_ref[...]
    h = h * (1.0 + sc_ref[...]) + sh_ref[...]
    hb = h if precise else h.astype(BF16)

    def seg(a, b):
        return _mm(hb, wn_ref[:, a:b], precise)

    u_ref[...] = jax.nn.gelu(seg(WN_U, WN_V)).astype(u_ref.dtype)
    v = jax.nn.gelu(seg(WN_V, WN_GA))
    mu = jnp.mean(v, axis=-1, keepdims=True)
    var = jnp.mean(jnp.square(v - mu), axis=-1, keepdims=True)
    v_ref[...] = (v - mu) * lax.rsqrt(var + EPS) * lng_ref[...] + lnb_ref[...]
    ga_ref[...] = jax.nn.sigmoid(seg(WN_GA, WN_GB)).astype(ga_ref.dtype)
    gb_ref[...] = jax.nn.sigmoid(seg(WN_GB, WN_KAUG)).astype(gb_ref.dtype)

    pos = lax.broadcasted_iota(I32, (tm, LANES), 0) + i * tm
    lane = lax.broadcasted_iota(I32, (tm, LANES), 1)
    aug = _pos_aug(pos, lane)
    for j in range(4):
        k = seg(WN_KAUG + j * LANES, WN_KAUG + (j + 1) * LANES)
        kaug_ref[j] = (k + aug).astype(BF16)
    if natural_kv == "all":
        kv_ref[...] = seg(WN_KV, WN_COLS)
    else:
        for j in range(2):
            kv_ref[j] = seg(WN_KV + j * LANES, WN_KV + (j + 1) * LANES)

    zt = _mm_nt(wt_ref[...], hb, precise)
    qt_ref[...] = (zt[WT_Q:WT_KV] * (HEAD_DIM ** -0.5)).astype(qt_ref.dtype)
    kvt = zt[WT_KV:WT_G]
    kvt_ref[...] = kvt
    two = N_KV * HEAD_DIM
    vt_ref[0:two, :] = kvt[3 * two:4 * two].astype(BF16)
    vt_ref[two:2 * two, :] = kvt[5 * two:6 * two].astype(BF16)
    gt_ref[...] = jax.nn.sigmoid(zt[WT_G:WT_ROWS])


def _in_proj(x, g, sc, sh, wn, wt, lng, lnb, tm, natural_kv, precise):
    m = x.shape[0]
    act = F32 if precise else BF16
    mod_rows = sc.shape[0]
    mod_block = (1, D_MODEL) if mod_rows == 1 else (tm, D_MODEL)
    mod_map = (lambda i: (0, 0)) if mod_rows == 1 else (lambda i: (i, 0))
    row = lambda i: (i, 0)
    col = lambda i: (0, i)
    const = lambda i: (0, 0)
    out_shape = [
        jax.ShapeDtypeStruct((KV_COLS, m), F32),
        jax.ShapeDtypeStruct((m, D_SG), act),
        jax.ShapeDtypeStruct((m, D_SG), F32),
        jax.ShapeDtypeStruct((m, D_MODEL), act),
        jax.ShapeDtypeStruct((m, D_MODEL), act),
        jax.ShapeDtypeStruct((4, m, LANES), BF16),
        jax.ShapeDtypeStruct((D_NSA, m), act),
        jax.ShapeDtypeStruct((4 * HEAD_DIM, m), BF16),
        jax.ShapeDtypeStruct((32, m), F32),
    ]
    out_specs = [
        pl.BlockSpec((KV_COLS, tm), col),
        pl.BlockSpec((tm, D_SG), row),
        pl.BlockSpec((tm, D_SG), row),
        pl.BlockSpec((tm, D_MODEL), row),
        pl.BlockSpec((tm, D_MODEL), row),
        pl.BlockSpec((4, tm, LANES), lambda i: (0, i, 0)),
        pl.BlockSpec((D_NSA, tm), col),
        pl.BlockSpec((4 * HEAD_DIM, tm), col),
        pl.BlockSpec((32, tm), col),
    ]
    if natural_kv == "all":
        out_shape.append(jax.ShapeDtypeStruct((m, KV_COLS), F32))
        out_specs.append(pl.BlockSpec((tm, KV_COLS), row))
    else:
        out_shape.append(jax.ShapeDtypeStruct((2, m, LANES), F32))
        out_specs.append(pl.BlockSpec((2, tm, LANES), lambda i: (0, i, 0)))
    return pl.pallas_call(
        functools.partial(_in_proj_kernel, tm=tm, natural_kv=natural_kv, precise=precise),
        grid=(m // tm,),
        in_specs=[pl.BlockSpec((tm, D_MODEL), row),
                  pl.BlockSpec((1, D_MODEL), const),
                  pl.BlockSpec(mod_block, mod_map),
                  pl.BlockSpec(mod_block, mod_map),
                  pl.BlockSpec((D_MODEL, WN_COLS), const),
                  pl.BlockSpec((WT_ROWS, D_MODEL), const),
                  pl.BlockSpec((1, D_SG), const),
                  pl.BlockSpec((1, D_SG), const)],
        out_specs=tuple(out_specs),
        out_shape=tuple(out_shape),
        compiler_params=_cparams(("arbitrary",)),
        name="in_proj",
    )(x, g, sc, sh, wn, wt, lng, lnb)


def _pack_w_in(w):
    kv = w[:, OFF_KV:OFF_G]

    def kvcol(j, g):
        return kv[:, (j * N_KV + g) * HEAD_DIM:(j * N_KV + g + 1) * HEAD_DIM]

    zpad = jnp.zeros((D_MODEL, LANES - HEAD_DIM), w.dtype)
    kaug = [jnp.concatenate([kvcol(j, g), zpad], axis=1) for j in (2, 4) for g in range(N_KV)]
    wn = jnp.concatenate([w[:, OFF_U:OFF_V], w[:, OFF_V:OFF_GA], w[:, OFF_GA:OFF_GB],
                          w[:, OFF_GB:IN_COLS]] + kaug + [kv], axis=1)
    gpad = jnp.zeros((D_MODEL, 32 - 3 * N_HEADS), w.dtype)
    wt = jnp.concatenate([w[:, :OFF_KV], kv, w[:, OFF_G:OFF_U], gpad], axis=1).T
    return wn, wt


def _compress_weights(pe, w1, w2):
    kd = CMP_BLOCK * LANES
    pe_flat = jnp.concatenate([pe, pe], axis=-1).reshape(2, 1, kd)
    w1r = w1.reshape(2, CMP_BLOCK, HEAD_DIM, CMP_HIDDEN)
    w1p = jnp.einsum("gh,jtdn->jtgdhn", jnp.eye(N_KV, dtype=w1.dtype), w1r)
    w1p = w1p.reshape(2, kd, N_KV * CMP_HIDDEN).astype(BF16)
    w2p = jnp.pad(w2, ((0, 0), (0, 0), (0, LANES - HEAD_DIM))).astype(BF16)
    return pe_flat, w1p, w2p


def _compress_kernel(x_ref, pe_ref, w1_ref, w2_ref, w2vt_ref, kc_ref, vct_ref, *, half):
    parity = pl.program_id(0)
    row = lax.broadcasted_iota(I32, (half, LANES), 0)
    lane = lax.broadcasted_iota(I32, (half, LANES), 1)
    aug = _pos_aug((2 * row + parity) * CMP_BLOCK + (CMP_BLOCK - 1), lane)
    hid_k = _bdot(x_ref[0] + pe_ref[0], w1_ref[0])
    hid_v = _bdot(x_ref[1] + pe_ref[1], w1_ref[1])
    for g in range(N_KV):
        cols = slice(g * CMP_HIDDEN, (g + 1) * CMP_HIDDEN)
        kc_ref[g] = (_bdot(_silu(hid_k[:, cols]), w2_ref[0]) + aug).astype(BF16)
        vct_ref[g] = _dot_nt(w2vt_ref[...], _silu(hid_v[:, cols])).astype(BF16)


def _compress_prompt(kvc, pe, w1, w2):
    m = kvc.shape[1]
    nc = m // CMP_BLOCK
    half = nc // 2
    kd = CMP_BLOCK * LANES
    x = kvc.reshape(2, half, 2 * kd)
    pe_flat, w1p, w2p = _compress_weights(pe, w1, w2)
    w2vt = w2[1].T.astype(BF16)
    return pl.pallas_call(
        functools.partial(_compress_kernel, half=half),
        grid=(2,),
        in_specs=[pl.BlockSpec((2, half, kd), lambda p: (0, 0, p)),
                  pl.BlockSpec((2, 1, kd), lambda p: (0, 0, 0)),
                  pl.BlockSpec((2, kd, N_KV * CMP_HIDDEN), lambda p: (0, 0, 0)),
                  pl.BlockSpec((2, CMP_HIDDEN, LANES), lambda p: (0, 0, 0)),
                  pl.BlockSpec((HEAD_DIM, CMP_HIDDEN), lambda p: (0, 0))],
        out_specs=(pl.BlockSpec((N_KV, half, LANES), lambda p: (0, p, 0)),
                   pl.BlockSpec((N_KV, HEAD_DIM, half), lambda p: (0, 0, p))),
        out_shape=(jax.ShapeDtypeStruct((N_KV, nc, LANES), BF16),
                   jax.ShapeDtypeStruct((N_KV, HEAD_DIM, nc), BF16)),
        compiler_params=_cparams(("arbitrary",)),
        name="compress_prompt",
    )(x, pe_flat, w1p, w2p, w2vt)


def _topk_mask_rows(imp, k):
    rows = lax.broadcasted_iota(I32, imp.shape, 0)
    big = jnp.int32(2 ** 30)

    def body(_, v):
        m = jnp.max(v, axis=0, keepdims=True)
        first = jnp.min(jnp.where(v == m, rows, big), axis=0, keepdims=True)
        return jnp.where(rows == first, REMOVED, v)

    return jnp.where(lax.fori_loop(0, k, body, imp) < 0.5 * REMOVED, 1.0, 0.0)


def _visited_blocks(sel, limit):
    ns = sel.shape[0]
    ones_q = jnp.ones((SUBLANES, sel.shape[1]), BF16)
    blk = lax.broadcasted_iota(I32, (SUBLANES, ns), 1)
    flags = ((_dot_nt(ones_q, sel) > 0.5) & (blk < limit)).astype(F32)
    r = lax.broadcasted_iota(I32, (ns, ns), 0)
    c = lax.broadcasted_iota(I32, (ns, ns), 1)
    upper = jnp.where(r <= c, 1.0, 0.0)
    prefix = _bdot(flags, upper)
    before = jnp.where(prefix[0:1, :] <= r.astype(F32), 1.0, 0.0)
    ids = _dot_nt(jnp.ones((SUBLANES, ns), BF16), before)
    ids = jnp.minimum(ids, ns - 1.0).astype(I32)
    total = jnp.broadcast_to(prefix[:, ns - 1:ns], (SUBLANES, ns))
    return ids, total


def _select_tile(qt_ref, kc_ref, vct_ref, mask_ref, ids_ref, cnt_ref, ocmp_ref, i, nc, live):
    ns = nc // 2
    n = GROUP * Q_BLOCK
    q0 = i * Q_BLOCK
    r = lax.broadcasted_iota(I32, (live, 1), 0)
    cend = jnp.concatenate([2 * r, 2 * r + 1], axis=0) * CMP_BLOCK + (CMP_BLOCK - 1)
    qpos = q0 + lax.broadcasted_iota(I32, (1, n), 1) % Q_BLOCK
    valid = cend <= qpos
    blk = lax.broadcasted_iota(I32, (live, Q_BLOCK), 0)
    cur = (q0 + lax.broadcasted_iota(I32, (live, Q_BLOCK), 1)) // SEL_BLOCK
    forced = (blk == 0) | (blk == cur) | (blk == cur - 1)
    imps = []
    for g in range(N_KV):
        qa = _qt_aug(qt_ref.at[g * GROUP * HEAD_DIM:(g + 1) * GROUP * HEAD_DIM], g, i)
        kc = jnp.concatenate([kc_ref[g, 0:live], kc_ref[g, ns:ns + live]], axis=0)
        s = jnp.dot(kc, qa, preferred_element_type=F32)
        s = jnp.where(valid, s, NEG)
        m = jnp.max(s, axis=0, keepdims=True)
        p = jnp.where(valid, jnp.exp(s - m), 0.0)
        p = p / jnp.maximum(jnp.sum(p, axis=0, keepdims=True), 1e-30)
        vct = jnp.concatenate([vct_ref[g, :, 0:live], vct_ref[g, :, ns:ns + live]], axis=1)
        ocmp_ref[0, g] = jnp.dot(vct, p.astype(BF16), preferred_element_type=F32)
        ph = p[:, 0:Q_BLOCK]
        for h in range(1, GROUP):
            ph = ph + p[:, h * Q_BLOCK:(h + 1) * Q_BLOCK]
        imp = ph[:live] + ph[live:]
        imp = jnp.where(forced, FORCED, imp)
        imps.append(jnp.where(blk <= cur, imp, NEG))
    for g in range(N_KV):
        sel = jnp.where(blk <= cur, _topk_mask_rows(imps[g], min(SEL_TOPK, ns)), 0.0)
        if live < ns:
            sel = jnp.concatenate([sel, jnp.zeros((ns - live, Q_BLOCK), F32)], axis=0)
        mask_ref[0, g] = sel
        ids_ref[0, g], cnt_ref[0, g] = _visited_blocks(sel, 2 * i)


def _select_kernel(qt_ref, kc_ref, vct_ref, mask_ref, ids_ref, cnt_ref, ocmp_ref, *, nc):
    i = pl.program_id(0)
    ns = nc // 2
    refs = (qt_ref, kc_ref, vct_ref, mask_ref, ids_ref, cnt_ref, ocmp_ref)
    half = ns // 2
    if half % LANES == 0 and half >= SEL_TOPK:
        early = 2 * i + 1 < half

        @pl.when(early)
        def _():
            _select_tile(*refs, i, nc, half)

        @pl.when(jnp.logical_not(early))
        def _():
            _select_tile(*refs, i, nc, ns)
    else:
        _select_tile(*refs, i, nc, ns)


def _select(qt, kc, vct):
    m = qt.shape[1]
    nqb = m // Q_BLOCK
    nc = m // CMP_BLOCK
    ns = nc // 2
    n = GROUP * Q_BLOCK
    return pl.pallas_call(
        functools.partial(_select_kernel, nc=nc),
        grid=(nqb,),
        in_specs=[pl.BlockSpec((D_NSA, Q_BLOCK), lambda i: (0, i)),
                  pl.BlockSpec((N_KV, nc, LANES), lambda i: (0, 0, 0)),
                  pl.BlockSpec((N_KV, HEAD_DIM, nc), lambda i: (0, 0, 0))],
        out_specs=(pl.BlockSpec((1, N_KV, ns, Q_BLOCK), lambda i: (i, 0, 0, 0)),
                   pl.BlockSpec((1, N_KV, SUBLANES, ns), lambda i: (i, 0, 0, 0)),
                   pl.BlockSpec((1, N_KV, SUBLANES, ns), lambda i: (i, 0, 0, 0)),
                   pl.BlockSpec((1, N_KV, HEAD_DIM, n), lambda i: (i, 0, 0, 0))),
        out_shape=(jax.ShapeDtypeStruct((nqb, N_KV, ns, Q_BLOCK), F32),
                   jax.ShapeDtypeStruct((nqb, N_KV, SUBLANES, ns), I32),
                   jax.ShapeDtypeStruct((nqb, N_KV, SUBLANES, ns), F32),
                   jax.ShapeDtypeStruct((nqb, N_KV, HEAD_DIM, n), F32)),
        compiler_params=_cparams(("arbitrary",)),
        name="nsa_select",
    )(qt, kc, vct)


SEL_PER_STEP = 8
WIN_TILES_PER_STEP = 4


def _softmax_step(state, s, valid=None):
    m, l, _ = state
    if valid is not None:
        s = jnp.where(valid, s, NEG)
    m_new = jnp.maximum(m, jnp.max(s, axis=0, keepdims=True))
    alpha = jnp.exp(m - m_new)
    p = jnp.exp(s - m_new)
    if valid is not None:
        p = jnp.where(valid, p, 0.0)
    return m_new, alpha * l + jnp.sum(p, axis=0, keepdims=True), alpha, p.astype(BF16)


def _pv_lane_tiles(vt_tiles):
    vt = jnp.concatenate(vt_tiles, axis=1)
    return lambda pb: jnp.dot(vt, pb, preferred_element_type=F32)


def _pv_row_tiles(v_tiles):
    v = jnp.concatenate(v_tiles, axis=0)
    return lambda pb: lax.dot_general(v, pb, (((0,), (0,)), ((), ())), preferred_element_type=F32)


def _attend_kernel(cnt_ref, ids_ref, qt_ref, ksel_ref, vsel_ref, kwin_ref, vwin_ref, mask_ref,
                   ocmp_ref, gt_ref, o_ref, *, nqb, ns):
    g = pl.program_id(0)
    i = pl.program_id(1)
    q0 = i * Q_BLOCK
    qa = _qt_aug(qt_ref, g, i)
    qa_heads = [qa[:, h * Q_BLOCK:(h + 1) * Q_BLOCK] for h in range(GROUP)]
    empty = (jnp.full((1, Q_BLOCK), NEG, F32), jnp.zeros((1, Q_BLOCK), F32), jnp.zeros((HEAD_DIM, Q_BLOCK), F32))
    init = (empty,) * GROUP

    def heads_update(states, keys, pv_fn, bias=None, valid=None):
        scores = [jnp.dot(keys, qa_heads[h], preferred_element_type=F32) for h in range(GROUP)]
        if bias is not None:
            scores = [s + bias for s in scores]
        parts = [_softmax_step(states[h], scores[h], valid) for h in range(GROUP)]
        pvs = [pv_fn(p) for _, _, _, p in parts]
        return tuple((m, l, alpha * states[h][2] + pvs[h]) for h, (m, l, alpha, _) in enumerate(parts))

    slot = g * nqb + i
    count = cnt_ref[slot]
    key_off = lax.broadcasted_iota(I32, (Q_BLOCK, 1), 0)
    query_off = lax.broadcasted_iota(I32, (1, Q_BLOCK), 1)
    causal = key_off <= query_off

    own = mask_ref[0, 0, pl.ds(pl.multiple_of(2 * i, 2), 2), :]
    chosen = jnp.concatenate([jnp.broadcast_to(own[0:1] > 0.5, (SEL_BLOCK, Q_BLOCK)),
                              jnp.broadcast_to(own[1:2] > 0.5, (SEL_BLOCK, Q_BLOCK))], axis=0)
    k_own = ksel_ref[0, pl.ds(pl.multiple_of(q0, Q_BLOCK), Q_BLOCK), :]
    first = heads_update(init, k_own, _pv_row_tiles([vsel_ref[0, 2 * i], vsel_ref[0, 2 * i + 1]]),
                         valid=chosen & causal)

    def sel_body(t, states):
        ks, vts, biases = [], [], []
        for u in range(SEL_PER_STEP):
            e = t * SEL_PER_STEP + u
            b = ids_ref[slot * ns + e]
            ks.append(ksel_ref[0, pl.ds(pl.multiple_of(b * SEL_BLOCK, SEL_BLOCK), SEL_BLOCK), :])
            row = mask_ref[0, 0, pl.ds(b, 1), :]
            live = (row > 0.5) & (e < count)
            biases.append(jnp.broadcast_to(jnp.where(live, 0.0, NEG), (SEL_BLOCK, Q_BLOCK)))
            vts.append(vsel_ref[0, b])
        return heads_update(states, jnp.concatenate(ks, axis=0), _pv_row_tiles(vts),
                            bias=jnp.concatenate(biases, axis=0))

    sel_states = lax.fori_loop(0, (count + (SEL_PER_STEP - 1)) // SEL_PER_STEP, sel_body, first)

    k_own = kwin_ref[0, pl.ds(pl.multiple_of(q0, Q_BLOCK), Q_BLOCK), :]
    win_states = heads_update(init, k_own, _pv_lane_tiles([vwin_ref[0, i]]), valid=causal)
    n_old = WINDOW // Q_BLOCK
    oldest_in_window = key_off > query_off
    for j0 in range(0, n_old, WIN_TILES_PER_STEP):
        ks, vts, biases = [], [], []
        for j in range(j0, min(j0 + WIN_TILES_PER_STEP, n_old)):
            tile = i - n_old + j
            held = jnp.maximum(tile, 0)
            ks.append(kwin_ref[0, pl.ds(pl.multiple_of(held * Q_BLOCK, Q_BLOCK), Q_BLOCK), :])
            inside = tile >= 0
            live = (oldest_in_window & inside) if j == 0 else inside
            biases.append(jnp.broadcast_to(jnp.where(live, 0.0, NEG), (Q_BLOCK, Q_BLOCK)))
            vts.append(vwin_ref[0, held])
        win_states = heads_update(win_states, jnp.concatenate(ks, axis=0), _pv_lane_tiles(vts),
                                  bias=jnp.concatenate(biases, axis=0))

    def head_out(h):
        def gate(branch):
            return gt_ref[pl.ds(branch * N_HEADS + g * GROUP + h, 1), :]
        _, l_s, acc_s = sel_states[h]
        _, l_w, acc_w = win_states[h]
        return (ocmp_ref[0, 0, :, h * Q_BLOCK:(h + 1) * Q_BLOCK] * gate(0)
                + acc_s / jnp.maximum(l_s, 1e-30) * gate(1) + acc_w / jnp.maximum(l_w, 1e-30) * gate(2))

    for hp in range(GROUP // 2):
        pair = jnp.concatenate([head_out(2 * hp), head_out(2 * hp + 1)], axis=0)
        o_ref[:, hp * LANES:(hp + 1) * LANES] = pair.T.astype(o_ref.dtype)


def _attend(cnt, ids, qt, kaug, vt_blocks_sel, vt_blocks_win, mask, ocmp, gt):
    m = qt.shape[1]
    nqb = m // Q_BLOCK
    ns = m // SEL_BLOCK
    n = GROUP * Q_BLOCK
    gh = GROUP * HEAD_DIM
    grid_spec = pltpu.PrefetchScalarGridSpec(
        num_scalar_prefetch=2,
        grid=(N_KV, nqb),
        in_specs=[pl.BlockSpec((gh, Q_BLOCK), lambda g, i, c, d: (g, i)),
                  pl.BlockSpec((1, m, LANES), lambda g, i, c, d: (g, 0, 0)),
                  pl.BlockSpec((1, ns, SEL_BLOCK, HEAD_DIM), lambda g, i, c, d: (g, 0, 0, 0)),
                  pl.BlockSpec((1, m, LANES), lambda g, i, c, d: (N_KV + g, 0, 0)),
                  pl.BlockSpec((1, nqb, HEAD_DIM, Q_BLOCK), lambda g, i, c, d: (g, 0, 0, 0)),
                  pl.BlockSpec((1, 1, ns, Q_BLOCK), lambda g, i, c, d: (i, g, 0, 0)),
                  pl.BlockSpec((1, 1, HEAD_DIM, n), lambda g, i, c, d: (i, g, 0, 0)),
                  pl.BlockSpec((32, Q_BLOCK), lambda g, i, c, d: (0, i))],
        out_specs=pl.BlockSpec((Q_BLOCK, gh), lambda g, i, c, d: (i, g)),
    )
    return pl.pallas_call(
        functools.partial(_attend_kernel, nqb=nqb, ns=ns),
        grid_spec=grid_spec,
        out_shape=jax.ShapeDtypeStruct((m, D_NSA), BF16),
        compiler_params=_cparams(("arbitrary", "arbitrary")),
        name="nsa_attend",
    )(cnt, ids, qt, kaug, vt_blocks_sel, kaug, vt_blocks_win, mask, ocmp, gt)


def _nsa_prompt(kvc, kaug, qt, vt, gt, pe, w1, w2):
    m = kvc.shape[1]
    nqb = m // Q_BLOCK
    ns = m // SEL_BLOCK
    kc, vct = _compress_prompt(kvc, pe, w1, w2)
    mask, ids8, cnt8, ocmp = _select(qt, kc, vct)
    ids = ids8[:, :, 0, :].transpose(1, 0, 2).reshape(-1)
    cnt = cnt8[:, :, 0, 0].T.astype(I32).reshape(-1)
    vt4 = vt.reshape(4, HEAD_DIM, m)
    vsel = vt4[:N_KV].reshape(N_KV, HEAD_DIM, ns, SEL_BLOCK).transpose(0, 2, 3, 1)
    vwin = vt4[N_KV:].reshape(N_KV, HEAD_DIM, nqb, Q_BLOCK).transpose(0, 2, 1, 3)
    return _attend(cnt, ids, qt, kaug, vsel, vwin, mask, ocmp, gt)


def _mix_kernel(x_ref, onsa_ref, u_ref, v_ref, ga_ref, gb_ref, g1_ref, wa_ref, wb_ref, wo_ref,
                sgw_ref, sgb_ref, o_ref, osg_ref, *, tm, chunked, precise):
    if chunked:
        lane = lax.broadcasted_iota(I32, (CHUNK, LANES), 1)
        for c in range(tm // CHUNK):
            rows = slice(c * CHUNK, (c + 1) * CHUNK)
            for pr in range(N_SG // 2):
                cols = slice(pr * LANES, (pr + 1) * LANES)
                vp = v_ref[rows, cols].astype(BF16)
                a = jnp.dot(sgw_ref[2 * pr], vp, preferred_element_type=F32)
                b = jnp.dot(sgw_ref[2 * pr + 1], vp, preferred_element_type=F32)
                mix = jnp.where(lane < SG_DIM, a, b) + sgb_ref[:, cols]
                osg_ref[rows, cols] = (u_ref[rows, cols].astype(F32) * mix).astype(BF16)
    else:
        mix = v_ref[...] * sgw_ref[...] + sgb_ref[...]
        osg_ref[...] = (u_ref[...].astype(F32) * mix).astype(osg_ref.dtype)
    a = _mm(onsa_ref[...], wa_ref[...], precise)
    b = _mm(osg_ref[...], wb_ref[...], precise)
    merged = ga_ref[...].astype(F32) * a + gb_ref[...].astype(F32) * b
    y = _mm(merged, wo_ref[...], precise)
    o_ref[...] = x_ref[...] + g1_ref[...] * y


def _mix_out(x, onsa, u, v, ga, gb, g1, wa, wb, wo, sgw, sgb, tm, chunked, precise):
    m = x.shape[0]
    mod_rows = g1.shape[0]
    mod_block = (1, D_MODEL) if mod_rows == 1 else (tm, D_MODEL)
    mod_map = (lambda i: (0, 0)) if mod_rows == 1 else (lambda i: (i, 0))
    row = lambda i: (i, 0)
    const2 = lambda i: (0, 0)
    if chunked:
        sg_specs = [pl.BlockSpec((N_SG, CHUNK, CHUNK), lambda i: (0, 0, 0)),
                    pl.BlockSpec((CHUNK, D_SG), const2)]
    else:
        sg_specs = [pl.BlockSpec((1, D_SG), const2), pl.BlockSpec((1, D_SG), const2)]
    return pl.pallas_call(
        functools.partial(_mix_kernel, tm=tm, chunked=chunked, precise=precise),
        grid=(m // tm,),
        in_specs=[pl.BlockSpec((tm, D_MODEL), row),
                  pl.BlockSpec((tm, D_NSA), row),
                  pl.BlockSpec((tm, D_SG), row),
                  pl.BlockSpec((tm, D_SG), row),
                  pl.BlockSpec((tm, D_MODEL), row),
                  pl.BlockSpec((tm, D_MODEL), row),
                  pl.BlockSpec(mod_block, mod_map),
                  pl.BlockSpec((D_NSA, D_MODEL), const2),
                  pl.BlockSpec((D_SG, D_MODEL), const2),
                  pl.BlockSpec((D_MODEL, D_MODEL), const2)] + sg_specs,
        out_specs=pl.BlockSpec((tm, D_MODEL), row),
        out_shape=jax.ShapeDtypeStruct((m, D_MODEL), F32),
        scratch_shapes=[pltpu.VMEM((tm, D_SG), F32 if precise else BF16)],
        compiler_params=_cparams(("arbitrary",)),
        name="mix_out",
    )(x, onsa, u, v, ga, gb, g1, wa, wb, wo, sgw, sgb)


def _top2_combine(logits):
    lane = lax.broadcasted_iota(I32, logits.shape, 1)
    big = jnp.int32(2 ** 30)
    z = jnp.where(lane < N_EXPERTS, logits, -jnp.inf)
    t1 = jnp.max(z, axis=-1, keepdims=True)
    i1 = jnp.min(jnp.where(z == t1, lane, big), axis=-1, keepdims=True)
    z2 = jnp.where(lane == i1, -jnp.inf, z)
    t2 = jnp.max(z2, axis=-1, keepdims=True)
    i2 = jnp.min(jnp.where(z2 == t2, lane, big), axis=-1, keepdims=True)
    e = jnp.exp(t2 - t1)
    den = 1.0 + e
    return jnp.where(lane == i1, 1.0 / den, 0.0) + jnp.where(lane == i2, e / den, 0.0)


FFN_COL_SPLITS = ((0, 6 * LANES), (6 * LANES, D_FF_CHUNK))


def _ffn_kernel(x_ref, g_ref, sc_ref, sh_ref, g2_ref, wa_ref, wb_ref, wd_ref, rw_ref, rb_ref, gf_ref,
                o_ref, h_ref, acc_ref, comb_ref, *, routed, final_norm, n_chunks, precise):
    e = pl.program_id(1)

    @pl.when(e == 0)
    def _():
        x = x_ref[...]
        ms = jnp.mean(x * x, axis=-1, keepdims=True)
        h = x * lax.rsqrt(ms + EPS) * g_ref[...]
        h = h * (1.0 + sc_ref[...]) + sh_ref[...]
        h_ref[...] = h.astype(h_ref.dtype)
        acc_ref[...] = jnp.zeros_like(acc_ref)
        if routed:
            logits = jnp.dot(h, rw_ref[...], precision=HIGHEST, preferred_element_type=F32) + rb_ref[...]
            comb_ref[...] = _top2_combine(logits)

    hb = h_ref[...]
    y = None
    for c0, c1 in FFN_COL_SPLITS:
        a = _mm(hb, wa_ref[0, :, c0:c1], precise)
        b = _mm(hb, wb_ref[0, :, c0:c1], precise)
        part = _mm(_silu(a) * b, wd_ref[0, c0:c1, :], precise)
        y = part if y is None else y + part
    if routed:
        lane = lax.broadcasted_iota(I32, comb_ref.shape, 1)
        w = jnp.sum(jnp.where(lane == e, comb_ref[...], 0.0), axis=-1, keepdims=True)
        acc_ref[...] += w * y
    else:
        acc_ref[...] += y

    @pl.when(e == n_chunks - 1)
    def _():
        out = x_ref[...] + g2_ref[...] * acc_ref[...]
        if final_norm:
            ms = jnp.mean(out * out, axis=-1, keepdims=True)
            out = out * lax.rsqrt(ms + EPS) * gf_ref[...]
        o_ref[...] = out


def _ffn(x, g, sc, sh, g2, gu, wd, rw, rb, gf, tm, routed, final_norm, precise):
    m = x.shape[0]
    n_chunks = wd.shape[0]
    if routed:
        a_map, b_map = (lambda i, e: (e, 0, 0)), (lambda i, e: (e, 0, 1))
    else:
        a_map, b_map = (lambda i, e: (0, 0, e)), (lambda i, e: (0, 0, n_chunks + e))
    mod_rows = sc.shape[0]
    mod_block = (1, D_MODEL) if mod_rows == 1 else (tm, D_MODEL)
    mod_map = (lambda i, e: (0, 0)) if mod_rows == 1 else (lambda i, e: (i, 0))
    row = lambda i, e: (i, 0)
    const2 = lambda i, e: (0, 0)
    return pl.pallas_call(
        functools.partial(_ffn_kernel, routed=routed, final_norm=final_norm, n_chunks=n_chunks,
                          precise=precise),
        grid=(m // tm, n_chunks),
        in_specs=[pl.BlockSpec((tm, D_MODEL), row),
                  pl.BlockSpec((1, D_MODEL), const2),
                  pl.BlockSpec(mod_block, mod_map),
                  pl.BlockSpec(mod_block, mod_map),
                  pl.BlockSpec(mod_block, mod_map),
                  pl.BlockSpec((1, D_MODEL, D_FF_CHUNK), a_map),
                  pl.BlockSpec((1, D_MODEL, D_FF_CHUNK), b_map),
                  pl.BlockSpec((1, D_FF_CHUNK, D_MODEL), lambda i, e: (e, 0, 0)),
                  pl.BlockSpec((D_MODEL, LANES), const2),
                  pl.BlockSpec((1, LANES), const2),
                  pl.BlockSpec((1, D_MODEL), const2)],
        out_specs=pl.BlockSpec((tm, D_MODEL), row),
        out_shape=jax.ShapeDtypeStruct((m, D_MODEL), F32),
        scratch_shapes=[pltpu.VMEM((tm, D_MODEL), F32 if precise else BF16),
                        pltpu.VMEM((tm, D_MODEL), F32),
                        pltpu.VMEM((tm, LANES), F32)],
        compiler_params=_cparams(("arbitrary", "arbitrary")),
        name="ffn",
    )(x, g, sc, sh, g2, gu, gu, wd, rw, rb, gf)


PAGES_PER_STEP = 32
BLOCKS_PER_PAGE = PAGE_SIZE // CMP_BLOCK


def _compress_sample_kernel(pt_ref, cache_ref, pe_ref, w1_ref, w2_ref, kc_ref, vc_ref, pages_ref, sem_ref,
                            rows_ref, x_ref, *, layer, n_steps, total_steps):
    s = pl.program_id(1)
    step = pl.program_id(0) * n_steps + s
    slot = step % 2
    nb = PAGES_PER_STEP * BLOCKS_PER_PAGE
    half = nb // 2

    def page_copy(at_step, at_slot, p):
        page = pt_ref[at_step * PAGES_PER_STEP + p]
        return pltpu.make_async_copy(cache_ref.at[layer, page, pl.ds(0, 2)], pages_ref.at[at_slot, p],
                                     sem_ref.at[at_slot])

    def start_pages(at_step, at_slot):
        for p in range(PAGES_PER_STEP):
            page_copy(at_step, at_slot, p).start(priority=p % 2)

    @pl.when(step == 0)
    def _():
        start_pages(step, slot)

    @pl.when(step + 1 < total_steps)
    def _():
        start_pages(step + 1, 1 - slot)

    for p in range(PAGES_PER_STEP):
        page_copy(step, slot, p).wait()

    for j in range(2):
        for p in range(PAGES_PER_STEP):
            tile = pages_ref[slot, p, j].reshape(N_KV * HEAD_DIM, PAGE_SIZE)
            rows_ref[j, p * PAGE_SIZE:(p + 1) * PAGE_SIZE, :] = tile.T
    for t in range(CMP_BLOCK):
        for j in range(2):
            even = rows_ref[j, pl.ds(t, half, stride=2 * CMP_BLOCK), :]
            odd = rows_ref[j, pl.ds(CMP_BLOCK + t, half, stride=2 * CMP_BLOCK), :]
            xj = jnp.concatenate([even, odd], axis=0) + pe_ref[t, j]
            x_ref[j, :, t * LANES:(t + 1) * LANES] = xj.astype(BF16)
    row = lax.broadcasted_iota(I32, (nb, LANES), 0)
    lane = lax.broadcasted_iota(I32, (nb, LANES), 1)
    c = s * nb + 2 * (row % half) + row // half
    aug = _pos_aug(c * CMP_BLOCK + (CMP_BLOCK - 1), lane)
    hid_k = jnp.dot(x_ref[0], w1_ref[0], preferred_element_type=F32)
    hid_v = jnp.dot(x_ref[1], w1_ref[1], preferred_element_type=F32)
    for g in range(N_KV):
        cols = slice(g * CMP_HIDDEN, (g + 1) * CMP_HIDDEN)
        hk = _silu(hid_k[:, cols]).astype(BF16)
        kc = (jnp.dot(hk, w2_ref[0], preferred_element_type=F32) + aug).astype(BF16)
        kc_ref[0, g, 0] = kc[:half]
        kc_ref[0, g, 1] = kc[half:]
        hv = _silu(hid_v[:, cols]).astype(BF16)
        vc = jnp.dot(hv, w2_ref[1], preferred_element_type=F32).astype(BF16)
        vc_ref[0, g, 0] = vc[:half]
        vc_ref[0, g, 1] = vc[half:]


def _compress_sample(cache_t, layer, page_table, pe, w1, w2):
    n_seq, n_pages = page_table.shape
    n_steps = n_pages // PAGES_PER_STEP
    nb = PAGES_PER_STEP * BLOCKS_PER_PAGE
    half = nb // 2
    nc_half = n_pages * BLOCKS_PER_PAGE // 2
    pe_flat, w1p, w2p = _compress_weights(pe, w1, w2)
    pe4 = pe_flat.reshape(2, CMP_BLOCK, LANES).transpose(1, 0, 2).reshape(CMP_BLOCK, 2, 1, LANES)

    grid_spec = pltpu.PrefetchScalarGridSpec(
        num_scalar_prefetch=1,
        grid=(n_seq, n_steps),
        in_specs=[
            pl.BlockSpec(memory_space=pl.ANY),
            pl.BlockSpec((CMP_BLOCK, 2, 1, LANES), lambda b, s, pt: (0, 0, 0, 0)),
            pl.BlockSpec((2, CMP_BLOCK * LANES, N_KV * CMP_HIDDEN), lambda b, s, pt: (0, 0, 0)),
            pl.BlockSpec((2, CMP_HIDDEN, LANES), lambda b, s, pt: (0, 0, 0))],
        out_specs=(pl.BlockSpec((1, N_KV, 2, half, LANES), lambda b, s, pt: (b, 0, 0, s, 0)),
                   pl.BlockSpec((1, N_KV, 2, half, LANES), lambda b, s, pt: (b, 0, 0, s, 0))),
        scratch_shapes=[pltpu.VMEM((2, PAGES_PER_STEP, 2, N_KV, HEAD_DIM, PAGE_SIZE), F32),
                        pltpu.SemaphoreType.DMA((2,)),
                        pltpu.VMEM((2, PAGES_PER_STEP * PAGE_SIZE, LANES), F32),
                        pltpu.VMEM((2, nb, CMP_BLOCK * LANES), BF16)],
    )
    kc, vc = pl.pallas_call(
        functools.partial(_compress_sample_kernel, layer=layer, n_steps=n_steps, total_steps=n_seq * n_steps),
        grid_spec=grid_spec,
        out_shape=(jax.ShapeDtypeStruct((n_seq, N_KV, 2, nc_half, LANES), BF16),
                   jax.ShapeDtypeStruct((n_seq, N_KV, 2, nc_half, LANES), BF16)),
        compiler_params=_cparams(("arbitrary", "arbitrary")),
        name="compress_sample",
    )(page_table.reshape(-1), cache_t, pe4, w1p, w2p)
    return (kc.reshape(n_seq, N_KV, 2 * nc_half, LANES), vc.reshape(n_seq, N_KV, 2 * nc_half, LANES))


def _row_slopes(shape):
    head = lax.broadcasted_iota(I32, shape, 0)
    out = jnp.zeros(shape, F32)
    for h in range(N_HEADS):
        out = jnp.where(head == h, 2.0 ** (-8.0 * (h + 1) / N_HEADS), out)
    return out


def _sample_select_kernel(q_ref, kc_ref, vc_ref, ocmp_ref, imp_ref, *, nc):
    q = q_ref[0]
    ns = nc // 2
    rowgroup = lax.broadcasted_iota(I32, (N_HEADS, 1), 0) // GROUP
    o = jnp.zeros((N_HEADS, LANES), F32)
    for g in range(N_KV):
        s = _dot_nt(q, kc_ref[0, g])
        m = jnp.max(s, axis=-1, keepdims=True)
        p = jnp.exp(s - m)
        p = p / jnp.maximum(jnp.sum(p, axis=-1, keepdims=True), 1e-30)
        og = jnp.dot(p.astype(BF16), vc_ref[0, g], preferred_element_type=F32)
        mine = rowgroup == g
        o = jnp.where(mine, og, o)
        ph = jnp.sum(jnp.where(mine, p, 0.0), axis=0, keepdims=True)
        imp_ref[0, pl.ds(g, 1), :] = ph[:, :ns] + ph[:, ns:]
    ocmp_ref[0] = o


def _sample_select(q_aug, kc, vc):
    n_seq, _, nc, _ = kc.shape
    ns = nc // 2
    return pl.pallas_call(
        functools.partial(_sample_select_kernel, nc=nc),
        grid=(n_seq,),
        in_specs=[pl.BlockSpec((1, N_HEADS, LANES), lambda b: (b, 0, 0)),
                  pl.BlockSpec((1, N_KV, nc, LANES), lambda b: (b, 0, 0, 0)),
                  pl.BlockSpec((1, N_KV, nc, LANES), lambda b: (b, 0, 0, 0))],
        out_specs=(pl.BlockSpec((1, N_HEADS, LANES), lambda b: (b, 0, 0)),
                   pl.BlockSpec((1, N_KV, ns), lambda b: (b, 0, 0))),
        out_shape=(jax.ShapeDtypeStruct((n_seq, N_HEADS, LANES), F32),
                   jax.ShapeDtypeStruct((n_seq, N_KV, ns), F32)),
        compiler_params=_cparams(("arbitrary",)),
        name="sample_select",
    )(q_aug, kc, vc)


def _sample_topk_kernel(imp_ref, ids_ref, *, ns, k):
    imp = imp_ref[...]
    lane = lax.broadcasted_iota(I32, imp.shape, 1)
    out_lane = lax.broadcasted_iota(I32, ids_ref.shape, 1)
    big = jnp.int32(2 ** 30)
    v = jnp.where((lane == 0) | (lane == ns - 1), FORCED, imp)
    ids = jnp.zeros(ids_ref.shape, I32)
    for t in range(k):
        m = jnp.max(v, axis=-1, keepdims=True)
        first = jnp.min(jnp.where(v == m, lane, big), axis=-1, keepdims=True)
        v = jnp.where(lane == first, REMOVED, v)
        ids = jnp.where(out_lane == t, first, ids)
    ids_ref[...] = ids


def _sample_topk(imp, k):
    rows, ns = imp.shape
    return pl.pallas_call(
        functools.partial(_sample_topk_kernel, ns=ns, k=k),
        out_shape=jax.ShapeDtypeStruct((rows, LANES), I32),
        name="sample_topk",
    )(imp)


SEL_PAST = SEL_TOPK - 1


def _sample_attend_kernel(pt_ref, ids_ref, *refs, past_len):
    del pt_ref
    nblk = N_KV * SEL_PAST
    blocks = refs[:nblk]
    win_ref, new_ref, q_ref, ocmp_ref, gate_ref, o_ref = refs[nblk:]
    b = pl.program_id(0)
    slope = _row_slopes((N_HEADS, 1))
    rowgroup = lax.broadcasted_iota(I32, (N_HEADS, 1), 0) // GROUP
    gates = gate_ref[0]
    w_buf = win_ref.shape[-1]
    q = q_ref[0]
    qf = q.astype(F32)
    lane = lax.broadcasted_iota(I32, (1, PAGE_SIZE), 1)
    o = jnp.zeros((N_HEADS, HEAD_DIM), F32)
    for g in range(N_KV):
        def new_row(j):
            return new_ref[0, j * N_KV + g:j * N_KV + g + 1, :].astype(BF16).astype(F32)

        s_new = jnp.sum(qf * new_row(2), axis=-1, keepdims=True)
        scores, oks = [], []
        for t in range(SEL_PAST):
            blk = ids_ref[(b * N_KV + g) * SEL_TOPK + t]
            kt = blocks[g * SEL_PAST + t][0].astype(BF16)
            in_blk = (lane // SEL_BLOCK) == (blk % 2)
            pos = (blk // 2) * PAGE_SIZE + lane
            s = jnp.dot(q, kt, preferred_element_type=F32) - slope * (past_len - pos).astype(F32)
            scores.append(jnp.where(in_blk, s, NEG))
            oks.append(in_blk)
        m = s_new
        for s in scores:
            m = jnp.maximum(m, jnp.max(s, axis=-1, keepdims=True))
        p_new = jnp.exp(s_new - m)
        den = p_new
        acc = p_new * new_row(3)
        for t in range(SEL_PAST):
            p = jnp.where(oks[t], jnp.exp(scores[t] - m), 0.0)
            den = den + jnp.sum(p, axis=-1, keepdims=True)
            acc = acc + _dot_nt(p, blocks[g * SEL_PAST + t][1])
        o_sel = acc / jnp.maximum(den, 1e-30)
        kwt = win_ref[0, g].astype(BF16)
        dist = w_buf - lax.broadcasted_iota(I32, (1, w_buf), 1)
        ok = dist < WINDOW
        s = jnp.where(ok, jnp.dot(q, kwt, preferred_element_type=F32) - slope * dist.astype(F32), NEG)
        s_new = jnp.sum(qf * new_row(4), axis=-1, keepdims=True)
        m = jnp.maximum(s_new, jnp.max(s, axis=-1, keepdims=True))
        p = jnp.where(ok, jnp.exp(s - m), 0.0)
        p_new = jnp.exp(s_new - m)
        den = p_new + jnp.sum(p, axis=-1, keepdims=True)
        o_win = (p_new * new_row(5) + _dot_nt(p, win_ref[1, g])) / jnp.maximum(den, 1e-30)
        og = gates[:, 0:1] * ocmp_ref[0][:, :HEAD_DIM] + gates[:, 1:2] * o_sel + gates[:, 2:3] * o_win
        o = jnp.where(rowgroup == g, og, o)
    o_ref[0] = o


def _sample_attend(cache_t, win_t, layer, page_table, ids, kv_new, q, ocmp, gates, past_len):
    n_seq, n_pages = page_table.shape
    w_buf = win_t.shape[-1]
    ids3 = ids.reshape(n_seq, N_KV, SEL_TOPK)
    phys = jnp.take_along_axis(page_table[:, None, :], ids3 // 2, axis=-1)
    phys = jnp.pad(phys, ((0, 1), (0, 0), (0, 0))).astype(I32)

    def blk_map(g, t):
        return lambda b, ph, idr: (layer, ph[(b * N_KV + g) * SEL_TOPK + t], 1, g, 0, 0)

    grid_spec = pltpu.PrefetchScalarGridSpec(
        num_scalar_prefetch=2,
        grid=(n_seq,),
        in_specs=[pl.BlockSpec((None, None, 2, None, HEAD_DIM, PAGE_SIZE), blk_map(g, t))
                  for g in range(N_KV) for t in range(SEL_PAST)] + [
            pl.BlockSpec((None, None, 2, N_KV, HEAD_DIM, w_buf), lambda b, pt, idr: (layer, b, 0, 0, 0, 0)),
            pl.BlockSpec((1, 6 * N_KV, HEAD_DIM), lambda b, pt, idr: (b, 0, 0)),
            pl.BlockSpec((1, N_HEADS, HEAD_DIM), lambda b, pt, idr: (b, 0, 0)),
            pl.BlockSpec((1, N_HEADS, LANES), lambda b, pt, idr: (b, 0, 0)),
            pl.BlockSpec((1, N_HEADS, LANES), lambda b, pt, idr: (b, 0, 0))],
        out_specs=pl.BlockSpec((1, N_HEADS, HEAD_DIM), lambda b, pt, idr: (b, 0, 0)),
    )
    return pl.pallas_call(
        functools.partial(_sample_attend_kernel, past_len=past_len),
        grid_spec=grid_spec,
        out_shape=jax.ShapeDtypeStruct((n_seq, N_HEADS, HEAD_DIM), F32),
        compiler_params=_cparams(("arbitrary",)),
        name="sample_attend",
    )(phys.reshape(-1), ids.reshape(-1), *([cache_t] * (N_KV * SEL_PAST)), win_t, kv_new, q, ocmp, gates)


def _nsa_sample(cache_t, win_t, layer, page_table, kv_new, qt, gt, pe, w1, w2):
    n_seq, n_pages = page_table.shape
    past_len = n_pages * PAGE_SIZE
    kc, vc = _compress_sample(cache_t, layer, page_table, pe, w1, w2)
    q = qt.T.reshape(n_seq, N_HEADS, HEAD_DIM).astype(BF16)
    slopes = (2.0 ** (-8.0 * jnp.arange(1, N_HEADS + 1, dtype=F32) / N_HEADS)).reshape(1, N_HEADS, 1)
    aug = jnp.concatenate([slopes * 128.0, slopes, -(slopes * 128.0) * (past_len // 128),
                           jnp.zeros((1, N_HEADS, LANES - HEAD_DIM - 3), F32)], axis=-1)
    q_aug = jnp.concatenate([q, jnp.broadcast_to(aug, (n_seq, N_HEADS, LANES - HEAD_DIM)).astype(BF16)], axis=-1)
    ocmp, imp = _sample_select(q_aug, kc, vc)
    ids = _sample_topk(imp.reshape(n_seq * N_KV, -1), SEL_PAST)[:, :SEL_TOPK]
    gates = gt[:3 * N_HEADS].reshape(3, N_HEADS, n_seq).transpose(2, 1, 0)
    gates = jnp.pad(gates, ((0, 0), (0, 0), (0, LANES - 3)))
    o = _sample_attend(cache_t, win_t, layer, page_table, ids, kv_new.reshape(n_seq, 6 * N_KV, HEAD_DIM), q,
                       ocmp, gates, past_len)
    return o.reshape(n_seq, D_NSA)


def _sg_chunk_params(sg_w, sg_b):
    w = jnp.tril(sg_w).astype(BF16)
    bias = jnp.repeat(sg_b.T, SG_DIM, axis=1)
    return w, bias


def _pad_rows(a, mult):
    pad = (-a.shape[0]) % mult
    return jnp.pad(a, ((0, pad),) + ((0, 0),) * (a.ndim - 1)) if pad else a


def _prep_weights(norm_mix_g, norm_ffn_g, norm_final_g, w_in, cmp_pe, cmp_w1, cmp_w2, sg_norm_g, sg_norm_b,
                  sg_w, sg_b, w_branch_nsa, w_branch_sg, w_out, ffn_w_gu, ffn_w_down, router_w, router_b,
                  moe_w_gu, moe_w_down):
    depth = w_in.shape[0]
    layers = []
    for i in range(depth):
        wn, wt = _pack_w_in(w_in[i])
        sgw_chunk, sgb_chunk = _sg_chunk_params(sg_w[i], sg_b[i])
        lw = {
            "norm_mix": norm_mix_g[i].reshape(1, D_MODEL), "norm_ffn": norm_ffn_g[i].reshape(1, D_MODEL),
            "norm_final": norm_final_g.reshape(1, D_MODEL),
            "sg_norm_g": sg_norm_g[i].reshape(1, D_SG), "sg_norm_b": sg_norm_b[i].reshape(1, D_SG),
            "cmp_pe": cmp_pe[i], "cmp_w1": cmp_w1[i], "cmp_w2": cmp_w2[i],
            "f32": {"wn": wn, "wt": wt, "wa": w_branch_nsa[i], "wb": w_branch_sg[i], "wo": w_out[i]},
            "bf16": {"wn": wn.astype(BF16), "wt": wt.astype(BF16), "wa": w_branch_nsa[i].astype(BF16),
                     "wb": w_branch_sg[i].astype(BF16), "wo": w_out[i].astype(BF16)},
            "sgw_chunk": sgw_chunk, "sgb_chunk": sgb_chunk,
            "sgw_first": jnp.repeat(sg_w[i][:, 0, 0], SG_DIM).reshape(1, D_SG),
            "sgb_first": jnp.repeat(sg_b[i][:, 0], SG_DIM).reshape(1, D_SG),
            "routed": i % 2 == 1, "final": i == depth - 1,
        }
        if i % 2 == 0:
            gu = ffn_w_gu[i // 2][None]
            wd = ffn_w_down[i // 2].reshape(-1, D_FF_CHUNK, D_MODEL)
            lw["router_w"] = jnp.zeros((D_MODEL, LANES), F32)
            lw["router_b"] = jnp.zeros((1, LANES), F32)
        else:
            gu, wd = moe_w_gu[i // 2], moe_w_down[i // 2]
            lw["router_w"] = jnp.pad(router_w[i // 2], ((0, 0), (0, LANES - N_EXPERTS)))
            lw["router_b"] = jnp.pad(router_b[i // 2], (0, LANES - N_EXPERTS)).reshape(1, LANES)
        lw["f32"].update(gu=gu, wd=wd)
        lw["bf16"].update(gu=gu.astype(BF16), wd=wd.astype(BF16))
        layers.append(lw)
    return layers


def _prompt_layer(x, mod, lw, tm):
    sh1, sc1, g1, sh2, sc2, g2 = mod
    w = lw["bf16"]
    kvt, u, v, ga, gb, kaug, qt, vt, gt, kvc = _in_proj(x, lw["norm_mix"], sc1, sh1, w["wn"], w["wt"],
                                                         lw["sg_norm_g"], lw["sg_norm_b"], tm, "cmp", False)
    onsa = _nsa_prompt(kvc, kaug, qt, vt, gt, lw["cmp_pe"], lw["cmp_w1"], lw["cmp_w2"])
    x = _mix_out(x, onsa, u, v, ga, gb, g1, w["wa"], w["wb"], w["wo"], lw["sgw_chunk"], lw["sgb_chunk"],
                 tm, True, False)
    x = _ffn(x, lw["norm_ffn"], sc2, sh2, g2, w["gu"], w["wd"], lw["router_w"], lw["router_b"],
             lw["norm_final"], lw["ffn_tm"], lw["routed"], lw["final"], False)
    return x, kvt


def _sample_layer(x, mod, lw, cache_t, win_t, layer, page_table):
    sh1, sc1, g1, sh2, sc2, g2 = mod
    tm = x.shape[0]
    w = lw["f32"]
    kvt, u, v, ga, gb, _, qt, _, gt, kv = _in_proj(x, lw["norm_mix"], sc1, sh1, w["wn"], w["wt"],
                                                   lw["sg_norm_g"], lw["sg_norm_b"], tm, "all", True)
    onsa = _nsa_sample(cache_t, win_t, layer, page_table, kv, qt, gt, lw["cmp_pe"], lw["cmp_w1"], lw["cmp_w2"])
    x = _mix_out(x, onsa, u, v, ga, gb, g1, w["wa"], w["wb"], w["wo"], lw["sgw_first"], lw["sgb_first"],
                 tm, False, True)
    x = _ffn(x, lw["norm_ffn"], sc2, sh2, g2, w["gu"], w["wd"], lw["router_w"], lw["router_b"],
             lw["norm_final"], tm, lw["routed"], lw["final"], True)
    return x, kv, kvt, v


PROMPT_ROW_TILE = 512
PROMPT_FFN_ROW_TILE = 1024


def kernel(x_prompt, x_sample, cache_kv, state_win, page_table, c_prompt, c_sample, norm_mix_g, norm_ffn_g,
           norm_final_g, w_ada, b_ada, w_in, cmp_pe, cmp_w1, cmp_w2, sg_norm_g, sg_norm_b, sg_w, sg_b,
           w_branch_nsa, w_branch_sg, w_out, ffn_w_gu, ffn_w_down, router_w, router_b, moe_w_gu, moe_w_down):
    batch, seq, _ = x_prompt.shape
    n_seq, dec_seq, _ = x_sample.shape
    depth = w_in.shape[0]
    assert batch == 1 and dec_seq == 1
    assert seq % PROMPT_FFN_ROW_TILE == 0 and seq // Q_BLOCK <= 256
    past_len = page_table.shape[1] * PAGE_SIZE
    assert past_len % CHUNK == 0 and past_len // LANES <= 256
    assert state_win.shape[2] == WINDOW and page_table.shape[1] % PAGES_PER_STEP == 0

    layers = _prep_weights(norm_mix_g, norm_ffn_g, norm_final_g, w_in, cmp_pe, cmp_w1, cmp_w2, sg_norm_g,
                           sg_norm_b, sg_w, sg_b, w_branch_nsa, w_branch_sg, w_out, ffn_w_gu, ffn_w_down,
                           router_w, router_b, moe_w_gu, moe_w_down)
    c_all = _pad_rows(jnp.concatenate([c_prompt, c_sample], axis=0), SUBLANES)
    mods = _ada(c_all, w_ada, b_ada)

    cache_t = cache_kv.transpose(0, 1, 3, 4, 5, 2)
    win_t = state_win.transpose(0, 1, 3, 4, 5, 2)

    xp = x_prompt[0]
    xs = x_sample[:, 0]
    kv_p, kv_s, win_p, win_new, sgv_s = [], [], [], [], []
    w_keep = min(WINDOW, seq)
    kv_rows = 4 * N_KV * HEAD_DIM
    for i in range(depth):
        lw = dict(layers[i])
        lw["ffn_tm"] = PROMPT_FFN_ROW_TILE
        mod_p = tuple(mods[i, 0:1, j * D_MODEL:(j + 1) * D_MODEL] for j in range(6))
        mod_s = tuple(mods[i, 1:1 + n_seq, j * D_MODEL:(j + 1) * D_MODEL] for j in range(6))
        xp, kvt_p = _prompt_layer(xp, mod_p, lw, PROMPT_ROW_TILE)
        xs, kvs, kvt_s, v_s = _sample_layer(xs, mod_s, lw, cache_t, win_t, i, page_table)
        kv_p.append(kvt_p[:kv_rows].reshape(4, N_KV, HEAD_DIM, seq))
        win_p.append(kvt_p[kv_rows:, seq - w_keep:].reshape(2, N_KV, HEAD_DIM, w_keep))
        kv_s.append(kvs[:, :kv_rows].reshape(n_seq, 1, 4, N_KV, HEAD_DIM))
        win_new.append(kvt_s[kv_rows:].reshape(2, N_KV, HEAD_DIM, n_seq).transpose(3, 0, 1, 2)[..., None])
        sgv_s.append(v_s.reshape(n_seq, 1, D_SG))
    kv_prompt = jnp.stack(kv_p).transpose(0, 4, 1, 2, 3)[:, None]
    win_prompt = jnp.stack(win_p).transpose(0, 4, 1, 2, 3)[:, None]
    win_sample = jnp.concatenate([win_t[..., 1:], jnp.stack(win_new)], axis=-1).transpose(0, 1, 5, 2, 3, 4)
    return (xp[None], xs[:, None], kv_prompt, jnp.stack(kv_s), win_prompt, win_sample, jnp.stack(sgv_s))
```

```python
import functools

import jax
import jax.numpy as jnp
from jax import lax
from jax.experimental import pallas as pl
from jax.experimental.pallas import tpu as pltpu

F32 = jnp.float32
BF16 = jnp.bfloat16
I32 = jnp.int32
HIGHEST = lax.Precision.HIGHEST

LANES = 128
SUBLANES = 8
VMEM_LIMIT_BYTES = 56 * 1024 * 1024

D_MODEL = 1024
N_HEADS = 8
HEAD_DIM = 64
N_KV = 2
GROUP = N_HEADS // N_KV
D_NSA = N_HEADS * HEAD_DIM
CMP_BLOCK = 32
CMP_HIDDEN = 256
SEL_BLOCK = 64
SEL_TOPK = 16
WINDOW = 512
Q_BLOCK = 128
N_SG = 8
SG_DIM = 64
D_SG = N_SG * SG_DIM
CHUNK = 128
N_EXPERTS = 8
D_FF_CHUNK = 1408
PAGE_SIZE = 128
EPS = 1e-6
NEG = -1e30
FORCED = 1e9
REMOVED = -3e38

KV_COLS = 6 * N_KV * HEAD_DIM
OFF_KV = D_NSA
OFF_G = OFF_KV + KV_COLS
OFF_U = OFF_G + 3 * N_HEADS
OFF_V = OFF_U + D_SG
OFF_GA = OFF_V + D_SG
OFF_GB = OFF_GA + D_MODEL
IN_COLS = OFF_GB + D_MODEL

WN_U = 0
WN_V = WN_U + D_SG
WN_GA = WN_V + D_SG
WN_GB = WN_GA + D_MODEL
WN_KAUG = WN_GB + D_MODEL
WN_KV = WN_KAUG + 4 * LANES
WN_COLS = WN_KV + KV_COLS
WT_Q = 0
WT_KV = D_NSA
WT_G = WT_KV + KV_COLS
WT_ROWS = WT_G + 32

POS_HI_LANE = HEAD_DIM
POS_LO_LANE = HEAD_DIM + 1
ONE_LANE = HEAD_DIM + 2


def _cparams(sem):
    return pltpu.CompilerParams(dimension_semantics=sem, vmem_limit_bytes=VMEM_LIMIT_BYTES)


def _bdot(a, b):
    return jnp.dot(a.astype(BF16), b.astype(BF16), preferred_element_type=F32)


def _dot_nt(a, b):
    return lax.dot_general(a.astype(BF16), b.astype(BF16), (((1,), (1,)), ((), ())),
                           preferred_element_type=F32)


def _mm(a, b, precise):
    if precise:
        return jnp.dot(a.astype(F32), b.astype(F32), precision=HIGHEST, preferred_element_type=F32)
    return _bdot(a, b)


def _mm_nt(a, b, precise):
    if precise:
        return lax.dot_general(a.astype(F32), b.astype(F32), (((1,), (1,)), ((), ())), precision=HIGHEST,
                               preferred_element_type=F32)
    return _dot_nt(a, b)


def _silu(x):
    return x * jax.nn.sigmoid(x)


def _pos_aug(pos, lane):
    hi = (pos >> 7).astype(F32)
    lo = (pos & 127).astype(F32)
    return jnp.where(lane == POS_HI_LANE, hi,
                     jnp.where(lane == POS_LO_LANE, lo,
                               jnp.where(lane == ONE_LANE, 1.0, 0.0)))


def _group_slopes(g, lane_head):
    out = jnp.zeros(lane_head.shape, F32)
    for gg in range(N_KV):
        for hh in range(GROUP):
            s = 2.0 ** (-8.0 * (gg * GROUP + hh + 1) / N_HEADS)
            out = jnp.where((lane_head == hh) & (g == gg), s, out)
    return out


def _qt_aug(qt_ref, g, q0_blocks):
    heads = [qt_ref[h * HEAD_DIM:(h + 1) * HEAD_DIM, :] for h in range(GROUP)]
    q = jnp.concatenate(heads, axis=1)
    n = GROUP * Q_BLOCK
    row = lax.broadcasted_iota(I32, (LANES - HEAD_DIM, n), 0)
    lane_head = lax.broadcasted_iota(I32, (LANES - HEAD_DIM, n), 1) // Q_BLOCK
    slope = _group_slopes(g, lane_head)
    q0f = (q0_blocks).astype(F32)
    aug = jnp.where(row == 0, slope * 128.0,
                    jnp.where(row == 1, slope,
                              jnp.where(row == 2, -(slope * 128.0) * q0f, 0.0)))
    return jnp.concatenate([q.astype(BF16), aug.astype(BF16)], axis=0)


def _ada_kernel(c_ref, w_ref, b_ref, o_ref):
    c = c_ref[...]
    o_ref[0] = jnp.dot(_silu(c), w_ref[0], precision=HIGHEST, preferred_element_type=F32) + b_ref[0]


def _ada(c_all, w_ada, b_ada):
    depth = w_ada.shape[0]
    rows = c_all.shape[0]
    tn = 1024
    n = w_ada.shape[2]
    return pl.pallas_call(
        _ada_kernel,
        grid=(depth, n // tn),
        in_specs=[pl.BlockSpec((rows, D_MODEL), lambda l, j: (0, 0)),
                  pl.BlockSpec((1, D_MODEL, tn), lambda l, j: (l, 0, j)),
                  pl.BlockSpec((1, 1, tn), lambda l, j: (l, 0, j))],
        out_specs=pl.BlockSpec((1, rows, tn), lambda l, j: (l, 0, j)),
        out_shape=jax.ShapeDtypeStruct((depth, rows, n), F32),
        compiler_params=_cparams(("arbitrary", "arbitrary")),
        name="ada",
    )(c_all, w_ada, b_ada.reshape(depth, 1, n))


def _in_proj_kernel(x_ref, g_ref, sc_ref, sh_ref, wn_ref, wt_ref, lng_ref, lnb_ref, *out_refs, tm, natural_kv,
                    precise):
    kvt_ref, u_ref, v_ref, ga_ref, gb_ref, kaug_ref, qt_ref, vt_ref, gt_ref, kv_ref = out_refs
    i = pl.program_id(0)
    x = x_ref[...]
    ms = jnp.mean(x * x, axis=-1, keepdims=True)
    h = x * lax.rsqrt(ms + EPS) * g_ref[...]
    h = h * (1.0 + sc_ref[...]) + sh_ref[...]
    hb = h if precise else h.astype(BF16)

    def seg(a, b):
        return _mm(hb, wn_ref[:, a:b], precise)

    u_ref[...] = jax.nn.gelu(seg(WN_U, WN_V)).astype(u_ref.dtype)
    v = jax.nn.gelu(seg(WN_V, WN_GA))
    mu = jnp.mean(v, axis=-1, keepdims=True)
    var = jnp.mean(jnp.square(v - mu), axis=-1, keepdims=True)
    v_ref[...] = (v - mu) * lax.rsqrt(var + EPS) * lng_ref[...] + lnb_ref[...]
    ga_ref[...] = jax.nn.sigmoid(seg(WN_GA, WN_GB)).astype(ga_ref.dtype)
    gb_ref[...] = jax.nn.sigmoid(seg(WN_GB, WN_KAUG)).astype(gb_ref.dtype)

    pos = lax.broadcasted_iota(I32, (tm, LANES), 0) + i * tm
    lane = lax.broadcasted_iota(I32, (tm, LANES), 1)
    aug = _pos_aug(pos, lane)
    for j in range(4):
        k = seg(WN_KAUG + j * LANES, WN_KAUG + (j + 1) * LANES)
        kaug_ref[j] = (k + aug).astype(BF16)
    if natural_kv == "all":
        kv_ref[...] = seg(WN_KV, WN_COLS)
    else:
        for j in range(2):
            kv_ref[j] = seg(WN_KV + j * LANES, WN_KV + (j + 1) * LANES)

    zt = _mm_nt(wt_ref[...], hb, precise)
    qt_ref[...] = (zt[WT_Q:WT_KV] * (HEAD_DIM ** -0.5)).astype(qt_ref.dtype)
    kvt = zt[WT_KV:WT_G]
    kvt_ref[...] = kvt
    two = N_KV * HEAD_DIM
    vt_ref[0:two, :] = kvt[3 * two:4 * two].astype(BF16)
    vt_ref[two:2 * two, :] = kvt[5 * two:6 * two].astype(BF16)
    gt_ref[...] = jax.nn.sigmoid(zt[WT_G:WT_ROWS])


def _in_proj(x, g, sc, sh, wn, wt, lng, lnb, tm, natural_kv, precise):
    m = x.shape[0]
    act = F32 if precise else BF16
    mod_rows = sc.shape[0]
    mod_block = (1, D_MODEL) if mod_rows == 1 else (tm, D_MODEL)
    mod_map = (lambda i: (0, 0)) if mod_rows == 1 else (lambda i: (i, 0))
    row = lambda i: (i, 0)
    col = lambda i: (0, i)
    const = lambda i: (0, 0)
    out_shape = [
        jax.ShapeDtypeStruct((KV_COLS, m), F32),
        jax.ShapeDtypeStruct((m, D_SG), act),
        jax.ShapeDtypeStruct((m, D_SG), F32),
        jax.ShapeDtypeStruct((m, D_MODEL), act),
        jax.ShapeDtypeStruct((m, D_MODEL), act),
        jax.ShapeDtypeStruct((4, m, LANES), BF16),
        jax.ShapeDtypeStruct((D_NSA, m), act),
        jax.ShapeDtypeStruct((4 * HEAD_DIM, m), BF16),
        jax.ShapeDtypeStruct((32, m), F32),
    ]
    out_specs = [
        pl.BlockSpec((KV_COLS, tm), col),
        pl.BlockSpec((tm, D_SG), row),
        pl.BlockSpec((tm, D_SG), row),
        pl.BlockSpec((tm, D_MODEL), row),
        pl.BlockSpec((tm, D_MODEL), row),
        pl.BlockSpec((4, tm, LANES), lambda i: (0, i, 0)),
        pl.BlockSpec((D_NSA, tm), col),
        pl.BlockSpec((4 * HEAD_DIM, tm), col),
        pl.BlockSpec((32, tm), col),
    ]
    if natural_kv == "all":
        out_shape.append(jax.ShapeDtypeStruct((m, KV_COLS), F32))
        out_specs.append(pl.BlockSpec((tm, KV_COLS), row))
    else:
        out_shape.append(jax.ShapeDtypeStruct((2, m, LANES), F32))
        out_specs.append(pl.BlockSpec((2, tm, LANES), lambda i: (0, i, 0)))
    return pl.pallas_call(
        functools.partial(_in_proj_kernel, tm=tm, natural_kv=natural_kv, precise=precise),
        grid=(m // tm,),
        in_specs=[pl.BlockSpec((tm, D_MODEL), row),
                  pl.BlockSpec((1, D_MODEL), const),
                  pl.BlockSpec(mod_block, mod_map),
                  pl.BlockSpec(mod_block, mod_map),
                  pl.BlockSpec((D_MODEL, WN_COLS), const),
                  pl.BlockSpec((WT_ROWS, D_MODEL), const),
                  pl.BlockSpec((1, D_SG), const),
                  pl.BlockSpec((1, D_SG), const)],
        out_specs=tuple(out_specs),
        out_shape=tuple(out_shape),
        compiler_params=_cparams(("arbitrary",)),
        name="in_proj",
    )(x, g, sc, sh, wn, wt, lng, lnb)


def _pack_w_in(w):
    kv = w[:, OFF_KV:OFF_G]

    def kvcol(j, g):
        return kv[:, (j * N_KV + g) * HEAD_DIM:(j * N_KV + g + 1) * HEAD_DIM]

    zpad = jnp.zeros((D_MODEL, LANES - HEAD_DIM), w.dtype)
    kaug = [jnp.concatenate([kvcol(j, g), zpad], axis=1) for j in (2, 4) for g in range(N_KV)]
    wn = jnp.concatenate([w[:, OFF_U:OFF_V], w[:, OFF_V:OFF_GA], w[:, OFF_GA:OFF_GB],
                          w[:, OFF_GB:IN_COLS]] + kaug + [kv], axis=1)
    gpad = jnp.zeros((D_MODEL, 32 - 3 * N_HEADS), w.dtype)
    wt = jnp.concatenate([w[:, :OFF_KV], kv, w[:, OFF_G:OFF_U], gpad], axis=1).T
    return wn, wt


def _compress_weights(pe, w1, w2):
    kd = CMP_BLOCK * LANES
    pe_flat = jnp.concatenate([pe, pe], axis=-1).reshape(2, 1, kd)
    w1r = w1.reshape(2, CMP_BLOCK, HEAD_DIM, CMP_HIDDEN)
    w1p = jnp.einsum("gh,jtdn->jtgdhn", jnp.eye(N_KV, dtype=w1.dtype), w1r)
    w1p = w1p.reshape(2, kd, N_KV * CMP_HIDDEN).astype(BF16)
    w2p = jnp.pad(w2, ((0, 0), (0, 0), (0, LANES - HEAD_DIM))).astype(BF16)
    return pe_flat, w1p, w2p


def _compress_kernel(x_ref, pe_ref, w1_ref, w2_ref, w2vt_ref, kc_ref, vct_ref, *, half):
    parity = pl.program_id(0)
    row = lax.broadcasted_iota(I32, (half, LANES), 0)
    lane = lax.broadcasted_iota(I32, (half, LANES), 1)
    aug = _pos_aug((2 * row + parity) * CMP_BLOCK + (CMP_BLOCK - 1), lane)
    hid_k = _bdot(x_ref[0] + pe_ref[0], w1_ref[0])
    hid_v = _bdot(x_ref[1] + pe_ref[1], w1_ref[1])
    for g in range(N_KV):
        cols = slice(g * CMP_HIDDEN, (g + 1) * CMP_HIDDEN)
        kc_ref[g] = (_bdot(_silu(hid_k[:, cols]), w2_ref[0]) + aug).astype(BF16)
        vct_ref[g] = _dot_nt(w2vt_ref[...], _silu(hid_v[:, cols])).astype(BF16)


def _compress_prompt(kvc, pe, w1, w2):
    m = kvc.shape[1]
    nc = m // CMP_BLOCK
    half = nc // 2
    kd = CMP_BLOCK * LANES
    x = kvc.reshape(2, half, 2 * kd)
    pe_flat, w1p, w2p = _compress_weights(pe, w1, w2)
    w2vt = w2[1].T.astype(BF16)
    return pl.pallas_call(
        functools.partial(_compress_kernel, half=half),
        grid=(2,),
        in_specs=[pl.BlockSpec((2, half, kd), lambda p: (0, 0, p)),
                  pl.BlockSpec((2, 1, kd), lambda p: (0, 0, 0)),
                  pl.BlockSpec((2, kd, N_KV * CMP_HIDDEN), lambda p: (0, 0, 0)),
                  pl.BlockSpec((2, CMP_HIDDEN, LANES), lambda p: (0, 0, 0)),
                  pl.BlockSpec((HEAD_DIM, CMP_HIDDEN), lambda p: (0, 0))],
        out_specs=(pl.BlockSpec((N_KV, half, LANES), lambda p: (0, p, 0)),
                   pl.BlockSpec((N_KV, HEAD_DIM, half), lambda p: (0, 0, p))),
        out_shape=(jax.ShapeDtypeStruct((N_KV, nc, LANES), BF16),
                   jax.ShapeDtypeStruct((N_KV, HEAD_DIM, nc), BF16)),
        compiler_params=_cparams(("arbitrary",)),
        name="compress_prompt",
    )(x, pe_flat, w1p, w2p, w2vt)


def _topk_mask_rows(imp, k):
    rows = lax.broadcasted_iota(I32, imp.shape, 0)
    big = jnp.int32(2 ** 30)

    def body(_, v):
        m = jnp.max(v, axis=0, keepdims=True)
        first = jnp.min(jnp.where(v == m, rows, big), axis=0, keepdims=True)
        return jnp.where(rows == first, REMOVED, v)

    return jnp.where(lax.fori_loop(0, k, body, imp) < 0.5 * REMOVED, 1.0, 0.0)


def _visited_blocks(sel, limit):
    ns = sel.shape[0]
    ones_q = jnp.ones((SUBLANES, sel.shape[1]), BF16)
    blk = lax.broadcasted_iota(I32, (SUBLANES, ns), 1)
    flags = ((_dot_nt(ones_q, sel) > 0.5) & (blk < limit)).astype(F32)
    r = lax.broadcasted_iota(I32, (ns, ns), 0)
    c = lax.broadcasted_iota(I32, (ns, ns), 1)
    upper = jnp.where(r <= c, 1.0, 0.0)
    prefix = _bdot(flags, upper)
    before = jnp.where(prefix[0:1, :] <= r.astype(F32), 1.0, 0.0)
    ids = _dot_nt(jnp.ones((SUBLANES, ns), BF16), before)
    ids = jnp.minimum(ids, ns - 1.0).astype(I32)
    total = jnp.broadcast_to(prefix[:, ns - 1:ns], (SUBLANES, ns))
    return ids, total


def _select_tile(qt_ref, kc_ref, vct_ref, mask_ref, ids_ref, cnt_ref, ocmp_ref, i, nc, live):
    ns = nc // 2
    n = GROUP * Q_BLOCK
    q0 = i * Q_BLOCK
    r = lax.broadcasted_iota(I32, (live, 1), 0)
    cend = jnp.concatenate([2 * r, 2 * r + 1], axis=0) * CMP_BLOCK + (CMP_BLOCK - 1)
    qpos = q0 + lax.broadcasted_iota(I32, (1, n), 1) % Q_BLOCK
    valid = cend <= qpos
    blk = lax.broadcasted_iota(I32, (live, Q_BLOCK), 0)
    cur = (q0 + lax.broadcasted_iota(I32, (live, Q_BLOCK), 1)) // SEL_BLOCK
    forced = (blk == 0) | (blk == cur) | (blk == cur - 1)
    imps = []
    for g in range(N_KV):
        qa = _qt_aug(qt_ref.at[g * GROUP * HEAD_DIM:(g + 1) * GROUP * HEAD_DIM], g, i)
        kc = jnp.concatenate([kc_ref[g, 0:live], kc_ref[g, ns:ns + live]], axis=0)
        s = jnp.dot(kc, qa, preferred_element_type=F32)
        s = jnp.where(valid, s, NEG)
        m = jnp.max(s, axis=0, keepdims=True)
        p = jnp.where(valid, jnp.exp(s - m), 0.0)
        p = p / jnp.maximum(jnp.sum(p, axis=0, keepdims=True), 1e-30)
        vct = jnp.concatenate([vct_ref[g, :, 0:live], vct_ref[g, :, ns:ns + live]], axis=1)
        ocmp_ref[0, g] = jnp.dot(vct, p.astype(BF16), preferred_element_type=F32)
        ph = p[:, 0:Q_BLOCK]
        for h in range(1, GROUP):
            ph = ph + p[:, h * Q_BLOCK:(h + 1) * Q_BLOCK]
        imp = ph[:live] + ph[live:]
        imp = jnp.where(forced, FORCED, imp)
        imps.append(jnp.where(blk <= cur, imp, NEG))
    for g in range(N_KV):
        sel = jnp.where(blk <= cur, _topk_mask_rows(imps[g], min(SEL_TOPK, ns)), 0.0)
        if live < ns:
            sel = jnp.concatenate([sel, jnp.zeros((ns - live, Q_BLOCK), F32)], axis=0)
        mask_ref[0, g] = sel
        ids_ref[0, g], cnt_ref[0, g] = _visited_blocks(sel, 2 * i)


def _select_kernel(qt_ref, kc_ref, vct_ref, mask_ref, ids_ref, cnt_ref, ocmp_ref, *, nc):
    i = pl.program_id(0)
    ns = nc // 2
    refs = (qt_ref, kc_ref, vct_ref, mask_ref, ids_ref, cnt_ref, ocmp_ref)
    half = ns // 2
    if half % LANES == 0 and half >= SEL_TOPK:
        early = 2 * i + 1 < half

        @pl.when(early)
        def _():
            _select_tile(*refs, i, nc, half)

        @pl.when(jnp.logical_not(early))
        def _():
            _select_tile(*refs, i, nc, ns)
    else:
        _select_tile(*refs, i, nc, ns)


def _select(qt, kc, vct):
    m = qt.shape[1]
    nqb = m // Q_BLOCK
    nc = m // CMP_BLOCK
    ns = nc // 2
    n = GROUP * Q_BLOCK
    return pl.pallas_call(
        functools.partial(_select_kernel, nc=nc),
        grid=(nqb,),
        in_specs=[pl.BlockSpec((D_NSA, Q_BLOCK), lambda i: (0, i)),
                  pl.BlockSpec((N_KV, nc, LANES), lambda i: (0, 0, 0)),
                  pl.BlockSpec((N_KV, HEAD_DIM, nc), lambda i: (0, 0, 0))],
        out_specs=(pl.BlockSpec((1, N_KV, ns, Q_BLOCK), lambda i: (i, 0, 0, 0)),
                   pl.BlockSpec((1, N_KV, SUBLANES, ns), lambda i: (i, 0, 0, 0)),
                   pl.BlockSpec((1, N_KV, SUBLANES, ns), lambda i: (i, 0, 0, 0)),
                   pl.BlockSpec((1, N_KV, HEAD_DIM, n), lambda i: (i, 0, 0, 0))),
        out_shape=(jax.ShapeDtypeStruct((nqb, N_KV, ns, Q_BLOCK), F32),
                   jax.ShapeDtypeStruct((nqb, N_KV, SUBLANES, ns), I32),
                   jax.ShapeDtypeStruct((nqb, N_KV, SUBLANES, ns), F32),
                   jax.ShapeDtypeStruct((nqb, N_KV, HEAD_DIM, n), F32)),
        compiler_params=_cparams(("arbitrary",)),
        name="nsa_select",
    )(qt, kc, vct)


SEL_PER_STEP = 8
WIN_TILES_PER_STEP = 4


def _softmax_step(state, s, valid=None):
    m, l, _ = state
    if valid is not None:
        s = jnp.where(valid, s, NEG)
    m_new = jnp.maximum(m, jnp.max(s, axis=0, keepdims=True))
    alpha = jnp.exp(m - m_new)
    p = jnp.exp(s - m_new)
    if valid is not None:
        p = jnp.where(valid, p, 0.0)
    return m_new, alpha * l + jnp.sum(p, axis=0, keepdims=True), alpha, p.astype(BF16)


def _pv_lane_tiles(vt_tiles):
    vt = jnp.concatenate(vt_tiles, axis=1)
    return lambda pb: jnp.dot(vt, pb, preferred_element_type=F32)


def _pv_row_tiles(v_tiles):
    v = jnp.concatenate(v_tiles, axis=0)
    return lambda pb: lax.dot_general(v, pb, (((0,), (0,)), ((), ())), preferred_element_type=F32)


def _attend_kernel(cnt_ref, ids_ref, qt_ref, ksel_ref, vsel_ref, kwin_ref, vwin_ref, mask_ref,
                   ocmp_ref, gt_ref, o_ref, *, nqb, ns):
    g = pl.program_id(0)
    i = pl.program_id(1)
    q0 = i * Q_BLOCK
    qa = _qt_aug(qt_ref, g, i)
    qa_heads = [qa[:, h * Q_BLOCK:(h + 1) * Q_BLOCK] for h in range(GROUP)]
    empty = (jnp.full((1, Q_BLOCK), NEG, F32), jnp.zeros((1, Q_BLOCK), F32), jnp.zeros((HEAD_DIM, Q_BLOCK), F32))
    init = (empty,) * GROUP

    def heads_update(states, keys, pv_fn, bias=None, valid=None):
        scores = [jnp.dot(keys, qa_heads[h], preferred_element_type=F32) for h in range(GROUP)]
        if bias is not None:
            scores = [s + bias for s in scores]
        parts = [_softmax_step(states[h], scores[h], valid) for h in range(GROUP)]
        pvs = [pv_fn(p) for _, _, _, p in parts]
        return tuple((m, l, alpha * states[h][2] + pvs[h]) for h, (m, l, alpha, _) in enumerate(parts))

    slot = g * nqb + i
    count = cnt_ref[slot]
    key_off = lax.broadcasted_iota(I32, (Q_BLOCK, 1), 0)
    query_off = lax.broadcasted_iota(I32, (1, Q_BLOCK), 1)
    causal = key_off <= query_off

    own = mask_ref[0, 0, pl.ds(pl.multiple_of(2 * i, 2), 2), :]
    chosen = jnp.concatenate([jnp.broadcast_to(own[0:1] > 0.5, (SEL_BLOCK, Q_BLOCK)),
                              jnp.broadcast_to(own[1:2] > 0.5, (SEL_BLOCK, Q_BLOCK))], axis=0)
    k_own = ksel_ref[0, pl.ds(pl.multiple_of(q0, Q_BLOCK), Q_BLOCK), :]
    first = heads_update(init, k_own, _pv_row_tiles([vsel_ref[0, 2 * i], vsel_ref[0, 2 * i + 1]]),
                         valid=chosen & causal)

    def sel_body(t, states):
        ks, vts, biases = [], [], []
        for u in range(SEL_PER_STEP):
            e = t * SEL_PER_STEP + u
            b = ids_ref[slot * ns + e]
            ks.append(ksel_ref[0, pl.ds(pl.multiple_of(b * SEL_BLOCK, SEL_BLOCK), SEL_BLOCK), :])
            row = mask_ref[0, 0, pl.ds(b, 1), :]
            live = (row > 0.5) & (e < count)
            biases.append(jnp.broadcast_to(jnp.where(live, 0.0, NEG), (SEL_BLOCK, Q_BLOCK)))
            vts.append(vsel_ref[0, b])
        return heads_update(states, jnp.concatenate(ks, axis=0), _pv_row_tiles(vts),
                            bias=jnp.concatenate(biases, axis=0))

    sel_states = lax.fori_loop(0, (count + (SEL_PER_STEP - 1)) // SEL_PER_STEP, sel_body, first)

    k_own = kwin_ref[0, pl.ds(pl.multiple_of(q0, Q_BLOCK), Q_BLOCK), :]
    win_states = heads_update(init, k_own, _pv_lane_tiles([vwin_ref[0, i]]), valid=causal)
    n_old = WINDOW // Q_BLOCK
    oldest_in_window = key_off > query_off
    for j0 in range(0, n_old, WIN_TILES_PER_STEP):
        ks, vts, biases = [], [], []
        for j in range(j0, min(j0 + WIN_TILES_PER_STEP, n_old)):
            tile = i - n_old + j
            held = jnp.maximum(tile, 0)
            ks.append(kwin_ref[0, pl.ds(pl.multiple_of(held * Q_BLOCK, Q_BLOCK), Q_BLOCK), :])
            inside = tile >= 0
            live = (oldest_in_window & inside) if j == 0 else inside
            biases.append(jnp.broadcast_to(jnp.where(live, 0.0, NEG), (Q_BLOCK, Q_BLOCK)))
            vts.append(vwin_ref[0, held])
        win_states = heads_update(win_states, jnp.concatenate(ks, axis=0), _pv_lane_tiles(vts),
                                  bias=jnp.concatenate(biases, axis=0))

    def head_out(h):
        def gate(branch):
            return gt_ref[pl.ds(branch * N_HEADS + g * GROUP + h, 1), :]
        _, l_s, acc_s = sel_states[h]
        _, l_w, acc_w = win_states[h]
        return (ocmp_ref[0, 0, :, h * Q_BLOCK:(h + 1) * Q_BLOCK] * gate(0)
                + acc_s / jnp.maximum(l_s, 1e-30) * gate(1) + acc_w / jnp.maximum(l_w, 1e-30) * gate(2))

    for hp in range(GROUP // 2):
        pair = jnp.concatenate([head_out(2 * hp), head_out(2 * hp + 1)], axis=0)
        o_ref[:, hp * LANES:(hp + 1) * LANES] = pair.T.astype(o_ref.dtype)


def _attend(cnt, ids, qt, kaug, vt_blocks_sel, vt_blocks_win, mask, ocmp, gt):
    m = qt.shape[1]
    nqb = m // Q_BLOCK
    ns = m // SEL_BLOCK
    n = GROUP * Q_BLOCK
    gh = GROUP * HEAD_DIM
    grid_spec = pltpu.PrefetchScalarGridSpec(
        num_scalar_prefetch=2,
        grid=(N_KV, nqb),
        in_specs=[pl.BlockSpec((gh, Q_BLOCK), lambda g, i, c, d: (g, i)),
                  pl.BlockSpec((1, m, LANES), lambda g, i, c, d: (g, 0, 0)),
                  pl.BlockSpec((1, ns, SEL_BLOCK, HEAD_DIM), lambda g, i, c, d: (g, 0, 0, 0)),
                  pl.BlockSpec((1, m, LANES), lambda g, i, c, d: (N_KV + g, 0, 0)),
                  pl.BlockSpec((1, nqb, HEAD_DIM, Q_BLOCK), lambda g, i, c, d: (g, 0, 0, 0)),
                  pl.BlockSpec((1, 1, ns, Q_BLOCK), lambda g, i, c, d: (i, g, 0, 0)),
                  pl.BlockSpec((1, 1, HEAD_DIM, n), lambda g, i, c, d: (i, g, 0, 0)),
                  pl.BlockSpec((32, Q_BLOCK), lambda g, i, c, d: (0, i))],
        out_specs=pl.BlockSpec((Q_BLOCK, gh), lambda g, i, c, d: (i, g)),
    )
    return pl.pallas_call(
        functools.partial(_attend_kernel, nqb=nqb, ns=ns),
        grid_spec=grid_spec,
        out_shape=jax.ShapeDtypeStruct((m, D_NSA), BF16),
        compiler_params=_cparams(("arbitrary", "arbitrary")),
        name="nsa_attend",
    )(cnt, ids, qt, kaug, vt_blocks_sel, kaug, vt_blocks_win, mask, ocmp, gt)


def _nsa_prompt(kvc, kaug, qt, vt, gt, pe, w1, w2):
    m = kvc.shape[1]
    nqb = m // Q_BLOCK
    ns = m // SEL_BLOCK
    kc, vct = _compress_prompt(kvc, pe, w1, w2)
    mask, ids8, cnt8, ocmp = _select(qt, kc, vct)
    ids = ids8[:, :, 0, :].transpose(1, 0, 2).reshape(-1)
    cnt = cnt8[:, :, 0, 0].T.astype(I32).reshape(-1)
    vt4 = vt.reshape(4, HEAD_DIM, m)
    vsel = vt4[:N_KV].reshape(N_KV, HEAD_DIM, ns, SEL_BLOCK).transpose(0, 2, 3, 1)
    vwin = vt4[N_KV:].reshape(N_KV, HEAD_DIM, nqb, Q_BLOCK).transpose(0, 2, 1, 3)
    return _attend(cnt, ids, qt, kaug, vsel, vwin, mask, ocmp, gt)


def _mix_kernel(x_ref, onsa_ref, u_ref, v_ref, ga_ref, gb_ref, g1_ref, wa_ref, wb_ref, wo_ref,
                sgw_ref, sgb_ref, o_ref, osg_ref, *, tm, chunked, precise):
    if chunked:
        lane = lax.broadcasted_iota(I32, (CHUNK, LANES), 1)
        for c in range(tm // CHUNK):
            rows = slice(c * CHUNK, (c + 1) * CHUNK)
            for pr in range(N_SG // 2):
                cols = slice(pr * LANES, (pr + 1) * LANES)
                vp = v_ref[rows, cols].astype(BF16)
                a = jnp.dot(sgw_ref[2 * pr], vp, preferred_element_type=F32)
                b = jnp.dot(sgw_ref[2 * pr + 1], vp, preferred_element_type=F32)
                mix = jnp.where(lane < SG_DIM, a, b) + sgb_ref[:, cols]
                osg_ref[rows, cols] = (u_ref[rows, cols].astype(F32) * mix).astype(BF16)
    else:
        mix = v_ref[...] * sgw_ref[...] + sgb_ref[...]
        osg_ref[...] = (u_ref[...].astype(F32) * mix).astype(osg_ref.dtype)
    a = _mm(onsa_ref[...], wa_ref[...], precise)
    b = _mm(osg_ref[...], wb_ref[...], precise)
    merged = ga_ref[...].astype(F32) * a + gb_ref[...].astype(F32) * b
    y = _mm(merged, wo_ref[...], precise)
    o_ref[...] = x_ref[...] + g1_ref[...] * y


def _mix_out(x, onsa, u, v, ga, gb, g1, wa, wb, wo, sgw, sgb, tm, chunked, precise):
    m = x.shape[0]
    mod_rows = g1.shape[0]
    mod_block = (1, D_MODEL) if mod_rows == 1 else (tm, D_MODEL)
    mod_map = (lambda i: (0, 0)) if mod_rows == 1 else (lambda i: (i, 0))
    row = lambda i: (i, 0)
    const2 = lambda i: (0, 0)
    if chunked:
        sg_specs = [pl.BlockSpec((N_SG, CHUNK, CHUNK), lambda i: (0, 0, 0)),
                    pl.BlockSpec((CHUNK, D_SG), const2)]
    else:
        sg_specs = [pl.BlockSpec((1, D_SG), const2), pl.BlockSpec((1, D_SG), const2)]
    return pl.pallas_call(
        functools.partial(_mix_kernel, tm=tm, chunked=chunked, precise=precise),
        grid=(m // tm,),
        in_specs=[pl.BlockSpec((tm, D_MODEL), row),
                  pl.BlockSpec((tm, D_NSA), row),
                  pl.BlockSpec((tm, D_SG), row),
                  pl.BlockSpec((tm, D_SG), row),
                  pl.BlockSpec((tm, D_MODEL), row),
                  pl.BlockSpec((tm, D_MODEL), row),
                  pl.BlockSpec(mod_block, mod_map),
                  pl.BlockSpec((D_NSA, D_MODEL), const2),
                  pl.BlockSpec((D_SG, D_MODEL), const2),
                  pl.BlockSpec((D_MODEL, D_MODEL), const2)] + sg_specs,
        out_specs=pl.BlockSpec((tm, D_MODEL), row),
        out_shape=jax.ShapeDtypeStruct((m, D_MODEL), F32),
        scratch_shapes=[pltpu.VMEM((tm, D_SG), F32 if precise else BF16)],
        compiler_params=_cparams(("arbitrary",)),
        name="mix_out",
    )(x, onsa, u, v, ga, gb, g1, wa, wb, wo, sgw, sgb)


def _top2_combine(logits):
    lane = lax.broadcasted_iota(I32, logits.shape, 1)
    big = jnp.int32(2 ** 30)
    z = jnp.where(lane < N_EXPERTS, logits, -jnp.inf)
    t1 = jnp.max(z, axis=-1, keepdims=True)
    i1 = jnp.min(jnp.where(z == t1, lane, big), axis=-1, keepdims=True)
    z2 = jnp.where(lane == i1, -jnp.inf, z)
    t2 = jnp.max(z2, axis=-1, keepdims=True)
    i2 = jnp.min(jnp.where(z2 == t2, lane, big), axis=-1, keepdims=True)
    e = jnp.exp(t2 - t1)
    den = 1.0 + e
    return jnp.where(lane == i1, 1.0 / den, 0.0) + jnp.where(lane == i2, e / den, 0.0)


FFN_COL_SPLITS = ((0, 6 * LANES), (6 * LANES, D_FF_CHUNK))


def _ffn_kernel(x_ref, g_ref, sc_ref, sh_ref, g2_ref, wa_ref, wb_ref, wd_ref, rw_ref, rb_ref, gf_ref,
                o_ref, h_ref, acc_ref, comb_ref, *, routed, final_norm, n_chunks, precise):
    e = pl.program_id(1)

    @pl.when(e == 0)
    def _():
        x = x_ref[...]
        ms = jnp.mean(x * x, axis=-1, keepdims=True)
        h = x * lax.rsqrt(ms + EPS) * g_ref[...]
        h = h * (1.0 + sc_ref[...]) + sh_ref[...]
        h_ref[...] = h.astype(h_ref.dtype)
        acc_ref[...] = jnp.zeros_like(acc_ref)
        if routed:
            logits = jnp.dot(h, rw_ref[...], precision=HIGHEST, preferred_element_type=F32) + rb_ref[...]
            comb_ref[...] = _top2_combine(logits)

    hb = h_ref[...]
    y = None
    for c0, c1 in FFN_COL_SPLITS:
        a = _mm(hb, wa_ref[0, :, c0:c1], precise)
        b = _mm(hb, wb_ref[0, :, c0:c1], precise)
        part = _mm(_silu(a) * b, wd_ref[0, c0:c1, :], precise)
        y = part if y is None else y + part
    if routed:
        lane = lax.broadcasted_iota(I32, comb_ref.shape, 1)
        w = jnp.sum(jnp.where(lane == e, comb_ref[...], 0.0), axis=-1, keepdims=True)
        acc_ref[...] += w * y
    else:
        acc_ref[...] += y

    @pl.when(e == n_chunks - 1)
    def _():
        out = x_ref[...] + g2_ref[...] * acc_ref[...]
        if final_norm:
            ms = jnp.mean(out * out, axis=-1, keepdims=True)
            out = out * lax.rsqrt(ms + EPS) * gf_ref[...]
        o_ref[...] = out


def _ffn(x, g, sc, sh, g2, gu, wd, rw, rb, gf, tm, routed, final_norm, precise):
    m = x.shape[0]
    n_chunks = wd.shape[0]
    if routed:
        a_map, b_map = (lambda i, e: (e, 0, 0)), (lambda i, e: (e, 0, 1))
    else:
        a_map, b_map = (lambda i, e: (0, 0, e)), (lambda i, e: (0, 0, n_chunks + e))
    mod_rows = sc.shape[0]
    mod_block = (1, D_MODEL) if mod_rows == 1 else (tm, D_MODEL)
    mod_map = (lambda i, e: (0, 0)) if mod_rows == 1 else (lambda i, e: (i, 0))
    row = lambda i, e: (i, 0)
    const2 = lambda i, e: (0, 0)
    return pl.pallas_call(
        functools.partial(_ffn_kernel, routed=routed, final_norm=final_norm, n_chunks=n_chunks,
                          precise=precise),
        grid=(m // tm, n_chunks),
        in_specs=[pl.BlockSpec((tm, D_MODEL), row),
                  pl.BlockSpec((1, D_MODEL), const2),
                  pl.BlockSpec(mod_block, mod_map),
                  pl.BlockSpec(mod_block, mod_map),
                  pl.BlockSpec(mod_block, mod_map),
                  pl.BlockSpec((1, D_MODEL, D_FF_CHUNK), a_map),
                  pl.BlockSpec((1, D_MODEL, D_FF_CHUNK), b_map),
                  pl.BlockSpec((1, D_FF_CHUNK, D_MODEL), lambda i, e: (e, 0, 0)),
                  pl.BlockSpec((D_MODEL, LANES), const2),
                  pl.BlockSpec((1, LANES), const2),
                  pl.BlockSpec((1, D_MODEL), const2)],
        out_specs=pl.BlockSpec((tm, D_MODEL), row),
        out_shape=jax.ShapeDtypeStruct((m, D_MODEL), F32),
        scratch_shapes=[pltpu.VMEM((tm, D_MODEL), F32 if precise else BF16),
                        pltpu.VMEM((tm, D_MODEL), F32),
                        pltpu.VMEM((tm, LANES), F32)],
        compiler_params=_cparams(("arbitrary", "arbitrary")),
        name="ffn",
    )(x, g, sc, sh, g2, gu, gu, wd, rw, rb, gf)


PAGES_PER_STEP = 32
BLOCKS_PER_PAGE = PAGE_SIZE // CMP_BLOCK


def _compress_sample_kernel(pt_ref, cache_ref, pe_ref, w1_ref, w2_ref, kc_ref, vc_ref, pages_ref, sem_ref,
                            rows_ref, x_ref, *, layer, n_steps, total_steps):
    s = pl.program_id(1)
    step = pl.program_id(0) * n_steps + s
    slot = step % 2
    nb = PAGES_PER_STEP * BLOCKS_PER_PAGE
    half = nb // 2

    def page_copy(at_step, at_slot, p):
        page = pt_ref[at_step * PAGES_PER_STEP + p]
        return pltpu.make_async_copy(cache_ref.at[layer, page, pl.ds(0, 2)], pages_ref.at[at_slot, p],
                                     sem_ref.at[at_slot])

    def start_pages(at_step, at_slot):
        for p in range(PAGES_PER_STEP):
            page_copy(at_step, at_slot, p).start(priority=p % 2)

    @pl.when(step == 0)
    def _():
        start_pages(step, slot)

    @pl.when(step + 1 < total_steps)
    def _():
        start_pages(step + 1, 1 - slot)

    for p in range(PAGES_PER_STEP):
        page_copy(step, slot, p).wait()

    eye = (lax.broadcasted_iota(I32, (PAGE_SIZE, PAGE_SIZE), 0)
           == lax.broadcasted_iota(I32, (PAGE_SIZE, PAGE_SIZE), 1)).astype(BF16)
    for j in range(2):
        for p in range(PAGES_PER_STEP):
            tile = pages_ref[slot, p, j].reshape(N_KV * HEAD_DIM, PAGE_SIZE) + pe_ref[j]
            rows_ref[j, p * PAGE_SIZE:(p + 1) * PAGE_SIZE, :] = _dot_nt(eye, tile)
    for t in range(CMP_BLOCK):
        for j in range(2):
            even = rows_ref[j, pl.ds(t, half, stride=2 * CMP_BLOCK), :]
            odd = rows_ref[j, pl.ds(CMP_BLOCK + t, half, stride=2 * CMP_BLOCK), :]
            x_ref[j, :, t * LANES:(t + 1) * LANES] = jnp.concatenate([even, odd], axis=0).astype(BF16)
    row = lax.broadcasted_iota(I32, (nb, LANES), 0)
    lane = lax.broadcasted_iota(I32, (nb, LANES), 1)
    c = s * nb + 2 * (row % half) + row // half
    aug = _pos_aug(c * CMP_BLOCK + (CMP_BLOCK - 1), lane)
    hid_k = jnp.dot(x_ref[0], w1_ref[0], preferred_element_type=F32)
    hid_v = jnp.dot(x_ref[1], w1_ref[1], preferred_element_type=F32)
    for g in range(N_KV):
        cols = slice(g * CMP_HIDDEN, (g + 1) * CMP_HIDDEN)
        hk = _silu(hid_k[:, cols]).astype(BF16)
        kc = (jnp.dot(hk, w2_ref[0], preferred_element_type=F32) + aug).astype(BF16)
        kc_ref[0, g, 0] = kc[:half]
        kc_ref[0, g, 1] = kc[half:]
        hv = _silu(hid_v[:, cols]).astype(BF16)
        vc = jnp.dot(hv, w2_ref[1], preferred_element_type=F32).astype(BF16)
        vc_ref[0, g, 0] = vc[:half]
        vc_ref[0, g, 1] = vc[half:]


def _compress_sample(cache_t, layer, page_table, pe, w1, w2):
    n_seq, n_pages = page_table.shape
    n_steps = n_pages // PAGES_PER_STEP
    nb = PAGES_PER_STEP * BLOCKS_PER_PAGE
    half = nb // 2
    nc_half = n_pages * BLOCKS_PER_PAGE // 2
    pe_flat, w1p, w2p = _compress_weights(pe, w1, w2)
    pe_t = jnp.tile(pe.transpose(0, 2, 1), (1, N_KV, BLOCKS_PER_PAGE))

    grid_spec = pltpu.PrefetchScalarGridSpec(
        num_scalar_prefetch=1,
        grid=(n_seq, n_steps),
        in_specs=[
            pl.BlockSpec(memory_space=pl.ANY),
            pl.BlockSpec((2, N_KV * HEAD_DIM, PAGE_SIZE), lambda b, s, pt: (0, 0, 0)),
            pl.BlockSpec((2, CMP_BLOCK * LANES, N_KV * CMP_HIDDEN), lambda b, s, pt: (0, 0, 0)),
            pl.BlockSpec((2, CMP_HIDDEN, LANES), lambda b, s, pt: (0, 0, 0))],
        out_specs=(pl.BlockSpec((1, N_KV, 2, half, LANES), lambda b, s, pt: (b, 0, 0, s, 0)),
                   pl.BlockSpec((1, N_KV, 2, half, LANES), lambda b, s, pt: (b, 0, 0, s, 0))),
        scratch_shapes=[pltpu.VMEM((2, PAGES_PER_STEP, 2, N_KV, HEAD_DIM, PAGE_SIZE), F32),
                        pltpu.SemaphoreType.DMA((2,)),
                        pltpu.VMEM((2, PAGES_PER_STEP * PAGE_SIZE, LANES), F32),
                        pltpu.VMEM((2, nb, CMP_BLOCK * LANES), BF16)],
    )
    kc, vc = pl.pallas_call(
        functools.partial(_compress_sample_kernel, layer=layer, n_steps=n_steps, total_steps=n_seq * n_steps),
        grid_spec=grid_spec,
        out_shape=(jax.ShapeDtypeStruct((n_seq, N_KV, 2, nc_half, LANES), BF16),
                   jax.ShapeDtypeStruct((n_seq, N_KV, 2, nc_half, LANES), BF16)),
        compiler_params=_cparams(("arbitrary", "arbitrary")),
        name="compress_sample",
    )(page_table.reshape(-1), cache_t, pe_t, w1p, w2p)
    return (kc.reshape(n_seq, N_KV, 2 * nc_half, LANES), vc.reshape(n_seq, N_KV, 2 * nc_half, LANES))


def _row_slopes(shape):
    head = lax.broadcasted_iota(I32, shape, 0)
    out = jnp.zeros(shape, F32)
    for h in range(N_HEADS):
        out = jnp.where(head == h, 2.0 ** (-8.0 * (h + 1) / N_HEADS), out)
    return out


def _sample_select_kernel(q_ref, kc_ref, vc_ref, ocmp_ref, imp_ref, *, nc):
    q = q_ref[0]
    ns = nc // 2
    rowgroup = lax.broadcasted_iota(I32, (N_HEADS, 1), 0) // GROUP
    o = jnp.zeros((N_HEADS, LANES), F32)
    for g in range(N_KV):
        s = _dot_nt(q, kc_ref[0, g])
        m = jnp.max(s, axis=-1, keepdims=True)
        p = jnp.exp(s - m)
        p = p / jnp.maximum(jnp.sum(p, axis=-1, keepdims=True), 1e-30)
        og = jnp.dot(p.astype(BF16), vc_ref[0, g], preferred_element_type=F32)
        mine = rowgroup == g
        o = jnp.where(mine, og, o)
        ph = jnp.sum(jnp.where(mine, p, 0.0), axis=0, keepdims=True)
        imp_ref[0, pl.ds(g, 1), :] = ph[:, :ns] + ph[:, ns:]
    ocmp_ref[0] = o


def _sample_select(q_aug, kc, vc):
    n_seq, _, nc, _ = kc.shape
    ns = nc // 2
    return pl.pallas_call(
        functools.partial(_sample_select_kernel, nc=nc),
        grid=(n_seq,),
        in_specs=[pl.BlockSpec((1, N_HEADS, LANES), lambda b: (b, 0, 0)),
                  pl.BlockSpec((1, N_KV, nc, LANES), lambda b: (b, 0, 0, 0)),
                  pl.BlockSpec((1, N_KV, nc, LANES), lambda b: (b, 0, 0, 0))],
        out_specs=(pl.BlockSpec((1, N_HEADS, LANES), lambda b: (b, 0, 0)),
                   pl.BlockSpec((1, N_KV, ns), lambda b: (b, 0, 0))),
        out_shape=(jax.ShapeDtypeStruct((n_seq, N_HEADS, LANES), F32),
                   jax.ShapeDtypeStruct((n_seq, N_KV, ns), F32)),
        compiler_params=_cparams(("arbitrary",)),
        name="sample_select",
    )(q_aug, kc, vc)


def _sample_topk_kernel(imp_ref, ids_ref, *, ns, k):
    imp = imp_ref[...]
    lane = lax.broadcasted_iota(I32, imp.shape, 1)
    out_lane = lax.broadcasted_iota(I32, ids_ref.shape, 1)
    big = jnp.int32(2 ** 30)
    v = jnp.where((lane == 0) | (lane == ns - 1), FORCED, imp)
    ids = jnp.zeros(ids_ref.shape, I32)
    for t in range(k):
        m = jnp.max(v, axis=-1, keepdims=True)
        first = jnp.min(jnp.where(v == m, lane, big), axis=-1, keepdims=True)
        v = jnp.where(lane == first, REMOVED, v)
        ids = jnp.where(out_lane == t, first, ids)
    ids_ref[...] = ids


def _sample_topk(imp, k):
    rows, ns = imp.shape
    return pl.pallas_call(
        functools.partial(_sample_topk_kernel, ns=ns, k=k),
        out_shape=jax.ShapeDtypeStruct((rows, LANES), I32),
        name="sample_topk",
    )(imp)


SEL_PAST = SEL_TOPK - 1


def _sample_attend_kernel(pt_ref, ids_ref, *refs, past_len):
    del pt_ref
    nblk = N_KV * SEL_PAST
    blocks = refs[:nblk]
    win_ref, new_ref, q_ref, ocmp_ref, gate_ref, o_ref = refs[nblk:]
    b = pl.program_id(0)
    slope = _row_slopes((N_HEADS, 1))
    rowgroup = lax.broadcasted_iota(I32, (N_HEADS, 1), 0) // GROUP
    gates = gate_ref[0]
    w_buf = win_ref.shape[-1]
    q = q_ref[0]
    qf = q.astype(F32)
    lane = lax.broadcasted_iota(I32, (1, PAGE_SIZE), 1)
    o = jnp.zeros((N_HEADS, HEAD_DIM), F32)
    for g in range(N_KV):
        def new_row(j):
            return new_ref[0, j * N_KV + g:j * N_KV + g + 1, :].astype(BF16).astype(F32)

        s_new = jnp.sum(qf * new_row(2), axis=-1, keepdims=True)
        scores, oks = [], []
        for t in range(SEL_PAST):
            blk = ids_ref[(b * N_KV + g) * SEL_TOPK + t]
            kt = blocks[g * SEL_PAST + t][0].astype(BF16)
            in_blk = (lane // SEL_BLOCK) == (blk % 2)
            pos = (blk // 2) * PAGE_SIZE + lane
            s = jnp.dot(q, kt, preferred_element_type=F32) - slope * (past_len - pos).astype(F32)
            scores.append(jnp.where(in_blk, s, NEG))
            oks.append(in_blk)
        m = s_new
        for s in scores:
            m = jnp.maximum(m, jnp.max(s, axis=-1, keepdims=True))
        p_new = jnp.exp(s_new - m)
        den = p_new
        acc = p_new * new_row(3)
        for t in range(SEL_PAST):
            p = jnp.where(oks[t], jnp.exp(scores[t] - m), 0.0)
            den = den + jnp.sum(p, axis=-1, keepdims=True)
            acc = acc + _dot_nt(p, blocks[g * SEL_PAST + t][1])
        o_sel = acc / jnp.maximum(den, 1e-30)
        kwt = win_ref[0, g].astype(BF16)
        dist = w_buf - lax.broadcasted_iota(I32, (1, w_buf), 1)
        ok = dist < WINDOW
        s = jnp.where(ok, jnp.dot(q, kwt, preferred_element_type=F32) - slope * dist.astype(F32), NEG)
        s_new = jnp.sum(qf * new_row(4), axis=-1, keepdims=True)
        m = jnp.maximum(s_new, jnp.max(s, axis=-1, keepdims=True))
        p = jnp.where(ok, jnp.exp(s - m), 0.0)
        p_new = jnp.exp(s_new - m)
        den = p_new + jnp.sum(p, axis=-1, keepdims=True)
        o_win = (p_new * new_row(5) + _dot_nt(p, win_ref[1, g])) / jnp.maximum(den, 1e-30)
        og = gates[:, 0:1] * ocmp_ref[0][:, :HEAD_DIM] + gates[:, 1:2] * o_sel + gates[:, 2:3] * o_win
        o = jnp.where(rowgroup == g, og, o)
    o_ref[0] = o


def _sample_attend(cache_t, win_t, layer, page_table, ids, kv_new, q, ocmp, gates, past_len):
    n_seq, n_pages = page_table.shape
    w_buf = win_t.shape[-1]
    ids3 = ids.reshape(n_seq, N_KV, SEL_TOPK)
    phys = jnp.take_along_axis(page_table[:, None, :], ids3 // 2, axis=-1)
    phys = jnp.pad(phys, ((0, 1), (0, 0), (0, 0))).astype(I32)

    def blk_map(g, t):
        return lambda b, ph, idr: (layer, ph[(b * N_KV + g) * SEL_TOPK + t], 1, g, 0, 0)

    grid_spec = pltpu.PrefetchScalarGridSpec(
        num_scalar_prefetch=2,
        grid=(n_seq,),
        in_specs=[pl.BlockSpec((None, None, 2, None, HEAD_DIM, PAGE_SIZE), blk_map(g, t))
                  for g in range(N_KV) for t in range(SEL_PAST)] + [
            pl.BlockSpec((None, None, 2, N_KV, HEAD_DIM, w_buf), lambda b, pt, idr: (layer, b, 0, 0, 0, 0)),
            pl.BlockSpec((1, 6 * N_KV, HEAD_DIM), lambda b, pt, idr: (b, 0, 0)),
            pl.BlockSpec((1, N_HEADS, HEAD_DIM), lambda b, pt, idr: (b, 0, 0)),
            pl.BlockSpec((1, N_HEADS, LANES), lambda b, pt, idr: (b, 0, 0)),
            pl.BlockSpec((1, N_HEADS, LANES), lambda b, pt, idr: (b, 0, 0))],
        out_specs=pl.BlockSpec((1, N_HEADS, HEAD_DIM), lambda b, pt, idr: (b, 0, 0)),
    )
    return pl.pallas_call(
        functools.partial(_sample_attend_kernel, past_len=past_len),
        grid_spec=grid_spec,
        out_shape=jax.ShapeDtypeStruct((n_seq, N_HEADS, HEAD_DIM), F32),
        compiler_params=_cparams(("arbitrary",)),
        name="sample_attend",
    )(phys.reshape(-1), ids.reshape(-1), *([cache_t] * (N_KV * SEL_PAST)), win_t, kv_new, q, ocmp, gates)


def _nsa_sample(cache_t, win_t, layer, page_table, kv_new, qt, gt, pe, w1, w2):
    n_seq, n_pages = page_table.shape
    past_len = n_pages * PAGE_SIZE
    kc, vc = _compress_sample(cache_t, layer, page_table, pe, w1, w2)
    q = qt.T.reshape(n_seq, N_HEADS, HEAD_DIM).astype(BF16)
    slopes = (2.0 ** (-8.0 * jnp.arange(1, N_HEADS + 1, dtype=F32) / N_HEADS)).reshape(1, N_HEADS, 1)
    aug = jnp.concatenate([slopes * 128.0, slopes, -(slopes * 128.0) * (past_len // 128),
                           jnp.zeros((1, N_HEADS, LANES - HEAD_DIM - 3), F32)], axis=-1)
    q_aug = jnp.concatenate([q, jnp.broadcast_to(aug, (n_seq, N_HEADS, LANES - HEAD_DIM)).astype(BF16)], axis=-1)
    ocmp, imp = _sample_select(q_aug, kc, vc)
    ids = _sample_topk(imp.reshape(n_seq * N_KV, -1), SEL_PAST)[:, :SEL_TOPK]
    gates = gt[:3 * N_HEADS].reshape(3, N_HEADS, n_seq).transpose(2, 1, 0)
    gates = jnp.pad(gates, ((0, 0), (0, 0), (0, LANES - 3)))
    o = _sample_attend(cache_t, win_t, layer, page_table, ids, kv_new.reshape(n_seq, 6 * N_KV, HEAD_DIM), q,
                       ocmp, gates, past_len)
    return o.reshape(n_seq, D_NSA)


def _sg_chunk_params(sg_w, sg_b):
    w = jnp.tril(sg_w).astype(BF16)
    bias = jnp.repeat(sg_b.T, SG_DIM, axis=1)
    return w, bias


def _pad_rows(a, mult):
    pad = (-a.shape[0]) % mult
    return jnp.pad(a, ((0, pad),) + ((0, 0),) * (a.ndim - 1)) if pad else a


def _prep_weights(norm_mix_g, norm_ffn_g, norm_final_g, w_in, cmp_pe, cmp_w1, cmp_w2, sg_norm_g, sg_norm_b,
                  sg_w, sg_b, w_branch_nsa, w_branch_sg, w_out, ffn_w_gu, ffn_w_down, router_w, router_b,
                  moe_w_gu, moe_w_down):
    depth = w_in.shape[0]
    layers = []
    for i in range(depth):
        wn, wt = _pack_w_in(w_in[i])
        sgw_chunk, sgb_chunk = _sg_chunk_params(sg_w[i], sg_b[i])
        lw = {
            "norm_mix": norm_mix_g[i].reshape(1, D_MODEL), "norm_ffn": norm_ffn_g[i].reshape(1, D_MODEL),
            "norm_final": norm_final_g.reshape(1, D_MODEL),
            "sg_norm_g": sg_norm_g[i].reshape(1, D_SG), "sg_norm_b": sg_norm_b[i].reshape(1, D_SG),
            "cmp_pe": cmp_pe[i], "cmp_w1": cmp_w1[i], "cmp_w2": cmp_w2[i],
            "f32": {"wn": wn, "wt": wt, "wa": w_branch_nsa[i], "wb": w_branch_sg[i], "wo": w_out[i]},
            "bf16": {"wn": wn.astype(BF16), "wt": wt.astype(BF16), "wa": w_branch_nsa[i].astype(BF16),
                     "wb": w_branch_sg[i].astype(BF16), "wo": w_out[i].astype(BF16)},
            "sgw_chunk": sgw_chunk, "sgb_chunk": sgb_chunk,
            "sgw_first": jnp.repeat(sg_w[i][:, 0, 0], SG_DIM).reshape(1, D_SG),
            "sgb_first": jnp.repeat(sg_b[i][:, 0], SG_DIM).reshape(1, D_SG),
            "routed": i % 2 == 1, "final": i == depth - 1,
        }
        if i % 2 == 0:
            gu = ffn_w_gu[i // 2][None]
            wd = ffn_w_down[i // 2].reshape(-1, D_FF_CHUNK, D_MODEL)
            lw["router_w"] = jnp.zeros((D_MODEL, LANES), F32)
            lw["router_b"] = jnp.zeros((1, LANES), F32)
        else:
            gu, wd = moe_w_gu[i // 2], moe_w_down[i // 2]
            lw["router_w"] = jnp.pad(router_w[i // 2], ((0, 0), (0, LANES - N_EXPERTS)))
            lw["router_b"] = jnp.pad(router_b[i // 2], (0, LANES - N_EXPERTS)).reshape(1, LANES)
        lw["f32"].update(gu=gu, wd=wd)
        lw["bf16"].update(gu=gu.astype(BF16), wd=wd.astype(BF16))
        layers.append(lw)
    return layers


def _prompt_layer(x, mod, lw, tm):
    sh1, sc1, g1, sh2, sc2, g2 = mod
    w = lw["bf16"]
    kvt, u, v, ga, gb, kaug, qt, vt, gt, kvc = _in_proj(x, lw["norm_mix"], sc1, sh1, w["wn"], w["wt"],
                                                         lw["sg_norm_g"], lw["sg_norm_b"], tm, "cmp", False)
    onsa = _nsa_prompt(kvc, kaug, qt, vt, gt, lw["cmp_pe"], lw["cmp_w1"], lw["cmp_w2"])
    x = _mix_out(x, onsa, u, v, ga, gb, g1, w["wa"], w["wb"], w["wo"], lw["sgw_chunk"], lw["sgb_chunk"],
                 tm, True, False)
    x = _ffn(x, lw["norm_ffn"], sc2, sh2, g2, w["gu"], w["wd"], lw["router_w"], lw["router_b"],
             lw["norm_final"], lw["ffn_tm"], lw["routed"], lw["final"], False)
    return x, kvt


def _sample_layer(x, mod, lw, cache_t, win_t, layer, page_table):
    sh1, sc1, g1, sh2, sc2, g2 = mod
    tm = x.shape[0]
    w = lw["f32"]
    kvt, u, v, ga, gb, _, qt, _, gt, kv = _in_proj(x, lw["norm_mix"], sc1, sh1, w["wn"], w["wt"],
                                                   lw["sg_norm_g"], lw["sg_norm_b"], tm, "all", True)
    onsa = _nsa_sample(cache_t, win_t, layer, page_table, kv, qt, gt, lw["cmp_pe"], lw["cmp_w1"], lw["cmp_w2"])
    x = _mix_out(x, onsa, u, v, ga, gb, g1, w["wa"], w["wb"], w["wo"], lw["sgw_first"], lw["sgb_first"],
                 tm, False, True)
    x = _ffn(x, lw["norm_ffn"], sc2, sh2, g2, w["gu"], w["wd"], lw["router_w"], lw["router_b"],
             lw["norm_final"], tm, lw["routed"], lw["final"], True)
    return x, kv, kvt, v


PROMPT_ROW_TILE = 512
PROMPT_FFN_ROW_TILE = 1024


def kernel(x_prompt, x_sample, cache_kv, state_win, page_table, c_prompt, c_sample, norm_mix_g, norm_ffn_g,
           norm_final_g, w_ada, b_ada, w_in, cmp_pe, cmp_w1, cmp_w2, sg_norm_g, sg_norm_b, sg_w, sg_b,
           w_branch_nsa, w_branch_sg, w_out, ffn_w_gu, ffn_w_down, router_w, router_b, moe_w_gu, moe_w_down):
    batch, seq, _ = x_prompt.shape
    n_seq, dec_seq, _ = x_sample.shape
    depth = w_in.shape[0]
    assert batch == 1 and dec_seq == 1
    assert seq % PROMPT_FFN_ROW_TILE == 0 and seq // Q_BLOCK <= 256
    past_len = page_table.shape[1] * PAGE_SIZE
    assert past_len % CHUNK == 0 and past_len // LANES <= 256
    assert state_win.shape[2] == WINDOW and page_table.shape[1] % PAGES_PER_STEP == 0

    layers = _prep_weights(norm_mix_g, norm_ffn_g, norm_final_g, w_in, cmp_pe, cmp_w1, cmp_w2, sg_norm_g,
                           sg_norm_b, sg_w, sg_b, w_branch_nsa, w_branch_sg, w_out, ffn_w_gu, ffn_w_down,
                           router_w, router_b, moe_w_gu, moe_w_down)
    c_all = _pad_rows(jnp.concatenate([c_prompt, c_sample], axis=0), SUBLANES)
    mods = _ada(c_all, w_ada, b_ada)

    cache_t = cache_kv.transpose(0, 1, 3, 4, 5, 2)
    win_t = state_win.transpose(0, 1, 3, 4, 5, 2)

    xp = x_prompt[0]
    xs = x_sample[:, 0]
    kv_p, kv_s, win_p, win_new, sgv_s = [], [], [], [], []
    w_keep = min(WINDOW, seq)
    kv_rows = 4 * N_KV * HEAD_DIM
    for i in range(depth):
        lw = dict(layers[i])
        lw["ffn_tm"] = PROMPT_FFN_ROW_TILE
        mod_p = tuple(mods[i, 0:1, j * D_MODEL:(j + 1) * D_MODEL] for j in range(6))
        mod_s = tuple(mods[i, 1:1 + n_seq, j * D_MODEL:(j + 1) * D_MODEL] for j in range(6))
        xp, kvt_p = _prompt_layer(xp, mod_p, lw, PROMPT_ROW_TILE)
        xs, kvs, kvt_s, v_s = _sample_layer(xs, mod_s, lw, cache_t, win_t, i, page_table)
        kv_p.append(kvt_p[:kv_rows].reshape(4, N_KV, HEAD_DIM, seq))
        win_p.append(kvt_p[kv_rows:, seq - w_keep:].reshape(2, N_KV, HEAD_DIM, w_keep))
        kv_s.append(kvs[:, :kv_rows].reshape(n_seq, 1, 4, N_KV, HEAD_DIM))
        win_new.append(kvt_s[kv_rows:].reshape(2, N_KV, HEAD_DIM, n_seq).transpose(3, 0, 1, 2)[..., None])
        sgv_s.append(v_s.reshape(n_seq, 1, D_SG))
    kv_prompt = jnp.stack(kv_p).transpose(0, 4, 1, 2, 3)[:, None]
    win_prompt = jnp.stack(win_p).transpose(0, 4, 1, 2, 3)[:, None]
    win_sample = jnp.concatenate([win_t[..., 1:], jnp.stack(win_new)], axis=-1).transpose(0, 1, 5, 2, 3, 4)
    return (xp[None], xs[:, None], kv_prompt, jnp.stack(kv_s), win_prompt, win_sample, jnp.stack(sgv_s))
```

```python
import functools

import jax
import jax.numpy as jnp
from jax import lax
from jax.experimental import pallas as pl
from jax.experimental.pallas import tpu as pltpu

F32 = jnp.float32
BF16 = jnp.bfloat16
I32 = jnp.int32
HIGHEST = lax.Precision.HIGHEST

LANES = 128
SUBLANES = 8
VMEM_LIMIT_BYTES = 56 * 1024 * 1024

D_MODEL = 1024
N_HEADS = 8
HEAD_DIM = 64
N_KV = 2
GROUP = N_HEADS // N_KV
D_NSA = N_HEADS * HEAD_DIM
CMP_BLOCK = 32
CMP_HIDDEN = 256
SEL_BLOCK = 64
SEL_TOPK = 16
WINDOW = 512
Q_BLOCK = 128
N_SG = 8
SG_DIM = 64
D_SG = N_SG * SG_DIM
CHUNK = 128
N_EXPERTS = 8
D_FF_CHUNK = 1408
PAGE_SIZE = 128
EPS = 1e-6
NEG = -1e30
FORCED = 1e9
REMOVED = -3e38

KV_COLS = 6 * N_KV * HEAD_DIM
OFF_KV = D_NSA
OFF_G = OFF_KV + KV_COLS
OFF_U = OFF_G + 3 * N_HEADS
OFF_V = OFF_U + D_SG
OFF_GA = OFF_V + D_SG
OFF_GB = OFF_GA + D_MODEL
IN_COLS = OFF_GB + D_MODEL

WN_U = 0
WN_V = WN_U + D_SG
WN_GA = WN_V + D_SG
WN_GB = WN_GA + D_MODEL
WN_KAUG = WN_GB + D_MODEL
WN_KV = WN_KAUG + 4 * LANES
WN_COLS = WN_KV + KV_COLS
WT_Q = 0
WT_KV = D_NSA
WT_G = WT_KV + KV_COLS
WT_ROWS = WT_G + 32

POS_HI_LANE = HEAD_DIM
POS_LO_LANE = HEAD_DIM + 1
ONE_LANE = HEAD_DIM + 2


def _cparams(sem):
    return pltpu.CompilerParams(dimension_semantics=sem, vmem_limit_bytes=VMEM_LIMIT_BYTES)


def _bdot(a, b):
    return jnp.dot(a.astype(BF16), b.astype(BF16), preferred_element_type=F32)


def _dot_nt(a, b):
    return lax.dot_general(a.astype(BF16), b.astype(BF16), (((1,), (1,)), ((), ())),
                           preferred_element_type=F32)


def _mm(a, b, precise):
    if precise:
        return jnp.dot(a.astype(F32), b.astype(F32), precision=HIGHEST, preferred_element_type=F32)
    return _bdot(a, b)


def _mm_nt(a, b, precise):
    if precise:
        return lax.dot_general(a.astype(F32), b.astype(F32), (((1,), (1,)), ((), ())), precision=HIGHEST,
                               preferred_element_type=F32)
    return _dot_nt(a, b)


def _silu(x):
    return x * jax.nn.sigmoid(x)


def _pos_aug(pos, lane):
    hi = (pos >> 7).astype(F32)
    lo = (pos & 127).astype(F32)
    return jnp.where(lane == POS_HI_LANE, hi,
                     jnp.where(lane == POS_LO_LANE, lo,
                               jnp.where(lane == ONE_LANE, 1.0, 0.0)))


def _group_slopes(g, lane_head):
    out = jnp.zeros(lane_head.shape, F32)
    for gg in range(N_KV):
        for hh in range(GROUP):
            s = 2.0 ** (-8.0 * (gg * GROUP + hh + 1) / N_HEADS)
            out = jnp.where((lane_head == hh) & (g == gg), s, out)
    return out


def _qt_aug(qt_ref, g, q0_blocks):
    heads = [qt_ref[h * HEAD_DIM:(h + 1) * HEAD_DIM, :] for h in range(GROUP)]
    q = jnp.concatenate(heads, axis=1)
    n = GROUP * Q_BLOCK
    row = lax.broadcasted_iota(I32, (LANES - HEAD_DIM, n), 0)
    lane_head = lax.broadcasted_iota(I32, (LANES - HEAD_DIM, n), 1) // Q_BLOCK
    slope = _group_slopes(g, lane_head)
    q0f = (q0_blocks).astype(F32)
    aug = jnp.where(row == 0, slope * 128.0,
                    jnp.where(row == 1, slope,
                              jnp.where(row == 2, -(slope * 128.0) * q0f, 0.0)))
    return jnp.concatenate([q.astype(BF16), aug.astype(BF16)], axis=0)


def _ada_kernel(c_ref, w_ref, b_ref, o_ref):
    c = c_ref[...]
    o_ref[0] = jnp.dot(_silu(c), w_ref[0], precision=HIGHEST, preferred_element_type=F32) + b_ref[0]


def _ada(c_all, w_ada, b_ada):
    depth = w_ada.shape[0]
    rows = c_all.shape[0]
    tn = 1024
    n = w_ada.shape[2]
    return pl.pallas_call(
        _ada_kernel,
        grid=(depth, n // tn),
        in_specs=[pl.BlockSpec((rows, D_MODEL), lambda l, j: (0, 0)),
                  pl.BlockSpec((1, D_MODEL, tn), lambda l, j: (l, 0, j)),
                  pl.BlockSpec((1, 1, tn), lambda l, j: (l, 0, j))],
        out_specs=pl.BlockSpec((1, rows, tn), lambda l, j: (l, 0, j)),
        out_shape=jax.ShapeDtypeStruct((depth, rows, n), F32),
        compiler_params=_cparams(("arbitrary", "arbitrary")),
        name="ada",
    )(c_all, w_ada, b_ada.reshape(depth, 1, n))


def _in_proj_kernel(x_ref, g_ref, sc_ref, sh_ref, wn_ref, wt_ref, lng_ref, lnb_ref, *out_refs, tm, natural_kv,
                    precise):
    kvt_ref, u_ref, v_ref, ga_ref, gb_ref, kaug_ref, qt_ref, vt_ref, gt_ref, kv_ref = out_refs
    i = pl.program_id(0)
    x = x_ref[...]
    ms = jnp.mean(x * x, axis=-1, keepdims=True)
    h = x * lax.rsqrt(ms + EPS) * g_ref[...]
    h = h * (1.0 + sc_ref[...]) + sh_ref[...]
    hb = h if precise else h.astype(BF16)

    def seg(a, b):
        return _mm(hb, wn_ref[:, a:b], precise)

    u_ref[...] = jax.nn.gelu(seg(WN_U, WN_V)).astype(u_ref.dtype)
    v = jax.nn.gelu(seg(WN_V, WN_GA))
    mu = jnp.mean(v, axis=-1, keepdims=True)
    var = jnp.mean(jnp.square(v - mu), axis=-1, keepdims=True)
    v_ref[...] = (v - mu) * lax.rsqrt(var + EPS) * lng_ref[...] + lnb_ref[...]
    ga_ref[...] = jax.nn.sigmoid(seg(WN_GA, WN_GB)).astype(ga_ref.dtype)
    gb_ref[...] = jax.nn.sigmoid(seg(WN_GB, WN_KAUG)).astype(gb_ref.dtype)

    pos = lax.broadcasted_iota(I32, (tm, LANES), 0) + i * tm
    lane = lax.broadcasted_iota(I32, (tm, LANES), 1)
    aug = _pos_aug(pos, lane)
    for j in range(4):
        k = seg(WN_KAUG + j * LANES, WN_KAUG + (j + 1) * LANES)
        kaug_ref[j] = (k + aug).astype(BF16)
    if natural_kv == "all":
        kv_ref[...] = seg(WN_KV, WN_COLS)
    else:
        for j in range(2):
            kv_ref[j] = seg(WN_KV + j * LANES, WN_KV + (j + 1) * LANES)

    zt = _mm_nt(wt_ref[...], hb, precise)
    qt_ref[...] = (zt[WT_Q:WT_KV] * (HEAD_DIM ** -0.5)).astype(qt_ref.dtype)
    kvt = zt[WT_KV:WT_G]
    kvt_ref[...] = kvt
    two = N_KV * HEAD_DIM
    vt_ref[0:two, :] = kvt[3 * two:4 * two].astype(BF16)
    vt_ref[two:2 * two, :] = kvt[5 * two:6 * two].astype(BF16)
    gt_ref[...] = jax.nn.sigmoid(zt[WT_G:WT_ROWS])


def _in_proj(x, g, sc, sh, wn, wt, lng, lnb, tm, natural_kv, precise):
    m = x.shape[0]
    act = F32 if precise else BF16
    mod_rows = sc.shape[0]
    mod_block = (1, D_MODEL) if mod_rows == 1 else (tm, D_MODEL)
    mod_map = (lambda i: (0, 0)) if mod_rows == 1 else (lambda i: (i, 0))
    row = lambda i: (i, 0)
    col = lambda i: (0, i)
    const = lambda i: (0, 0)
    out_shape = [
        jax.ShapeDtypeStruct((KV_COLS, m), F32),
        jax.ShapeDtypeStruct((m, D_SG), act),
        jax.ShapeDtypeStruct((m, D_SG), F32),
        jax.ShapeDtypeStruct((m, D_MODEL), act),
        jax.ShapeDtypeStruct((m, D_MODEL), act),
        jax.ShapeDtypeStruct((4, m, LANES), BF16),
        jax.ShapeDtypeStruct((D_NSA, m), act),
        jax.ShapeDtypeStruct((4 * HEAD_DIM, m), BF16),
        jax.ShapeDtypeStruct((32, m), F32),
    ]
    out_specs = [
        pl.BlockSpec((KV_COLS, tm), col),
        pl.BlockSpec((tm, D_SG), row),
        pl.BlockSpec((tm, D_SG), row),
        pl.BlockSpec((tm, D_MODEL), row),
        pl.BlockSpec((tm, D_MODEL), row),
        pl.BlockSpec((4, tm, LANES), lambda i: (0, i, 0)),
        pl.BlockSpec((D_NSA, tm), col),
        pl.BlockSpec((4 * HEAD_DIM, tm), col),
        pl.BlockSpec((32, tm), col),
    ]
    if natural_kv == "all":
        out_shape.append(jax.ShapeDtypeStruct((m, KV_COLS), F32))
        out_specs.append(pl.BlockSpec((tm, KV_COLS), row))
    else:
        out_shape.append(jax.ShapeDtypeStruct((2, m, LANES), F32))
        out_specs.append(pl.BlockSpec((2, tm, LANES), lambda i: (0, i, 0)))
    return pl.pallas_call(
        functools.partial(_in_proj_kernel, tm=tm, natural_kv=natural_kv, precise=precise),
        grid=(m // tm,),
        in_specs=[pl.BlockSpec((tm, D_MODEL), row),
                  pl.BlockSpec((1, D_MODEL), const),
                  pl.BlockSpec(mod_block, mod_map),
                  pl.BlockSpec(mod_block, mod_map),
                  pl.BlockSpec((D_MODEL, WN_COLS), const),
                  pl.BlockSpec((WT_ROWS, D_MODEL), const),
                  pl.BlockSpec((1, D_SG), const),
                  pl.BlockSpec((1, D_SG), const)],
        out_specs=tuple(out_specs),
        out_shape=tuple(out_shape),
        compiler_params=_cparams(("arbitrary",)),
        name="in_proj",
    )(x, g, sc, sh, wn, wt, lng, lnb)


def _pack_w_in(w):
    kv = w[:, OFF_KV:OFF_G]

    def kvcol(j, g):
        return kv[:, (j * N_KV + g) * HEAD_DIM:(j * N_KV + g + 1) * HEAD_DIM]

    zpad = jnp.zeros((D_MODEL, LANES - HEAD_DIM), w.dtype)
    kaug = [jnp.concatenate([kvcol(j, g), zpad], axis=1) for j in (2, 4) for g in range(N_KV)]
    wn = jnp.concatenate([w[:, OFF_U:OFF_V], w[:, OFF_V:OFF_GA], w[:, OFF_GA:OFF_GB],
                          w[:, OFF_GB:IN_COLS]] + kaug + [kv], axis=1)
    gpad = jnp.zeros((D_MODEL, 32 - 3 * N_HEADS), w.dtype)
    wt = jnp.concatenate([w[:, :OFF_KV], kv, w[:, OFF_G:OFF_U], gpad], axis=1).T
    return wn, wt


def _compress_weights(pe, w1, w2):
    kd = CMP_BLOCK * LANES
    pe_flat = jnp.concatenate([pe, pe], axis=-1).reshape(2, 1, kd)
    w1r = w1.reshape(2, CMP_BLOCK, HEAD_DIM, CMP_HIDDEN)
    w1p = jnp.einsum("gh,jtdn->jtgdhn", jnp.eye(N_KV, dtype=w1.dtype), w1r)
    w1p = w1p.reshape(2, kd, N_KV * CMP_HIDDEN).astype(BF16)
    w2p = jnp.pad(w2, ((0, 0), (0, 0), (0, LANES - HEAD_DIM))).astype(BF16)
    return pe_flat, w1p, w2p


def _compress_kernel(x_ref, pe_ref, w1_ref, w2_ref, w2vt_ref, kc_ref, vct_ref, *, half):
    parity = pl.program_id(0)
    row = lax.broadcasted_iota(I32, (half, LANES), 0)
    lane = lax.broadcasted_iota(I32, (half, LANES), 1)
    aug = _pos_aug((2 * row + parity) * CMP_BLOCK + (CMP_BLOCK - 1), lane)
    hid_k = _bdot(x_ref[0] + pe_ref[0], w1_ref[0])
    hid_v = _bdot(x_ref[1] + pe_ref[1], w1_ref[1])
    for g in range(N_KV):
        cols = slice(g * CMP_HIDDEN, (g + 1) * CMP_HIDDEN)
        kc_ref[g] = (_bdot(_silu(hid_k[:, cols]), w2_ref[0]) + aug).astype(BF16)
        vct_ref[g] = _dot_nt(w2vt_ref[...], _silu(hid_v[:, cols])).astype(BF16)


def _compress_prompt(kvc, pe, w1, w2):
    m = kvc.shape[1]
    nc = m // CMP_BLOCK
    half = nc // 2
    kd = CMP_BLOCK * LANES
    x = kvc.reshape(2, half, 2 * kd)
    pe_flat, w1p, w2p = _compress_weights(pe, w1, w2)
    w2vt = w2[1].T.astype(BF16)
    return pl.pallas_call(
        functools.partial(_compress_kernel, half=half),
        grid=(2,),
        in_specs=[pl.BlockSpec((2, half, kd), lambda p: (0, 0, p)),
                  pl.BlockSpec((2, 1, kd), lambda p: (0, 0, 0)),
                  pl.BlockSpec((2, kd, N_KV * CMP_HIDDEN), lambda p: (0, 0, 0)),
                  pl.BlockSpec((2, CMP_HIDDEN, LANES), lambda p: (0, 0, 0)),
                  pl.BlockSpec((HEAD_DIM, CMP_HIDDEN), lambda p: (0, 0))],
        out_specs=(pl.BlockSpec((N_KV, half, LANES), lambda p: (0, p, 0)),
                   pl.BlockSpec((N_KV, HEAD_DIM, half), lambda p: (0, 0, p))),
        out_shape=(jax.ShapeDtypeStruct((N_KV, nc, LANES), BF16),
                   jax.ShapeDtypeStruct((N_KV, HEAD_DIM, nc), BF16)),
        compiler_params=_cparams(("arbitrary",)),
        name="compress_prompt",
    )(x, pe_flat, w1p, w2p, w2vt)


def _topk_mask_rows(imp, k):
    rows = lax.broadcasted_iota(I32, imp.shape, 0)
    big = jnp.int32(2 ** 30)

    def body(_, v):
        m = jnp.max(v, axis=0, keepdims=True)
        first = jnp.min(jnp.where(v == m, rows, big), axis=0, keepdims=True)
        return jnp.where(rows == first, REMOVED, v)

    return jnp.where(lax.fori_loop(0, k, body, imp) < 0.5 * REMOVED, 1.0, 0.0)


def _visited_blocks(sel, limit):
    ns = sel.shape[0]
    ones_q = jnp.ones((SUBLANES, sel.shape[1]), BF16)
    blk = lax.broadcasted_iota(I32, (SUBLANES, ns), 1)
    flags = ((_dot_nt(ones_q, sel) > 0.5) & (blk < limit)).astype(F32)
    r = lax.broadcasted_iota(I32, (ns, ns), 0)
    c = lax.broadcasted_iota(I32, (ns, ns), 1)
    upper = jnp.where(r <= c, 1.0, 0.0)
    prefix = _bdot(flags, upper)
    before = jnp.where(prefix[0:1, :] <= r.astype(F32), 1.0, 0.0)
    ids = _dot_nt(jnp.ones((SUBLANES, ns), BF16), before)
    ids = jnp.minimum(ids, ns - 1.0).astype(I32)
    total = jnp.broadcast_to(prefix[:, ns - 1:ns], (SUBLANES, ns))
    return ids, total


def _select_tile(qt_ref, kc_ref, vct_ref, mask_ref, ids_ref, cnt_ref, ocmp_ref, i, nc, live):
    ns = nc // 2
    n = GROUP * Q_BLOCK
    q0 = i * Q_BLOCK
    r = lax.broadcasted_iota(I32, (live, 1), 0)
    cend = jnp.concatenate([2 * r, 2 * r + 1], axis=0) * CMP_BLOCK + (CMP_BLOCK - 1)
    qpos = q0 + lax.broadcasted_iota(I32, (1, n), 1) % Q_BLOCK
    valid = cend <= qpos
    blk = lax.broadcasted_iota(I32, (live, Q_BLOCK), 0)
    cur = (q0 + lax.broadcasted_iota(I32, (live, Q_BLOCK), 1)) // SEL_BLOCK
    forced = (blk == 0) | (blk == cur) | (blk == cur - 1)
    imps = []
    for g in range(N_KV):
        qa = _qt_aug(qt_ref.at[g * GROUP * HEAD_DIM:(g + 1) * GROUP * HEAD_DIM], g, i)
        kc = jnp.concatenate([kc_ref[g, 0:live], kc_ref[g, ns:ns + live]], axis=0)
        s = jnp.dot(kc, qa, preferred_element_type=F32)
        s = jnp.where(valid, s, NEG)
        m = jnp.max(s, axis=0, keepdims=True)
        p = jnp.where(valid, jnp.exp(s - m), 0.0)
        p = p / jnp.maximum(jnp.sum(p, axis=0, keepdims=True), 1e-30)
        vct = jnp.concatenate([vct_ref[g, :, 0:live], vct_ref[g, :, ns:ns + live]], axis=1)
        ocmp_ref[0, g] = jnp.dot(vct, p.astype(BF16), preferred_element_type=F32)
        ph = p[:, 0:Q_BLOCK]
        for h in range(1, GROUP):
            ph = ph + p[:, h * Q_BLOCK:(h + 1) * Q_BLOCK]
        imp = ph[:live] + ph[live:]
        imp = jnp.where(forced, FORCED, imp)
        imps.append(jnp.where(blk <= cur, imp, NEG))
    for g in range(N_KV):
        sel = jnp.where(blk <= cur, _topk_mask_rows(imps[g], min(SEL_TOPK, ns)), 0.0)
        if live < ns:
            sel = jnp.concatenate([sel, jnp.zeros((ns - live, Q_BLOCK), F32)], axis=0)
        mask_ref[0, g] = sel
        ids_ref[0, g], cnt_ref[0, g] = _visited_blocks(sel, 2 * i)


def _select_kernel(qt_ref, kc_ref, vct_ref, mask_ref, ids_ref, cnt_ref, ocmp_ref, *, nc):
    i = pl.program_id(0)
    ns = nc // 2
    refs = (qt_ref, kc_ref, vct_ref, mask_ref, ids_ref, cnt_ref, ocmp_ref)
    half = ns // 2
    if half % LANES == 0 and half >= SEL_TOPK:
        early = 2 * i + 1 < half

        @pl.when(early)
        def _():
            _select_tile(*refs, i, nc, half)

        @pl.when(jnp.logical_not(early))
        def _():
            _select_tile(*refs, i, nc, ns)
    else:
        _select_tile(*refs, i, nc, ns)


def _select(qt, kc, vct):
    m = qt.shape[1]
    nqb = m // Q_BLOCK
    nc = m // CMP_BLOCK
    ns = nc // 2
    n = GROUP * Q_BLOCK
    return pl.pallas_call(
        functools.partial(_select_kernel, nc=nc),
        grid=(nqb,),
        in_specs=[pl.BlockSpec((D_NSA, Q_BLOCK), lambda i: (0, i)),
                  pl.BlockSpec((N_KV, nc, LANES), lambda i: (0, 0, 0)),
                  pl.BlockSpec((N_KV, HEAD_DIM, nc), lambda i: (0, 0, 0))],
        out_specs=(pl.BlockSpec((1, N_KV, ns, Q_BLOCK), lambda i: (i, 0, 0, 0)),
                   pl.BlockSpec((1, N_KV, SUBLANES, ns), lambda i: (i, 0, 0, 0)),
                   pl.BlockSpec((1, N_KV, SUBLANES, ns), lambda i: (i, 0, 0, 0)),
                   pl.BlockSpec((1, N_KV, HEAD_DIM, n), lambda i: (i, 0, 0, 0))),
        out_shape=(jax.ShapeDtypeStruct((nqb, N_KV, ns, Q_BLOCK), F32),
                   jax.ShapeDtypeStruct((nqb, N_KV, SUBLANES, ns), I32),
                   jax.ShapeDtypeStruct((nqb, N_KV, SUBLANES, ns), F32),
                   jax.ShapeDtypeStruct((nqb, N_KV, HEAD_DIM, n), F32)),
        compiler_params=_cparams(("arbitrary",)),
        name="nsa_select",
    )(qt, kc, vct)


SEL_PER_STEP = 16
WIN_TILES_PER_STEP = 4


def _softmax_step(state, s, valid=None):
    m, l, _ = state
    if valid is not None:
        s = jnp.where(valid, s, NEG)
    m_new = jnp.maximum(m, jnp.max(s, axis=0, keepdims=True))
    alpha = jnp.exp(m - m_new)
    p = jnp.exp(s - m_new)
    if valid is not None:
        p = jnp.where(valid, p, 0.0)
    return m_new, alpha * l + jnp.sum(p, axis=0, keepdims=True), alpha, p.astype(BF16)


def _pv_lane_tiles(vt_tiles):
    vt = jnp.concatenate(vt_tiles, axis=1)
    return lambda pb: jnp.dot(vt, pb, preferred_element_type=F32)


def _pv_row_tiles(v_tiles):
    v = jnp.concatenate(v_tiles, axis=0)
    return lambda pb: lax.dot_general(v, pb, (((0,), (0,)), ((), ())), preferred_element_type=F32)


def _attend_kernel(cnt_ref, ids_ref, qt_ref, ksel_ref, vsel_ref, kwin_ref, vwin_ref, mask_ref,
                   ocmp_ref, gt_ref, o_ref, *, nqb, ns):
    g = pl.program_id(0)
    i = pl.program_id(1)
    q0 = i * Q_BLOCK
    qa = _qt_aug(qt_ref, g, i)
    qa_heads = [qa[:, h * Q_BLOCK:(h + 1) * Q_BLOCK] for h in range(GROUP)]
    empty = (jnp.full((1, Q_BLOCK), NEG, F32), jnp.zeros((1, Q_BLOCK), F32), jnp.zeros((HEAD_DIM, Q_BLOCK), F32))
    init = (empty,) * GROUP

    def update_jobs(jobs):
        chains = [(job, h) for job in jobs for h in range(GROUP)]
        scores = [jnp.dot(job[1], qa_heads[h], preferred_element_type=F32) for job, h in chains]
        scores = [s if job[3] is None else s + job[3] for s, (job, h) in zip(scores, chains)]
        parts = [_softmax_step(job[0][h], s, job[4]) for s, (job, h) in zip(scores, chains)]
        pvs = [job[2](part[3]) for part, (job, h) in zip(parts, chains)]
        new = [(part[0], part[1], part[2] * job[0][h][2] + pv) for part, pv, (job, h) in zip(parts, pvs, chains)]
        return [tuple(new[k * GROUP:(k + 1) * GROUP]) for k in range(len(jobs))]

    def heads_update(states, keys, pv_fn, bias=None, valid=None):
        return update_jobs([(states, keys, pv_fn, bias, valid)])[0]

    slot = g * nqb + i
    count = cnt_ref[slot]
    key_off = lax.broadcasted_iota(I32, (Q_BLOCK, 1), 0)
    query_off = lax.broadcasted_iota(I32, (1, Q_BLOCK), 1)
    causal = key_off <= query_off

    own = mask_ref[0, 0, pl.ds(pl.multiple_of(2 * i, 2), 2), :]
    chosen = jnp.concatenate([jnp.broadcast_to(own[0:1] > 0.5, (SEL_BLOCK, Q_BLOCK)),
                              jnp.broadcast_to(own[1:2] > 0.5, (SEL_BLOCK, Q_BLOCK))], axis=0)
    k_own = ksel_ref[0, pl.ds(pl.multiple_of(q0, Q_BLOCK), Q_BLOCK), :]
    kw_own = kwin_ref[0, pl.ds(pl.multiple_of(q0, Q_BLOCK), Q_BLOCK), :]
    first, win_states = update_jobs([
        (init, k_own, _pv_row_tiles([vsel_ref[0, 2 * i], vsel_ref[0, 2 * i + 1]]), None, chosen & causal),
        (init, kw_own, _pv_lane_tiles([vwin_ref[0, i]]), None, causal)])

    def sel_body(t, states):
        ks, vts, biases = [], [], []
        for u in range(SEL_PER_STEP):
            e = t * SEL_PER_STEP + u
            b = ids_ref[slot * ns + e]
            ks.append(ksel_ref[0, pl.ds(pl.multiple_of(b * SEL_BLOCK, SEL_BLOCK), SEL_BLOCK), :])
            row = mask_ref[0, 0, pl.ds(b, 1), :]
            live = (row > 0.5) & (e < count)
            biases.append(jnp.broadcast_to(jnp.where(live, 0.0, NEG), (SEL_BLOCK, Q_BLOCK)))
            vts.append(vsel_ref[0, b])
        return heads_update(states, jnp.concatenate(ks, axis=0), _pv_row_tiles(vts),
                            bias=jnp.concatenate(biases, axis=0))

    sel_states = lax.fori_loop(0, (count + (SEL_PER_STEP - 1)) // SEL_PER_STEP, sel_body, first)

    n_old = WINDOW // Q_BLOCK
    oldest_in_window = key_off > query_off
    for j0 in range(0, n_old, WIN_TILES_PER_STEP):
        ks, vts, biases = [], [], []
        for j in range(j0, min(j0 + WIN_TILES_PER_STEP, n_old)):
            tile = i - n_old + j
            held = jnp.maximum(tile, 0)
            ks.append(kwin_ref[0, pl.ds(pl.multiple_of(held * Q_BLOCK, Q_BLOCK), Q_BLOCK), :])
            inside = tile >= 0
            live = (oldest_in_window & inside) if j == 0 else inside
            biases.append(jnp.broadcast_to(jnp.where(live, 0.0, NEG), (Q_BLOCK, Q_BLOCK)))
            vts.append(vwin_ref[0, held])
        win_states = heads_update(win_states, jnp.concatenate(ks, axis=0), _pv_lane_tiles(vts),
                                  bias=jnp.concatenate(biases, axis=0))

    def head_out(h):
        def gate(branch):
            return gt_ref[pl.ds(branch * N_HEADS + g * GROUP + h, 1), :]
        _, l_s, acc_s = sel_states[h]
        _, l_w, acc_w = win_states[h]
        return (ocmp_ref[0, 0, :, h * Q_BLOCK:(h + 1) * Q_BLOCK] * gate(0)
                + acc_s / jnp.maximum(l_s, 1e-30) * gate(1) + acc_w / jnp.maximum(l_w, 1e-30) * gate(2))

    for hp in range(GROUP // 2):
        pair = jnp.concatenate([head_out(2 * hp), head_out(2 * hp + 1)], axis=0)
        o_ref[:, hp * LANES:(hp + 1) * LANES] = pair.T.astype(o_ref.dtype)


def _attend(cnt, ids, qt, kaug, vt_blocks_sel, vt_blocks_win, mask, ocmp, gt):
    m = qt.shape[1]
    nqb = m // Q_BLOCK
    ns = m // SEL_BLOCK
    n = GROUP * Q_BLOCK
    gh = GROUP * HEAD_DIM
    grid_spec = pltpu.PrefetchScalarGridSpec(
        num_scalar_prefetch=2,
        grid=(N_KV, nqb),
        in_specs=[pl.BlockSpec((gh, Q_BLOCK), lambda g, i, c, d: (g, i)),
                  pl.BlockSpec((1, m, LANES), lambda g, i, c, d: (g, 0, 0)),
                  pl.BlockSpec((1, ns, SEL_BLOCK, HEAD_DIM), lambda g, i, c, d: (g, 0, 0, 0)),
                  pl.BlockSpec((1, m, LANES), lambda g, i, c, d: (N_KV + g, 0, 0)),
                  pl.BlockSpec((1, nqb, HEAD_DIM, Q_BLOCK), lambda g, i, c, d: (g, 0, 0, 0)),
                  pl.BlockSpec((1, 1, ns, Q_BLOCK), lambda g, i, c, d: (i, g, 0, 0)),
                  pl.BlockSpec((1, 1, HEAD_DIM, n), lambda g, i, c, d: (i, g, 0, 0)),
                  pl.BlockSpec((32, Q_BLOCK), lambda g, i, c, d: (0, i))],
        out_specs=pl.BlockSpec((Q_BLOCK, gh), lambda g, i, c, d: (i, g)),
    )
    return pl.pallas_call(
        functools.partial(_attend_kernel, nqb=nqb, ns=ns),
        grid_spec=grid_spec,
        out_shape=jax.ShapeDtypeStruct((m, D_NSA), BF16),
        compiler_params=_cparams(("arbitrary", "arbitrary")),
        name="nsa_attend",
    )(cnt, ids, qt, kaug, vt_blocks_sel, kaug, vt_blocks_win, mask, ocmp, gt)


def _nsa_prompt(kvc, kaug, qt, vt, gt, pe, w1, w2):
    m = kvc.shape[1]
    nqb = m // Q_BLOCK
    ns = m // SEL_BLOCK
    kc, vct = _compress_prompt(kvc, pe, w1, w2)
    mask, ids8, cnt8, ocmp = _select(qt, kc, vct)
    ids = ids8[:, :, 0, :].transpose(1, 0, 2).reshape(-1)
    cnt = cnt8[:, :, 0, 0].T.astype(I32).reshape(-1)
    vt4 = vt.reshape(4, HEAD_DIM, m)
    vsel = vt4[:N_KV].reshape(N_KV, HEAD_DIM, ns, SEL_BLOCK).transpose(0, 2, 3, 1)
    vwin = vt4[N_KV:].reshape(N_KV, HEAD_DIM, nqb, Q_BLOCK).transpose(0, 2, 1, 3)
    return _attend(cnt, ids, qt, kaug, vsel, vwin, mask, ocmp, gt)


def _mix_kernel(x_ref, onsa_ref, u_ref, v_ref, ga_ref, gb_ref, g1_ref, wa_ref, wb_ref, wo_ref,
                sgw_ref, sgb_ref, o_ref, osg_ref, *, tm, chunked, precise):
    if chunked:
        lane = lax.broadcasted_iota(I32, (CHUNK, LANES), 1)
        for c in range(tm // CHUNK):
            rows = slice(c * CHUNK, (c + 1) * CHUNK)
            for pr in range(N_SG // 2):
                cols = slice(pr * LANES, (pr + 1) * LANES)
                vp = v_ref[rows, cols].astype(BF16)
                a = jnp.dot(sgw_ref[2 * pr], vp, preferred_element_type=F32)
                b = jnp.dot(sgw_ref[2 * pr + 1], vp, preferred_element_type=F32)
                mix = jnp.where(lane < SG_DIM, a, b) + sgb_ref[:, cols]
                osg_ref[rows, cols] = (u_ref[rows, cols].astype(F32) * mix).astype(BF16)
    else:
        mix = v_ref[...] * sgw_ref[...] + sgb_ref[...]
        osg_ref[...] = (u_ref[...].astype(F32) * mix).astype(osg_ref.dtype)
    a = _mm(onsa_ref[...], wa_ref[...], precise)
    b = _mm(osg_ref[...], wb_ref[...], precise)
    merged = ga_ref[...].astype(F32) * a + gb_ref[...].astype(F32) * b
    y = _mm(merged, wo_ref[...], precise)
    o_ref[...] = x_ref[...] + g1_ref[...] * y


def _mix_out(x, onsa, u, v, ga, gb, g1, wa, wb, wo, sgw, sgb, tm, chunked, precise):
    m = x.shape[0]
    mod_rows = g1.shape[0]
    mod_block = (1, D_MODEL) if mod_rows == 1 else (tm, D_MODEL)
    mod_map = (lambda i: (0, 0)) if mod_rows == 1 else (lambda i: (i, 0))
    row = lambda i: (i, 0)
    const2 = lambda i: (0, 0)
    if chunked:
        sg_specs = [pl.BlockSpec((N_SG, CHUNK, CHUNK), lambda i: (0, 0, 0)),
                    pl.BlockSpec((CHUNK, D_SG), const2)]
    else:
        sg_specs = [pl.BlockSpec((1, D_SG), const2), pl.BlockSpec((1, D_SG), const2)]
    return pl.pallas_call(
        functools.partial(_mix_kernel, tm=tm, chunked=chunked, precise=precise),
        grid=(m // tm,),
        in_specs=[pl.BlockSpec((tm, D_MODEL), row),
                  pl.BlockSpec((tm, D_NSA), row),
                  pl.BlockSpec((tm, D_SG), row),
                  pl.BlockSpec((tm, D_SG), row),
                  pl.BlockSpec((tm, D_MODEL), row),
                  pl.BlockSpec((tm, D_MODEL), row),
                  pl.BlockSpec(mod_block, mod_map),
                  pl.BlockSpec((D_NSA, D_MODEL), const2),
                  pl.BlockSpec((D_SG, D_MODEL), const2),
                  pl.BlockSpec((D_MODEL, D_MODEL), const2)] + sg_specs,
        out_specs=pl.BlockSpec((tm, D_MODEL), row),
        out_shape=jax.ShapeDtypeStruct((m, D_MODEL), F32),
        scratch_shapes=[pltpu.VMEM((tm, D_SG), F32 if precise else BF16)],
        compiler_params=_cparams(("arbitrary",)),
        name="mix_out",
    )(x, onsa, u, v, ga, gb, g1, wa, wb, wo, sgw, sgb)


def _top2_combine(logits):
    lane = lax.broadcasted_iota(I32, logits.shape, 1)
    big = jnp.int32(2 ** 30)
    z = jnp.where(lane < N_EXPERTS, logits, -jnp.inf)
    t1 = jnp.max(z, axis=-1, keepdims=True)
    i1 = jnp.min(jnp.where(z == t1, lane, big), axis=-1, keepdims=True)
    z2 = jnp.where(lane == i1, -jnp.inf, z)
    t2 = jnp.max(z2, axis=-1, keepdims=True)
    i2 = jnp.min(jnp.where(z2 == t2, lane, big), axis=-1, keepdims=True)
    e = jnp.exp(t2 - t1)
    den = 1.0 + e
    return jnp.where(lane == i1, 1.0 / den, 0.0) + jnp.where(lane == i2, e / den, 0.0)


FFN_COL_SPLITS = ((0, 6 * LANES), (6 * LANES, D_FF_CHUNK))


def _ffn_kernel(x_ref, g_ref, sc_ref, sh_ref, g2_ref, wa_ref, wb_ref, wd_ref, rw_ref, rb_ref, gf_ref,
                o_ref, h_ref, acc_ref, comb_ref, *, routed, final_norm, n_chunks, precise):
    e = pl.program_id(1)

    @pl.when(e == 0)
    def _():
        x = x_ref[...]
        ms = jnp.mean(x * x, axis=-1, keepdims=True)
        h = x * lax.rsqrt(ms + EPS) * g_ref[...]
        h = h * (1.0 + sc_ref[...]) + sh_ref[...]
        h_ref[...] = h.astype(h_ref.dtype)
        acc_ref[...] = jnp.zeros_like(acc_ref)
        if routed:
            logits = jnp.dot(h, rw_ref[...], precision=HIGHEST, preferred_element_type=F32) + rb_ref[...]
            comb_ref[...] = _top2_combine(logits)

    hb = h_ref[...]
    y = None
    for c0, c1 in FFN_COL_SPLITS:
        a = _mm(hb, wa_ref[0, :, c0:c1], precise)
        b = _mm(hb, wb_ref[0, :, c0:c1], precise)
        part = _mm(_silu(a) * b, wd_ref[0, c0:c1, :], precise)
        y = part if y is None else y + part
    if routed:
        lane = lax.broadcasted_iota(I32, comb_ref.shape, 1)
        w = jnp.sum(jnp.where(lane == e, comb_ref[...], 0.0), axis=-1, keepdims=True)
        acc_ref[...] += w * y
    else:
        acc_ref[...] += y

    @pl.when(e == n_chunks - 1)
    def _():
        out = x_ref[...] + g2_ref[...] * acc_ref[...]
        if final_norm:
            ms = jnp.mean(out * out, axis=-1, keepdims=True)
            out = out * lax.rsqrt(ms + EPS) * gf_ref[...]
        o_ref[...] = out


def _ffn(x, g, sc, sh, g2, gu, wd, rw, rb, gf, tm, routed, final_norm, precise):
    m = x.shape[0]
    n_chunks = wd.shape[0]
    if routed:
        a_map, b_map = (lambda i, e: (e, 0, 0)), (lambda i, e: (e, 0, 1))
    else:
        a_map, b_map = (lambda i, e: (0, 0, e)), (lambda i, e: (0, 0, n_chunks + e))
    mod_rows = sc.shape[0]
    mod_block = (1, D_MODEL) if mod_rows == 1 else (tm, D_MODEL)
    mod_map = (lambda i, e: (0, 0)) if mod_rows == 1 else (lambda i, e: (i, 0))
    row = lambda i, e: (i, 0)
    const2 = lambda i, e: (0, 0)
    return pl.pallas_call(
        functools.partial(_ffn_kernel, routed=routed, final_norm=final_norm, n_chunks=n_chunks,
                          precise=precise),
        grid=(m // tm, n_chunks),
        in_specs=[pl.BlockSpec((tm, D_MODEL), row),
                  pl.BlockSpec((1, D_MODEL), const2),
                  pl.BlockSpec(mod_block, mod_map),
                  pl.BlockSpec(mod_block, mod_map),
                  pl.BlockSpec(mod_block, mod_map),
                  pl.BlockSpec((1, D_MODEL, D_FF_CHUNK), a_map),
                  pl.BlockSpec((1, D_MODEL, D_FF_CHUNK), b_map),
                  pl.BlockSpec((1, D_FF_CHUNK, D_MODEL), lambda i, e: (e, 0, 0)),
                  pl.BlockSpec((D_MODEL, LANES), const2),
                  pl.BlockSpec((1, LANES), const2),
                  pl.BlockSpec((1, D_MODEL), const2)],
        out_specs=pl.BlockSpec((tm, D_MODEL), row),
        out_shape=jax.ShapeDtypeStruct((m, D_MODEL), F32),
        scratch_shapes=[pltpu.VMEM((tm, D_MODEL), F32 if precise else BF16),
                        pltpu.VMEM((tm, D_MODEL), F32),
                        pltpu.VMEM((tm, LANES), F32)],
        compiler_params=_cparams(("arbitrary", "arbitrary")),
        name="ffn",
    )(x, g, sc, sh, g2, gu, gu, wd, rw, rb, gf)


PAGES_PER_STEP = 32
BLOCKS_PER_PAGE = PAGE_SIZE // CMP_BLOCK


def _compress_sample_kernel(pt_ref, cache_ref, pe_ref, w1_ref, w2_ref, kc_ref, vc_ref, pages_ref, sem_ref,
                            rows_ref, x_ref, *, layer, n_steps, total_steps):
    s = pl.program_id(1)
    step = pl.program_id(0) * n_steps + s
    slot = step % 2
    nb = PAGES_PER_STEP * BLOCKS_PER_PAGE
    half = nb // 2

    def page_copy(at_step, at_slot, p):
        page = pt_ref[at_step * PAGES_PER_STEP + p]
        return pltpu.make_async_copy(cache_ref.at[layer, page, pl.ds(0, 2)], pages_ref.at[at_slot, p],
                                     sem_ref.at[at_slot])

    def start_pages(at_step, at_slot):
        for p in range(PAGES_PER_STEP):
            page_copy(at_step, at_slot, p).start(priority=p % 2)

    @pl.when(step == 0)
    def _():
        start_pages(step, slot)

    @pl.when(step + 1 < total_steps)
    def _():
        start_pages(step + 1, 1 - slot)

    for p in range(PAGES_PER_STEP):
        page_copy(step, slot, p).wait()

    eye = (lax.broadcasted_iota(I32, (PAGE_SIZE, PAGE_SIZE), 0)
           == lax.broadcasted_iota(I32, (PAGE_SIZE, PAGE_SIZE), 1)).astype(BF16)
    for j in range(2):
        for p in range(PAGES_PER_STEP):
            tile = pages_ref[slot, p, j].reshape(N_KV * HEAD_DIM, PAGE_SIZE) + pe_ref[j]
            rows_ref[j, p * PAGE_SIZE:(p + 1) * PAGE_SIZE, :] = _dot_nt(eye, tile)
    for t in range(CMP_BLOCK):
        for j in range(2):
            even = rows_ref[j, pl.ds(t, half, stride=2 * CMP_BLOCK), :]
            odd = rows_ref[j, pl.ds(CMP_BLOCK + t, half, stride=2 * CMP_BLOCK), :]
            x_ref[j, :, t * LANES:(t + 1) * LANES] = jnp.concatenate([even, odd], axis=0).astype(BF16)
    row = lax.broadcasted_iota(I32, (nb, LANES), 0)
    lane = lax.broadcasted_iota(I32, (nb, LANES), 1)
    c = s * nb + 2 * (row % half) + row // half
    aug = _pos_aug(c * CMP_BLOCK + (CMP_BLOCK - 1), lane)
    hid_k = jnp.dot(x_ref[0], w1_ref[0], preferred_element_type=F32)
    hid_v = jnp.dot(x_ref[1], w1_ref[1], preferred_element_type=F32)
    for g in range(N_KV):
        cols = slice(g * CMP_HIDDEN, (g + 1) * CMP_HIDDEN)
        hk = _silu(hid_k[:, cols]).astype(BF16)
        kc = (jnp.dot(hk, w2_ref[0], preferred_element_type=F32) + aug).astype(BF16)
        kc_ref[0, g, 0] = kc[:half]
        kc_ref[0, g, 1] = kc[half:]
        hv = _silu(hid_v[:, cols]).astype(BF16)
        vc = jnp.dot(hv, w2_ref[1], preferred_element_type=F32).astype(BF16)
        vc_ref[0, g, 0] = vc[:half]
        vc_ref[0, g, 1] = vc[half:]


def _compress_sample(cache_t, layer, page_table, pe, w1, w2):
    n_seq, n_pages = page_table.shape
    n_steps = n_pages // PAGES_PER_STEP
    nb = PAGES_PER_STEP * BLOCKS_PER_PAGE
    half = nb // 2
    nc_half = n_pages * BLOCKS_PER_PAGE // 2
    pe_flat, w1p, w2p = _compress_weights(pe, w1, w2)
    pe_t = jnp.tile(pe.transpose(0, 2, 1), (1, N_KV, BLOCKS_PER_PAGE))

    grid_spec = pltpu.PrefetchScalarGridSpec(
        num_scalar_prefetch=1,
        grid=(n_seq, n_steps),
        in_specs=[
            pl.BlockSpec(memory_space=pl.ANY),
            pl.BlockSpec((2, N_KV * HEAD_DIM, PAGE_SIZE), lambda b, s, pt: (0, 0, 0)),
            pl.BlockSpec((2, CMP_BLOCK * LANES, N_KV * CMP_HIDDEN), lambda b, s, pt: (0, 0, 0)),
            pl.BlockSpec((2, CMP_HIDDEN, LANES), lambda b, s, pt: (0, 0, 0))],
        out_specs=(pl.BlockSpec((1, N_KV, 2, half, LANES), lambda b, s, pt: (b, 0, 0, s, 0)),
                   pl.BlockSpec((1, N_KV, 2, half, LANES), lambda b, s, pt: (b, 0, 0, s, 0))),
        scratch_shapes=[pltpu.VMEM((2, PAGES_PER_STEP, 2, N_KV, HEAD_DIM, PAGE_SIZE), F32),
                        pltpu.SemaphoreType.DMA((2,)),
                        pltpu.VMEM((2, PAGES_PER_STEP * PAGE_SIZE, LANES), F32),
                        pltpu.VMEM((2, nb, CMP_BLOCK * LANES), BF16)],
    )
    kc, vc = pl.pallas_call(
        functools.partial(_compress_sample_kernel, layer=layer, n_steps=n_steps, total_steps=n_seq * n_steps),
        grid_spec=grid_spec,
        out_shape=(jax.ShapeDtypeStruct((n_seq, N_KV, 2, nc_half, LANES), BF16),
                   jax.ShapeDtypeStruct((n_seq, N_KV, 2, nc_half, LANES), BF16)),
        compiler_params=_cparams(("arbitrary", "arbitrary")),
        name="compress_sample",
    )(page_table.reshape(-1), cache_t, pe_t, w1p, w2p)
    return (kc.reshape(n_seq, N_KV, 2 * nc_half, LANES), vc.reshape(n_seq, N_KV, 2 * nc_half, LANES))


def _row_slopes(shape):
    head = lax.broadcasted_iota(I32, shape, 0)
    out = jnp.zeros(shape, F32)
    for h in range(N_HEADS):
        out = jnp.where(head == h, 2.0 ** (-8.0 * (h + 1) / N_HEADS), out)
    return out


def _sample_select_kernel(q_ref, kc_ref, vc_ref, ocmp_ref, imp_ref, *, nc):
    q = q_ref[0]
    ns = nc // 2
    rowgroup = lax.broadcasted_iota(I32, (N_HEADS, 1), 0) // GROUP
    o = jnp.zeros((N_HEADS, LANES), F32)
    for g in range(N_KV):
        s = _dot_nt(q, kc_ref[0, g])
        m = jnp.max(s, axis=-1, keepdims=True)
        p = jnp.exp(s - m)
        p = p / jnp.maximum(jnp.sum(p, axis=-1, keepdims=True), 1e-30)
        og = jnp.dot(p.astype(BF16), vc_ref[0, g], preferred_element_type=F32)
        mine = rowgroup == g
        o = jnp.where(mine, og, o)
        ph = jnp.sum(jnp.where(mine, p, 0.0), axis=0, keepdims=True)
        imp_ref[0, pl.ds(g, 1), :] = ph[:, :ns] + ph[:, ns:]
    ocmp_ref[0] = o


def _sample_select(q_aug, kc, vc):
    n_seq, _, nc, _ = kc.shape
    ns = nc // 2
    return pl.pallas_call(
        functools.partial(_sample_select_kernel, nc=nc),
        grid=(n_seq,),
        in_specs=[pl.BlockSpec((1, N_HEADS, LANES), lambda b: (b, 0, 0)),
                  pl.BlockSpec((1, N_KV, nc, LANES), lambda b: (b, 0, 0, 0)),
                  pl.BlockSpec((1, N_KV, nc, LANES), lambda b: (b, 0, 0, 0))],
        out_specs=(pl.BlockSpec((1, N_HEADS, LANES), lambda b: (b, 0, 0)),
                   pl.BlockSpec((1, N_KV, ns), lambda b: (b, 0, 0))),
        out_shape=(jax.ShapeDtypeStruct((n_seq, N_HEADS, LANES), F32),
                   jax.ShapeDtypeStruct((n_seq, N_KV, ns), F32)),
        compiler_params=_cparams(("arbitrary",)),
        name="sample_select",
    )(q_aug, kc, vc)


def _sample_topk_kernel(imp_ref, ids_ref, *, ns, k):
    imp = imp_ref[...]
    lane = lax.broadcasted_iota(I32, imp.shape, 1)
    out_lane = lax.broadcasted_iota(I32, ids_ref.shape, 1)
    big = jnp.int32(2 ** 30)
    v = jnp.where((lane == 0) | (lane == ns - 1), FORCED, imp)
    ids = jnp.zeros(ids_ref.shape, I32)
    for t in range(k):
        m = jnp.max(v, axis=-1, keepdims=True)
        first = jnp.min(jnp.where(v == m, lane, big), axis=-1, keepdims=True)
        v = jnp.where(lane == first, REMOVED, v)
        ids = jnp.where(out_lane == t, first, ids)
    ids_ref[...] = ids


def _sample_topk(imp, k):
    rows, ns = imp.shape
    return pl.pallas_call(
        functools.partial(_sample_topk_kernel, ns=ns, k=k),
        out_shape=jax.ShapeDtypeStruct((rows, LANES), I32),
        name="sample_topk",
    )(imp)


SEL_PAST = SEL_TOPK - 1


def _sample_attend_kernel(pt_ref, ids_ref, *refs, past_len):
    del pt_ref
    nblk = N_KV * SEL_PAST
    blocks = refs[:nblk]
    win_ref, new_ref, q_ref, ocmp_ref, gate_ref, o_ref = refs[nblk:]
    b = pl.program_id(0)
    slope = _row_slopes((N_HEADS, 1))
    rowgroup = lax.broadcasted_iota(I32, (N_HEADS, 1), 0) // GROUP
    gates = gate_ref[0]
    w_buf = win_ref.shape[-1]
    q = q_ref[0]
    qf = q.astype(F32)
    lane = lax.broadcasted_iota(I32, (1, PAGE_SIZE), 1)
    o = jnp.zeros((N_HEADS, HEAD_DIM), F32)
    for g in range(N_KV):
        def new_row(j):
            return new_ref[0, j * N_KV + g:j * N_KV + g + 1, :].astype(BF16).astype(F32)

        s_new = jnp.sum(qf * new_row(2), axis=-1, keepdims=True)
        scores, oks = [], []
        for t in range(SEL_PAST):
            blk = ids_ref[(b * N_KV + g) * SEL_TOPK + t]
            kt = blocks[g * SEL_PAST + t][0].astype(BF16)
            in_blk = (lane // SEL_BLOCK) == (blk % 2)
            pos = (blk // 2) * PAGE_SIZE + lane
            s = jnp.dot(q, kt, preferred_element_type=F32) - slope * (past_len - pos).astype(F32)
            scores.append(jnp.where(in_blk, s, NEG))
            oks.append(in_blk)
        m = s_new
        for s in scores:
            m = jnp.maximum(m, jnp.max(s, axis=-1, keepdims=True))
        p_new = jnp.exp(s_new - m)
        den = p_new
        acc = p_new * new_row(3)
        for t in range(SEL_PAST):
            p = jnp.where(oks[t], jnp.exp(scores[t] - m), 0.0)
            den = den + jnp.sum(p, axis=-1, keepdims=True)
            acc = acc + _dot_nt(p, blocks[g * SEL_PAST + t][1])
        o_sel = acc / jnp.maximum(den, 1e-30)
        kwt = win_ref[0, g].astype(BF16)
        dist = w_buf - lax.broadcasted_iota(I32, (1, w_buf), 1)
        ok = dist < WINDOW
        s = jnp.where(ok, jnp.dot(q, kwt, preferred_element_type=F32) - slope * dist.astype(F32), NEG)
        s_new = jnp.sum(qf * new_row(4), axis=-1, keepdims=True)
        m = jnp.maximum(s_new, jnp.max(s, axis=-1, keepdims=True))
        p = jnp.where(ok, jnp.exp(s - m), 0.0)
        p_new = jnp.exp(s_new - m)
        den = p_new + jnp.sum(p, axis=-1, keepdims=True)
        o_win = (p_new * new_row(5) + _dot_nt(p, win_ref[1, g])) / jnp.maximum(den, 1e-30)
        og = gates[:, 0:1] * ocmp_ref[0][:, :HEAD_DIM] + gates[:, 1:2] * o_sel + gates[:, 2:3] * o_win
        o = jnp.where(rowgroup == g, og, o)
    o_ref[0] = o


def _sample_attend(cache_t, win_t, layer, page_table, ids, kv_new, q, ocmp, gates, past_len):
    n_seq, n_pages = page_table.shape
    w_buf = win_t.shape[-1]
    ids3 = ids.reshape(n_seq, N_KV, SEL_TOPK)
    phys = jnp.take_along_axis(page_table[:, None, :], ids3 // 2, axis=-1)
    phys = jnp.pad(phys, ((0, 1), (0, 0), (0, 0))).astype(I32)

    def blk_map(g, t):
        return lambda b, ph, idr: (layer, ph[(b * N_KV + g) * SEL_TOPK + t], 1, g, 0, 0)

    grid_spec = pltpu.PrefetchScalarGridSpec(
        num_scalar_prefetch=2,
        grid=(n_seq,),
        in_specs=[pl.BlockSpec((None, None, 2, None, HEAD_DIM, PAGE_SIZE), blk_map(g, t))
                  for g in range(N_KV) for t in range(SEL_PAST)] + [
            pl.BlockSpec((None, None, 2, N_KV, HEAD_DIM, w_buf), lambda b, pt, idr: (layer, b, 0, 0, 0, 0)),
            pl.BlockSpec((1, 6 * N_KV, HEAD_DIM), lambda b, pt, idr: (b, 0, 0)),
            pl.BlockSpec((1, N_HEADS, HEAD_DIM), lambda b, pt, idr: (b, 0, 0)),
            pl.BlockSpec((1, N_HEADS, LANES), lambda b, pt, idr: (b, 0, 0)),
            pl.BlockSpec((1, N_HEADS, LANES), lambda b, pt, idr: (b, 0, 0))],
        out_specs=pl.BlockSpec((1, N_HEADS, HEAD_DIM), lambda b, pt, idr: (b, 0, 0)),
    )
    return pl.pallas_call(
        functools.partial(_sample_attend_kernel, past_len=past_len),
        grid_spec=grid_spec,
        out_shape=jax.ShapeDtypeStruct((n_seq, N_HEADS, HEAD_DIM), F32),
        compiler_params=_cparams(("arbitrary",)),
        name="sample_attend",
    )(phys.reshape(-1), ids.reshape(-1), *([cache_t] * (N_KV * SEL_PAST)), win_t, kv_new, q, ocmp, gates)


def _nsa_sample(cache_t, win_t, layer, page_table, kv_new, qt, gt, pe, w1, w2):
    n_seq, n_pages = page_table.shape
    past_len = n_pages * PAGE_SIZE
    kc, vc = _compress_sample(cache_t, layer, page_table, pe, w1, w2)
    q = qt.T.reshape(n_seq, N_HEADS, HEAD_DIM).astype(BF16)
    slopes = (2.0 ** (-8.0 * jnp.arange(1, N_HEADS + 1, dtype=F32) / N_HEADS)).reshape(1, N_HEADS, 1)
    aug = jnp.concatenate([slopes * 128.0, slopes, -(slopes * 128.0) * (past_len // 128),
                           jnp.zeros((1, N_HEADS, LANES - HEAD_DIM - 3), F32)], axis=-1)
    q_aug = jnp.concatenate([q, jnp.broadcast_to(aug, (n_seq, N_HEADS, LANES - HEAD_DIM)).astype(BF16)], axis=-1)
    ocmp, imp = _sample_select(q_aug, kc, vc)
    ids = _sample_topk(imp.reshape(n_seq * N_KV, -1), SEL_PAST)[:, :SEL_TOPK]
    gates = gt[:3 * N_HEADS].reshape(3, N_HEADS, n_seq).transpose(2, 1, 0)
    gates = jnp.pad(gates, ((0, 0), (0, 0), (0, LANES - 3)))
    o = _sample_attend(cache_t, win_t, layer, page_table, ids, kv_new.reshape(n_seq, 6 * N_KV, HEAD_DIM), q,
                       ocmp, gates, past_len)
    return o.reshape(n_seq, D_NSA)


def _sg_chunk_params(sg_w, sg_b):
    w = jnp.tril(sg_w).astype(BF16)
    bias = jnp.repeat(sg_b.T, SG_DIM, axis=1)
    return w, bias


def _pad_rows(a, mult):
    pad = (-a.shape[0]) % mult
    return jnp.pad(a, ((0, pad),) + ((0, 0),) * (a.ndim - 1)) if pad else a


def _prep_weights(norm_mix_g, norm_ffn_g, norm_final_g, w_in, cmp_pe, cmp_w1, cmp_w2, sg_norm_g, sg_norm_b,
                  sg_w, sg_b, w_branch_nsa, w_branch_sg, w_out, ffn_w_gu, ffn_w_down, router_w, router_b,
                  moe_w_gu, moe_w_down):
    depth = w_in.shape[0]
    layers = []
    for i in range(depth):
        wn, wt = _pack_w_in(w_in[i])
        sgw_chunk, sgb_chunk = _sg_chunk_params(sg_w[i], sg_b[i])
        lw = {
            "norm_mix": norm_mix_g[i].reshape(1, D_MODEL), "norm_ffn": norm_ffn_g[i].reshape(1, D_MODEL),
            "norm_final": norm_final_g.reshape(1, D_MODEL),
            "sg_norm_g": sg_norm_g[i].reshape(1, D_SG), "sg_norm_b": sg_norm_b[i].reshape(1, D_SG),
            "cmp_pe": cmp_pe[i], "cmp_w1": cmp_w1[i], "cmp_w2": cmp_w2[i],
            "f32": {"wn": wn, "wt": wt, "wa": w_branch_nsa[i], "wb": w_branch_sg[i], "wo": w_out[i]},
            "bf16": {"wn": wn.astype(BF16), "wt": wt.astype(BF16), "wa": w_branch_nsa[i].astype(BF16),
                     "wb": w_branch_sg[i].astype(BF16), "wo": w_out[i].astype(BF16)},
            "sgw_chunk": sgw_chunk, "sgb_chunk": sgb_chunk,
            "sgw_first": jnp.repeat(sg_w[i][:, 0, 0], SG_DIM).reshape(1, D_SG),
            "sgb_first": jnp.repeat(sg_b[i][:, 0], SG_DIM).reshape(1, D_SG),
            "routed": i % 2 == 1, "final": i == depth - 1,
        }
        if i % 2 == 0:
            gu = ffn_w_gu[i // 2][None]
            wd = ffn_w_down[i // 2].reshape(-1, D_FF_CHUNK, D_MODEL)
            lw["router_w"] = jnp.zeros((D_MODEL, LANES), F32)
            lw["router_b"] = jnp.zeros((1, LANES), F32)
        else:
            gu, wd = moe_w_gu[i // 2], moe_w_down[i // 2]
            lw["router_w"] = jnp.pad(router_w[i // 2], ((0, 0), (0, LANES - N_EXPERTS)))
            lw["router_b"] = jnp.pad(router_b[i // 2], (0, LANES - N_EXPERTS)).reshape(1, LANES)
        lw["f32"].update(gu=gu, wd=wd)
        lw["bf16"].update(gu=gu.astype(BF16), wd=wd.astype(BF16))
        layers.append(lw)
    return layers


def _prompt_layer(x, mod, lw, tm):
    sh1, sc1, g1, sh2, sc2, g2 = mod
    w = lw["bf16"]
    kvt, u, v, ga, gb, kaug, qt, vt, gt, kvc = _in_proj(x, lw["norm_mix"], sc1, sh1, w["wn"], w["wt"],
                                                         lw["sg_norm_g"], lw["sg_norm_b"], tm, "cmp", False)
    onsa = _nsa_prompt(kvc, kaug, qt, vt, gt, lw["cmp_pe"], lw["cmp_w1"], lw["cmp_w2"])
    x = _mix_out(x, onsa, u, v, ga, gb, g1, w["wa"], w["wb"], w["wo"], lw["sgw_chunk"], lw["sgb_chunk"],
                 tm, True, False)
    x = _ffn(x, lw["norm_ffn"], sc2, sh2, g2, w["gu"], w["wd"], lw["router_w"], lw["router_b"],
             lw["norm_final"], lw["ffn_tm"], lw["routed"], lw["final"], False)
    return x, kvt


def _sample_layer(x, mod, lw, cache_t, win_t, layer, page_table):
    sh1, sc1, g1, sh2, sc2, g2 = mod
    tm = x.shape[0]
    w = lw["f32"]
    kvt, u, v, ga, gb, _, qt, _, gt, kv = _in_proj(x, lw["norm_mix"], sc1, sh1, w["wn"], w["wt"],
                                                   lw["sg_norm_g"], lw["sg_norm_b"], tm, "all", True)
    onsa = _nsa_sample(cache_t, win_t, layer, page_table, kv, qt, gt, lw["cmp_pe"], lw["cmp_w1"], lw["cmp_w2"])
    x = _mix_out(x, onsa, u, v, ga, gb, g1, w["wa"], w["wb"], w["wo"], lw["sgw_first"], lw["sgb_first"],
                 tm, False, True)
    x = _ffn(x, lw["norm_ffn"], sc2, sh2, g2, w["gu"], w["wd"], lw["router_w"], lw["router_b"],
             lw["norm_final"], tm, lw["routed"], lw["final"], True)
    return x, kv, kvt, v


PROMPT_ROW_TILE = 512
PROMPT_FFN_ROW_TILE = 1024


def kernel(x_prompt, x_sample, cache_kv, state_win, page_table, c_prompt, c_sample, norm_mix_g, norm_ffn_g,
           norm_final_g, w_ada, b_ada, w_in, cmp_pe, cmp_w1, cmp_w2, sg_norm_g, sg_norm_b, sg_w, sg_b,
           w_branch_nsa, w_branch_sg, w_out, ffn_w_gu, ffn_w_down, router_w, router_b, moe_w_gu, moe_w_down):
    batch, seq, _ = x_prompt.shape
    n_seq, dec_seq, _ = x_sample.shape
    depth = w_in.shape[0]
    assert batch == 1 and dec_seq == 1
    assert seq % PROMPT_FFN_ROW_TILE == 0 and seq // Q_BLOCK <= 256
    past_len = page_table.shape[1] * PAGE_SIZE
    assert past_len % CHUNK == 0 and past_len // LANES <= 256
    assert state_win.shape[2] == WINDOW and page_table.shape[1] % PAGES_PER_STEP == 0

    layers = _prep_weights(norm_mix_g, norm_ffn_g, norm_final_g, w_in, cmp_pe, cmp_w1, cmp_w2, sg_norm_g,
                           sg_norm_b, sg_w, sg_b, w_branch_nsa, w_branch_sg, w_out, ffn_w_gu, ffn_w_down,
                           router_w, router_b, moe_w_gu, moe_w_down)
    c_all = _pad_rows(jnp.concatenate([c_prompt, c_sample], axis=0), SUBLANES)
    mods = _ada(c_all, w_ada, b_ada)

    cache_t = cache_kv.transpose(0, 1, 3, 4, 5, 2)
    win_t = state_win.transpose(0, 1, 3, 4, 5, 2)

    xp = x_prompt[0]
    xs = x_sample[:, 0]
    kv_p, kv_s, win_p, win_new, sgv_s = [], [], [], [], []
    w_keep = min(WINDOW, seq)
    kv_rows = 4 * N_KV * HEAD_DIM
    for i in range(depth):
        lw = dict(layers[i])
        lw["ffn_tm"] = PROMPT_FFN_ROW_TILE
        mod_p = tuple(mods[i, 0:1, j * D_MODEL:(j + 1) * D_MODEL] for j in range(6))
        mod_s = tuple(mods[i, 1:1 + n_seq, j * D_MODEL:(j + 1) * D_MODEL] for j in range(6))
        xp, kvt_p = _prompt_layer(xp, mod_p, lw, PROMPT_ROW_TILE)
        xs, kvs, kvt_s, v_s = _sample_layer(xs, mod_s, lw, cache_t, win_t, i, page_table)
        kv_p.append(kvt_p[:kv_rows].reshape(4, N_KV, HEAD_DIM, seq))
        win_p.append(kvt_p[kv_rows:, seq - w_keep:].reshape(2, N_KV, HEAD_DIM, w_keep))
        kv_s.append(kvs[:, :kv_rows].reshape(n_seq, 1, 4, N_KV, HEAD_DIM))
        win_new.append(kvt_s[kv_rows:].reshape(2, N_KV, HEAD_DIM, n_seq).transpose(3, 0, 1, 2)[..., None])
        sgv_s.append(v_s.reshape(n_seq, 1, D_SG))
    kv_prompt = jnp.stack(kv_p).transpose(0, 4, 1, 2, 3)[:, None]
    win_prompt = jnp.stack(win_p).transpose(0, 4, 1, 2, 3)[:, None]
    win_sample = jnp.concatenate([win_t[..., 1:], jnp.stack(win_new)], axis=-1).transpose(0, 1, 5, 2, 3, 4)
    return (xp[None], xs[:, None], kv_prompt, jnp.stack(kv_s), win_prompt, win_sample, jnp.stack(sgv_s))
```

```python
import functools

import jax
import jax.numpy as jnp
from jax import lax
from jax.experimental import pallas as pl
from jax.experimental.pallas import tpu as pltpu

F32 = jnp.float32
BF16 = jnp.bfloat16
I32 = jnp.int32
HIGHEST = lax.Precision.HIGHEST

LANES = 128
SUBLANES = 8
VMEM_LIMIT_BYTES = 56 * 1024 * 1024

D_MODEL = 1024
N_HEADS = 8
HEAD_DIM = 64
N_KV = 2
GROUP = N_HEADS // N_KV
D_NSA = N_HEADS * HEAD_DIM
CMP_BLOCK = 32
CMP_HIDDEN = 256
SEL_BLOCK = 64
SEL_TOPK = 16
WINDOW = 512
Q_BLOCK = 128
N_SG = 8
SG_DIM = 64
D_SG = N_SG * SG_DIM
CHUNK = 128
N_EXPERTS = 8
D_FF_CHUNK = 1408
PAGE_SIZE = 128
EPS = 1e-6
NEG = -1e30
FORCED = 1e9
REMOVED = -3e38

KV_COLS = 6 * N_KV * HEAD_DIM
OFF_KV = D_NSA
OFF_G = OFF_KV + KV_COLS
OFF_U = OFF_G + 3 * N_HEADS
OFF_V = OFF_U + D_SG
OFF_GA = OFF_V + D_SG
OFF_GB = OFF_GA + D_MODEL
IN_COLS = OFF_GB + D_MODEL

WN_U = 0
WN_V = WN_U + D_SG
WN_GA = WN_V + D_SG
WN_GB = WN_GA + D_MODEL
WN_KAUG = WN_GB + D_MODEL
WN_KV = WN_KAUG + 4 * LANES
WN_COLS = WN_KV + KV_COLS
WT_Q = 0
WT_KV = D_NSA
WT_G = WT_KV + KV_COLS
WT_ROWS = WT_G + 32

POS_HI_LANE = HEAD_DIM
POS_LO_LANE = HEAD_DIM + 1
ONE_LANE = HEAD_DIM + 2


def _cparams(sem):
    return pltpu.CompilerParams(dimension_semantics=sem, vmem_limit_bytes=VMEM_LIMIT_BYTES)


def _bdot(a, b):
    return jnp.dot(a.astype(BF16), b.astype(BF16), preferred_element_type=F32)


def _dot_nt(a, b):
    return lax.dot_general(a.astype(BF16), b.astype(BF16), (((1,), (1,)), ((), ())),
                           preferred_element_type=F32)


def _mm(a, b, precise):
    if precise:
        return jnp.dot(a.astype(F32), b.astype(F32), precision=HIGHEST, preferred_element_type=F32)
    return _bdot(a, b)


def _mm_nt(a, b, precise):
    if precise:
        return lax.dot_general(a.astype(F32), b.astype(F32), (((1,), (1,)), ((), ())), precision=HIGHEST,
                               preferred_element_type=F32)
    return _dot_nt(a, b)


def _silu(x):
    return x * jax.nn.sigmoid(x)


def _pos_aug(pos, lane):
    hi = (pos >> 7).astype(F32)
    lo = (pos & 127).astype(F32)
    return jnp.where(lane == POS_HI_LANE, hi,
                     jnp.where(lane == POS_LO_LANE, lo,
                               jnp.where(lane == ONE_LANE, 1.0, 0.0)))


def _group_slopes(g, lane_head):
    out = jnp.zeros(lane_head.shape, F32)
    for gg in range(N_KV):
        for hh in range(GROUP):
            s = 2.0 ** (-8.0 * (gg * GROUP + hh + 1) / N_HEADS)
            out = jnp.where((lane_head == hh) & (g == gg), s, out)
    return out


def _qt_aug(qt_ref, g, q0_blocks):
    heads = [qt_ref[h * HEAD_DIM:(h + 1) * HEAD_DIM, :] for h in range(GROUP)]
    q = jnp.concatenate(heads, axis=1)
    n = GROUP * Q_BLOCK
    row = lax.broadcasted_iota(I32, (LANES - HEAD_DIM, n), 0)
    lane_head = lax.broadcasted_iota(I32, (LANES - HEAD_DIM, n), 1) // Q_BLOCK
    slope = _group_slopes(g, lane_head)
    q0f = (q0_blocks).astype(F32)
    aug = jnp.where(row == 0, slope * 128.0,
                    jnp.where(row == 1, slope,
                              jnp.where(row == 2, -(slope * 128.0) * q0f, 0.0)))
    return jnp.concatenate([q.astype(BF16), aug.astype(BF16)], axis=0)


def _ada_kernel(c_ref, w_ref, b_ref, o_ref):
    c = c_ref[...]
    o_ref[0] = jnp.dot(_silu(c), w_ref[0], precision=HIGHEST, preferred_element_type=F32) + b_ref[0]


def _ada(c_all, w_ada, b_ada):
    depth = w_ada.shape[0]
    rows = c_all.shape[0]
    tn = 1024
    n = w_ada.shape[2]
    return pl.pallas_call(
        _ada_kernel,
        grid=(depth, n // tn),
        in_specs=[pl.BlockSpec((rows, D_MODEL), lambda l, j: (0, 0)),
                  pl.BlockSpec((1, D_MODEL, tn), lambda l, j: (l, 0, j)),
                  pl.BlockSpec((1, 1, tn), lambda l, j: (l, 0, j))],
        out_specs=pl.BlockSpec((1, rows, tn), lambda l, j: (l, 0, j)),
        out_shape=jax.ShapeDtypeStruct((depth, rows, n), F32),
        compiler_params=_cparams(("arbitrary", "arbitrary")),
        name="ada",
    )(c_all, w_ada, b_ada.reshape(depth, 1, n))


def _in_proj_kernel(x_ref, g_ref, sc_ref, sh_ref, wn_ref, wt_ref, lng_ref, lnb_ref, *out_refs, tm, natural_kv,
                    precise):
    kvt_ref, u_ref, v_ref, ga_ref, gb_ref, kaug_ref, qt_ref, vt_ref, gt_ref, kv_ref = out_refs
    i = pl.program_id(0)
    x = x_ref[...]
    ms = jnp.mean(x * x, axis=-1, keepdims=True)
    h = x * lax.rsqrt(ms + EPS) * g_ref[...]
    h = h * (1.0 + sc_ref[...]) + sh_ref[...]
    hb = h if precise else h.astype(BF16)

    def seg(a, b):
        return _mm(hb, wn_ref[:, a:b], precise)

    u_ref[...] = jax.nn.gelu(seg(WN_U, WN_V)).astype(u_ref.dtype)
    v = jax.nn.gelu(seg(WN_V, WN_GA))
    mu = jnp.mean(v, axis=-1, keepdims=True)
    var = jnp.mean(jnp.square(v - mu), axis=-1, keepdims=True)
    v_ref[...] = (v - mu) * lax.rsqrt(var + EPS) * lng_ref[...] + lnb_ref[...]
    ga_ref[...] = jax.nn.sigmoid(seg(WN_GA, WN_GB)).astype(ga_ref.dtype)
    gb_ref[...] = jax.nn.sigmoid(seg(WN_GB, WN_KAUG)).astype(gb_ref.dtype)

    pos = lax.broadcasted_iota(I32, (tm, LANES), 0) + i * tm
    lane = lax.broadcasted_iota(I32, (tm, LANES), 1)
    aug = _pos_aug(pos, lane)
    for j in range(4):
        k = seg(WN_KAUG + j * LANES, WN_KAUG + (j + 1) * LANES)
        kaug_ref[j] = (k + aug).astype(BF16)
    if natural_kv == "all":
        kv_ref[...] = seg(WN_KV, WN_COLS)
    else:
        for j in range(2):
            kv_ref[j] = seg(WN_KV + j * LANES, WN_KV + (j + 1) * LANES)

    zt = _mm_nt(wt_ref[...], hb, precise)
    qt_ref[...] = (zt[WT_Q:WT_KV] * (HEAD_DIM ** -0.5)).astype(qt_ref.dtype)
    kvt = zt[WT_KV:WT_G]
    kvt_ref[...] = kvt
    two = N_KV * HEAD_DIM
    vt_ref[0:two, :] = kvt[3 * two:4 * two].astype(BF16)
    vt_ref[two:2 * two, :] = kvt[5 * two:6 * two].astype(BF16)
    gt_ref[...] = jax.nn.sigmoid(zt[WT_G:WT_ROWS])


def _in_proj(x, g, sc, sh, wn, wt, lng, lnb, tm, natural_kv, precise):
    m = x.shape[0]
    act = F32 if precise else BF16
    mod_rows = sc.shape[0]
    mod_block = (1, D_MODEL) if mod_rows == 1 else (tm, D_MODEL)
    mod_map = (lambda i: (0, 0)) if mod_rows == 1 else (lambda i: (i, 0))
    row = lambda i: (i, 0)
    col = lambda i: (0, i)
    const = lambda i: (0, 0)
    out_shape = [
        jax.ShapeDtypeStruct((KV_COLS, m), F32),
        jax.ShapeDtypeStruct((m, D_SG), act),
        jax.ShapeDtypeStruct((m, D_SG), F32),
        jax.ShapeDtypeStruct((m, D_MODEL), act),
        jax.ShapeDtypeStruct((m, D_MODEL), act),
        jax.ShapeDtypeStruct((4, m, LANES), BF16),
        jax.ShapeDtypeStruct((D_NSA, m), act),
        jax.ShapeDtypeStruct((4 * HEAD_DIM, m), BF16),
        jax.ShapeDtypeStruct((32, m), F32),
    ]
    out_specs = [
        pl.BlockSpec((KV_COLS, tm), col),
        pl.BlockSpec((tm, D_SG), row),
        pl.BlockSpec((tm, D_SG), row),
        pl.BlockSpec((tm, D_MODEL), row),
        pl.BlockSpec((tm, D_MODEL), row),
        pl.BlockSpec((4, tm, LANES), lambda i: (0, i, 0)),
        pl.BlockSpec((D_NSA, tm), col),
        pl.BlockSpec((4 * HEAD_DIM, tm), col),
        pl.BlockSpec((32, tm), col),
    ]
    if natural_kv == "all":
        out_shape.append(jax.ShapeDtypeStruct((m, KV_COLS), F32))
        out_specs.append(pl.BlockSpec((tm, KV_COLS), row))
    else:
        out_shape.append(jax.ShapeDtypeStruct((2, m, LANES), F32))
        out_specs.append(pl.BlockSpec((2, tm, LANES), lambda i: (0, i, 0)))
    return pl.pallas_call(
        functools.partial(_in_proj_kernel, tm=tm, natural_kv=natural_kv, precise=precise),
        grid=(m // tm,),
        in_specs=[pl.BlockSpec((tm, D_MODEL), row),
                  pl.BlockSpec((1, D_MODEL), const),
                  pl.BlockSpec(mod_block, mod_map),
                  pl.BlockSpec(mod_block, mod_map),
                  pl.BlockSpec((D_MODEL, WN_COLS), const),
                  pl.BlockSpec((WT_ROWS, D_MODEL), const),
                  pl.BlockSpec((1, D_SG), const),
                  pl.BlockSpec((1, D_SG), const)],
        out_specs=tuple(out_specs),
        out_shape=tuple(out_shape),
        compiler_params=_cparams(("arbitrary",)),
        name="in_proj",
    )(x, g, sc, sh, wn, wt, lng, lnb)


def _pack_w_in(w):
    kv = w[:, OFF_KV:OFF_G]

    def kvcol(j, g):
        return kv[:, (j * N_KV + g) * HEAD_DIM:(j * N_KV + g + 1) * HEAD_DIM]

    zpad = jnp.zeros((D_MODEL, LANES - HEAD_DIM), w.dtype)
    kaug = [jnp.concatenate([kvcol(j, g), zpad], axis=1) for j in (2, 4) for g in range(N_KV)]
    wn = jnp.concatenate([w[:, OFF_U:OFF_V], w[:, OFF_V:OFF_GA], w[:, OFF_GA:OFF_GB],
                          w[:, OFF_GB:IN_COLS]] + kaug + [kv], axis=1)
    gpad = jnp.zeros((D_MODEL, 32 - 3 * N_HEADS), w.dtype)
    wt = jnp.concatenate([w[:, :OFF_KV], kv, w[:, OFF_G:OFF_U], gpad], axis=1).T
    return wn, wt


def _compress_weights(pe, w1, w2):
    kd = CMP_BLOCK * LANES
    pe_flat = jnp.concatenate([pe, pe], axis=-1).reshape(2, 1, kd)
    w1r = w1.reshape(2, CMP_BLOCK, HEAD_DIM, CMP_HIDDEN)
    w1p = jnp.einsum("gh,jtdn->jtgdhn", jnp.eye(N_KV, dtype=w1.dtype), w1r)
    w1p = w1p.reshape(2, kd, N_KV * CMP_HIDDEN).astype(BF16)
    w2p = jnp.pad(w2, ((0, 0), (0, 0), (0, LANES - HEAD_DIM))).astype(BF16)
    return pe_flat, w1p, w2p


def _compress_kernel(x_ref, pe_ref, w1_ref, w2_ref, w2vt_ref, kc_ref, vct_ref, *, half):
    parity = pl.program_id(0)
    row = lax.broadcasted_iota(I32, (half, LANES), 0)
    lane = lax.broadcasted_iota(I32, (half, LANES), 1)
    aug = _pos_aug((2 * row + parity) * CMP_BLOCK + (CMP_BLOCK - 1), lane)
    hid_k = _bdot(x_ref[0] + pe_ref[0], w1_ref[0])
    hid_v = _bdot(x_ref[1] + pe_ref[1], w1_ref[1])
    for g in range(N_KV):
        cols = slice(g * CMP_HIDDEN, (g + 1) * CMP_HIDDEN)
        kc_ref[g] = (_bdot(_silu(hid_k[:, cols]), w2_ref[0]) + aug).astype(BF16)
        vct_ref[g] = _dot_nt(w2vt_ref[...], _silu(hid_v[:, cols])).astype(BF16)


def _compress_prompt(kvc, pe, w1, w2):
    m = kvc.shape[1]
    nc = m // CMP_BLOCK
    half = nc // 2
    kd = CMP_BLOCK * LANES
    x = kvc.reshape(2, half, 2 * kd)
    pe_flat, w1p, w2p = _compress_weights(pe, w1, w2)
    w2vt = w2[1].T.astype(BF16)
    return pl.pallas_call(
        functools.partial(_compress_kernel, half=half),
        grid=(2,),
        in_specs=[pl.BlockSpec((2, half, kd), lambda p: (0, 0, p)),
                  pl.BlockSpec((2, 1, kd), lambda p: (0, 0, 0)),
                  pl.BlockSpec((2, kd, N_KV * CMP_HIDDEN), lambda p: (0, 0, 0)),
                  pl.BlockSpec((2, CMP_HIDDEN, LANES), lambda p: (0, 0, 0)),
                  pl.BlockSpec((HEAD_DIM, CMP_HIDDEN), lambda p: (0, 0))],
        out_specs=(pl.BlockSpec((N_KV, half, LANES), lambda p: (0, p, 0)),
                   pl.BlockSpec((N_KV, HEAD_DIM, half), lambda p: (0, 0, p))),
        out_shape=(jax.ShapeDtypeStruct((N_KV, nc, LANES), BF16),
                   jax.ShapeDtypeStruct((N_KV, HEAD_DIM, nc), BF16)),
        compiler_params=_cparams(("arbitrary",)),
        name="compress_prompt",
    )(x, pe_flat, w1p, w2p, w2vt)


def _topk_mask_rows(imp, k):
    rows = lax.broadcasted_iota(I32, imp.shape, 0)
    big = jnp.int32(2 ** 30)

    def body(_, v):
        m = jnp.max(v, axis=0, keepdims=True)
        first = jnp.min(jnp.where(v == m, rows, big), axis=0, keepdims=True)
        return jnp.where(rows == first, REMOVED, v)

    return jnp.where(lax.fori_loop(0, k, body, imp) < 0.5 * REMOVED, 1.0, 0.0)


def _visited_blocks(sel, limit):
    ns = sel.shape[0]
    ones_q = jnp.ones((SUBLANES, sel.shape[1]), BF16)
    blk = lax.broadcasted_iota(I32, (SUBLANES, ns), 1)
    flags = ((_dot_nt(ones_q, sel) > 0.5) & (blk < limit)).astype(F32)
    r = lax.broadcasted_iota(I32, (ns, ns), 0)
    c = lax.broadcasted_iota(I32, (ns, ns), 1)
    upper = jnp.where(r <= c, 1.0, 0.0)
    prefix = _bdot(flags, upper)
    before = jnp.where(prefix[0:1, :] <= r.astype(F32), 1.0, 0.0)
    ids = _dot_nt(jnp.ones((SUBLANES, ns), BF16), before)
    ids = jnp.minimum(ids, ns - 1.0).astype(I32)
    total = jnp.broadcast_to(prefix[:, ns - 1:ns], (SUBLANES, ns))
    return ids, total


def _select_tile(qt_ref, kc_ref, vct_ref, mask_ref, ids_ref, cnt_ref, ocmp_ref, i, nc, live):
    ns = nc // 2
    n = GROUP * Q_BLOCK
    q0 = i * Q_BLOCK
    r = lax.broadcasted_iota(I32, (live, 1), 0)
    cend = jnp.concatenate([2 * r, 2 * r + 1], axis=0) * CMP_BLOCK + (CMP_BLOCK - 1)
    qpos = q0 + lax.broadcasted_iota(I32, (1, n), 1) % Q_BLOCK
    valid = cend <= qpos
    blk = lax.broadcasted_iota(I32, (live, Q_BLOCK), 0)
    cur = (q0 + lax.broadcasted_iota(I32, (live, Q_BLOCK), 1)) // SEL_BLOCK
    forced = (blk == 0) | (blk == cur) | (blk == cur - 1)
    imps = []
    for g in range(N_KV):
        qa = _qt_aug(qt_ref.at[g * GROUP * HEAD_DIM:(g + 1) * GROUP * HEAD_DIM], g, i)
        kc = jnp.concatenate([kc_ref[g, 0:live], kc_ref[g, ns:ns + live]], axis=0)
        s = jnp.dot(kc, qa, preferred_element_type=F32)
        s = jnp.where(valid, s, NEG)
        m = jnp.max(s, axis=0, keepdims=True)
        p = jnp.where(valid, jnp.exp(s - m), 0.0)
        p = p / jnp.maximum(jnp.sum(p, axis=0, keepdims=True), 1e-30)
        vct = jnp.concatenate([vct_ref[g, :, 0:live], vct_ref[g, :, ns:ns + live]], axis=1)
        ocmp_ref[0, g] = jnp.dot(vct, p.astype(BF16), preferred_element_type=F32)
        ph = p[:, 0:Q_BLOCK]
        for h in range(1, GROUP):
            ph = ph + p[:, h * Q_BLOCK:(h + 1) * Q_BLOCK]
        imp = ph[:live] + ph[live:]
        imp = jnp.where(forced, FORCED, imp)
        imps.append(jnp.where(blk <= cur, imp, NEG))
    for g in range(N_KV):
        sel = jnp.where(blk <= cur, _topk_mask_rows(imps[g], min(SEL_TOPK, ns)), 0.0)
        if live < ns:
            sel = jnp.concatenate([sel, jnp.zeros((ns - live, Q_BLOCK), F32)], axis=0)
        mask_ref[0, g] = sel
        ids_ref[0, g], cnt_ref[0, g] = _visited_blocks(sel, 2 * i)


def _select_kernel(qt_ref, kc_ref, vct_ref, mask_ref, ids_ref, cnt_ref, ocmp_ref, *, nc):
    i = pl.program_id(0)
    ns = nc // 2
    refs = (qt_ref, kc_ref, vct_ref, mask_ref, ids_ref, cnt_ref, ocmp_ref)
    half = ns // 2
    if half % LANES == 0 and half >= SEL_TOPK:
        early = 2 * i + 1 < half

        @pl.when(early)
        def _():
            _select_tile(*refs, i, nc, half)

        @pl.when(jnp.logical_not(early))
        def _():
            _select_tile(*refs, i, nc, ns)
    else:
        _select_tile(*refs, i, nc, ns)


def _select(qt, kc, vct):
    m = qt.shape[1]
    nqb = m // Q_BLOCK
    nc = m // CMP_BLOCK
    ns = nc // 2
    n = GROUP * Q_BLOCK
    return pl.pallas_call(
        functools.partial(_select_kernel, nc=nc),
        grid=(nqb,),
        in_specs=[pl.BlockSpec((D_NSA, Q_BLOCK), lambda i: (0, i)),
                  pl.BlockSpec((N_KV, nc, LANES), lambda i: (0, 0, 0)),
                  pl.BlockSpec((N_KV, HEAD_DIM, nc), lambda i: (0, 0, 0))],
        out_specs=(pl.BlockSpec((1, N_KV, ns, Q_BLOCK), lambda i: (i, 0, 0, 0)),
                   pl.BlockSpec((1, N_KV, SUBLANES, ns), lambda i: (i, 0, 0, 0)),
                   pl.BlockSpec((1, N_KV, SUBLANES, ns), lambda i: (i, 0, 0, 0)),
                   pl.BlockSpec((1, N_KV, HEAD_DIM, n), lambda i: (i, 0, 0, 0))),
        out_shape=(jax.ShapeDtypeStruct((nqb, N_KV, ns, Q_BLOCK), F32),
                   jax.ShapeDtypeStruct((nqb, N_KV, SUBLANES, ns), I32),
                   jax.ShapeDtypeStruct((nqb, N_KV, SUBLANES, ns), F32),
                   jax.ShapeDtypeStruct((nqb, N_KV, HEAD_DIM, n), F32)),
        compiler_params=_cparams(("arbitrary",)),
        name="nsa_select",
    )(qt, kc, vct)


SEL_PER_STEP = 16
WIN_TILES_PER_STEP = 4


def _softmax_step(state, s, valid=None):
    m, l, _ = state
    if valid is not None:
        s = jnp.where(valid, s, NEG)
    m_new = jnp.maximum(m, jnp.max(s, axis=0, keepdims=True))
    alpha = jnp.exp(m - m_new)
    p = jnp.exp(s - m_new)
    if valid is not None:
        p = jnp.where(valid, p, 0.0)
    return m_new, alpha * l + jnp.sum(p, axis=0, keepdims=True), alpha, p.astype(BF16)


def _pv_lane_tiles(vt_tiles):
    vt = jnp.concatenate(vt_tiles, axis=1)
    return lambda pb: jnp.dot(vt, pb, preferred_element_type=F32)


def _pv_row_tiles(v_tiles):
    v = jnp.concatenate(v_tiles, axis=0)
    return lambda pb: lax.dot_general(v, pb, (((0,), (0,)), ((), ())), preferred_element_type=F32)


def _attend_kernel(cnt_ref, ids_ref, qt_ref, ksel_ref, vsel_ref, kwin_ref, vwin_ref, mask_ref,
                   ocmp_ref, gt_ref, o_ref, *, nqb, ns):
    g = pl.program_id(0)
    i = pl.program_id(1)
    q0 = i * Q_BLOCK
    qa = _qt_aug(qt_ref, g, i)
    qa_heads = [qa[:, h * Q_BLOCK:(h + 1) * Q_BLOCK] for h in range(GROUP)]
    empty = (jnp.full((1, Q_BLOCK), NEG, F32), jnp.zeros((1, Q_BLOCK), F32), jnp.zeros((HEAD_DIM, Q_BLOCK), F32))
    init = (empty,) * GROUP

    def update_jobs(jobs):
        chains = [(job, h) for job in jobs for h in range(GROUP)]
        scores = [jnp.dot(job[1], qa_heads[h], preferred_element_type=F32) for job, h in chains]
        scores = [s if job[3] is None else s + job[3] for s, (job, h) in zip(scores, chains)]
        parts = [_softmax_step(job[0][h], s, job[4]) for s, (job, h) in zip(scores, chains)]
        pvs = [job[2](part[3]) for part, (job, h) in zip(parts, chains)]
        new = [(part[0], part[1], part[2] * job[0][h][2] + pv) for part, pv, (job, h) in zip(parts, pvs, chains)]
        return [tuple(new[k * GROUP:(k + 1) * GROUP]) for k in range(len(jobs))]

    def heads_update(states, keys, pv_fn, bias=None, valid=None):
        return update_jobs([(states, keys, pv_fn, bias, valid)])[0]

    slot = g * nqb + i
    count = cnt_ref[slot]
    key_off = lax.broadcasted_iota(I32, (Q_BLOCK, 1), 0)
    query_off = lax.broadcasted_iota(I32, (1, Q_BLOCK), 1)
    causal = key_off <= query_off

    own = mask_ref[0, 0, pl.ds(pl.multiple_of(2 * i, 2), 2), :]
    chosen = jnp.concatenate([jnp.broadcast_to(own[0:1] > 0.5, (SEL_BLOCK, Q_BLOCK)),
                              jnp.broadcast_to(own[1:2] > 0.5, (SEL_BLOCK, Q_BLOCK))], axis=0)
    k_own = ksel_ref[0, pl.ds(pl.multiple_of(q0, Q_BLOCK), Q_BLOCK), :]
    kw_own = kwin_ref[0, pl.ds(pl.multiple_of(q0, Q_BLOCK), Q_BLOCK), :]
    first, win_states = update_jobs([
        (init, k_own, _pv_row_tiles([vsel_ref[0, 2 * i], vsel_ref[0, 2 * i + 1]]), None, chosen & causal),
        (init, kw_own, _pv_lane_tiles([vwin_ref[0, i]]), None, causal)])

    def sel_body(t, states):
        ks, vts, biases = [], [], []
        for u in range(SEL_PER_STEP):
            e = t * SEL_PER_STEP + u
            b = ids_ref[slot * ns + e]
            ks.append(ksel_ref[0, pl.ds(pl.multiple_of(b * SEL_BLOCK, SEL_BLOCK), SEL_BLOCK), :])
            row = mask_ref[0, 0, pl.ds(b, 1), :]
            live = (row > 0.5) & (e < count)
            biases.append(jnp.broadcast_to(jnp.where(live, 0.0, NEG), (SEL_BLOCK, Q_BLOCK)))
            vts.append(vsel_ref[0, b])
        return heads_update(states, jnp.concatenate(ks, axis=0), _pv_row_tiles(vts),
                            bias=jnp.concatenate(biases, axis=0))

    sel_states = lax.fori_loop(0, (count + (SEL_PER_STEP - 1)) // SEL_PER_STEP, sel_body, first)

    n_old = WINDOW // Q_BLOCK
    oldest_in_window = key_off > query_off
    for j0 in range(0, n_old, WIN_TILES_PER_STEP):
        ks, vts, biases = [], [], []
        for j in range(j0, min(j0 + WIN_TILES_PER_STEP, n_old)):
            tile = i - n_old + j
            held = jnp.maximum(tile, 0)
            ks.append(kwin_ref[0, pl.ds(pl.multiple_of(held * Q_BLOCK, Q_BLOCK), Q_BLOCK), :])
            inside = tile >= 0
            live = (oldest_in_window & inside) if j == 0 else inside
            biases.append(jnp.broadcast_to(jnp.where(live, 0.0, NEG), (Q_BLOCK, Q_BLOCK)))
            vts.append(vwin_ref[0, held])
        win_states = heads_update(win_states, jnp.concatenate(ks, axis=0), _pv_lane_tiles(vts),
                                  bias=jnp.concatenate(biases, axis=0))

    def head_out(h):
        def gate(branch):
            return gt_ref[pl.ds(branch * N_HEADS + g * GROUP + h, 1), :]
        _, l_s, acc_s = sel_states[h]
        _, l_w, acc_w = win_states[h]
        return (ocmp_ref[0, 0, :, h * Q_BLOCK:(h + 1) * Q_BLOCK] * gate(0)
                + acc_s / jnp.maximum(l_s, 1e-30) * gate(1) + acc_w / jnp.maximum(l_w, 1e-30) * gate(2))

    for hp in range(GROUP // 2):
        pair = jnp.concatenate([head_out(2 * hp), head_out(2 * hp + 1)], axis=0)
        o_ref[:, hp * LANES:(hp + 1) * LANES] = pair.T.astype(o_ref.dtype)


def _attend(cnt, ids, qt, kaug, vt_blocks_sel, vt_blocks_win, mask, ocmp, gt):
    m = qt.shape[1]
    nqb = m // Q_BLOCK
    ns = m // SEL_BLOCK
    n = GROUP * Q_BLOCK
    gh = GROUP * HEAD_DIM
    grid_spec = pltpu.PrefetchScalarGridSpec(
        num_scalar_prefetch=2,
        grid=(N_KV, nqb),
        in_specs=[pl.BlockSpec((gh, Q_BLOCK), lambda g, i, c, d: (g, i)),
                  pl.BlockSpec((1, m, LANES), lambda g, i, c, d: (g, 0, 0)),
                  pl.BlockSpec((1, ns, SEL_BLOCK, HEAD_DIM), lambda g, i, c, d: (g, 0, 0, 0)),
                  pl.BlockSpec((1, m, LANES), lambda g, i, c, d: (N_KV + g, 0, 0)),
                  pl.BlockSpec((1, nqb, HEAD_DIM, Q_BLOCK), lambda g, i, c, d: (g, 0, 0, 0)),
                  pl.BlockSpec((1, 1, ns, Q_BLOCK), lambda g, i, c, d: (i, g, 0, 0)),
                  pl.BlockSpec((1, 1, HEAD_DIM, n), lambda g, i, c, d: (i, g, 0, 0)),
                  pl.BlockSpec((32, Q_BLOCK), lambda g, i, c, d: (0, i))],
        out_specs=pl.BlockSpec((Q_BLOCK, gh), lambda g, i, c, d: (i, g)),
    )
    return pl.pallas_call(
        functools.partial(_attend_kernel, nqb=nqb, ns=ns),
        grid_spec=grid_spec,
        out_shape=jax.ShapeDtypeStruct((m, D_NSA), BF16),
        compiler_params=_cparams(("arbitrary", "arbitrary")),
        name="nsa_attend",
    )(cnt, ids, qt, kaug, vt_blocks_sel, kaug, vt_blocks_win, mask, ocmp, gt)


def _nsa_prompt(kvc, kaug, qt, vt, gt, pe, w1, w2):
    m = kvc.shape[1]
    nqb = m // Q_BLOCK
    ns = m // SEL_BLOCK
    kc, vct = _compress_prompt(kvc, pe, w1, w2)
    mask, ids8, cnt8, ocmp = _select(qt, kc, vct)
    ids = ids8[:, :, 0, :].transpose(1, 0, 2).reshape(-1)
    cnt = cnt8[:, :, 0, 0].T.astype(I32).reshape(-1)
    vt4 = vt.reshape(4, HEAD_DIM, m)
    vsel = vt4[:N_KV].reshape(N_KV, HEAD_DIM, ns, SEL_BLOCK).transpose(0, 2, 3, 1)
    vwin = vt4[N_KV:].reshape(N_KV, HEAD_DIM, nqb, Q_BLOCK).transpose(0, 2, 1, 3)
    return _attend(cnt, ids, qt, kaug, vsel, vwin, mask, ocmp, gt)


def _mix_kernel(x_ref, onsa_ref, u_ref, v_ref, ga_ref, gb_ref, g1_ref, wa_ref, wb_ref, wo_ref,
                sgw_ref, sgb_ref, o_ref, osg_ref, *, tm, chunked, precise):
    if chunked:
        lane = lax.broadcasted_iota(I32, (CHUNK, LANES), 1)
        for c in range(tm // CHUNK):
            rows = slice(c * CHUNK, (c + 1) * CHUNK)
            for pr in range(N_SG // 2):
                cols = slice(pr * LANES, (pr + 1) * LANES)
                vp = v_ref[rows, cols].astype(BF16)
                a = jnp.dot(sgw_ref[2 * pr], vp, preferred_element_type=F32)
                b = jnp.dot(sgw_ref[2 * pr + 1], vp, preferred_element_type=F32)
                mix = jnp.where(lane < SG_DIM, a, b) + sgb_ref[:, cols]
                osg_ref[rows, cols] = (u_ref[rows, cols].astype(F32) * mix).astype(BF16)
    else:
        mix = v_ref[...] * sgw_ref[...] + sgb_ref[...]
        osg_ref[...] = (u_ref[...].astype(F32) * mix).astype(osg_ref.dtype)
    a = _mm(onsa_ref[...], wa_ref[...], precise)
    b = _mm(osg_ref[...], wb_ref[...], precise)
    merged = ga_ref[...].astype(F32) * a + gb_ref[...].astype(F32) * b
    y = _mm(merged, wo_ref[...], precise)
    o_ref[...] = x_ref[...] + g1_ref[...] * y


def _mix_out(x, onsa, u, v, ga, gb, g1, wa, wb, wo, sgw, sgb, tm, chunked, precise):
    m = x.shape[0]
    mod_rows = g1.shape[0]
    mod_block = (1, D_MODEL) if mod_rows == 1 else (tm, D_MODEL)
    mod_map = (lambda i: (0, 0)) if mod_rows == 1 else (lambda i: (i, 0))
    row = lambda i: (i, 0)
    const2 = lambda i: (0, 0)
    if chunked:
        sg_specs = [pl.BlockSpec((N_SG, CHUNK, CHUNK), lambda i: (0, 0, 0)),
                    pl.BlockSpec((CHUNK, D_SG), const2)]
    else:
        sg_specs = [pl.BlockSpec((1, D_SG), const2), pl.BlockSpec((1, D_SG), const2)]
    return pl.pallas_call(
        functools.partial(_mix_kernel, tm=tm, chunked=chunked, precise=precise),
        grid=(m // tm,),
        in_specs=[pl.BlockSpec((tm, D_MODEL), row),
                  pl.BlockSpec((tm, D_NSA), row),
                  pl.BlockSpec((tm, D_SG), row),
                  pl.BlockSpec((tm, D_SG), row),
                  pl.BlockSpec((tm, D_MODEL), row),
                  pl.BlockSpec((tm, D_MODEL), row),
                  pl.BlockSpec(mod_block, mod_map),
                  pl.BlockSpec((D_NSA, D_MODEL), const2),
                  pl.BlockSpec((D_SG, D_MODEL), const2),
                  pl.BlockSpec((D_MODEL, D_MODEL), const2)] + sg_specs,
        out_specs=pl.BlockSpec((tm, D_MODEL), row),
        out_shape=jax.ShapeDtypeStruct((m, D_MODEL), F32),
        scratch_shapes=[pltpu.VMEM((tm, D_SG), F32 if precise else BF16)],
        compiler_params=_cparams(("arbitrary",)),
        name="mix_out",
    )(x, onsa, u, v, ga, gb, g1, wa, wb, wo, sgw, sgb)


def _top2_combine(logits):
    lane = lax.broadcasted_iota(I32, logits.shape, 1)
    big = jnp.int32(2 ** 30)
    z = jnp.where(lane < N_EXPERTS, logits, -jnp.inf)
    t1 = jnp.max(z, axis=-1, keepdims=True)
    i1 = jnp.min(jnp.where(z == t1, lane, big), axis=-1, keepdims=True)
    z2 = jnp.where(lane == i1, -jnp.inf, z)
    t2 = jnp.max(z2, axis=-1, keepdims=True)
    i2 = jnp.min(jnp.where(z2 == t2, lane, big), axis=-1, keepdims=True)
    e = jnp.exp(t2 - t1)
    den = 1.0 + e
    return jnp.where(lane == i1, 1.0 / den, 0.0) + jnp.where(lane == i2, e / den, 0.0)


FFN_COL_SPLITS = ((0, 6 * LANES), (6 * LANES, D_FF_CHUNK))


def _ffn_kernel(x_ref, g_ref, sc_ref, sh_ref, g2_ref, wa_ref, wb_ref, wd_ref, rw_ref, rb_ref, gf_ref,
                o_ref, h_ref, acc_ref, comb_ref, *, routed, final_norm, n_chunks, precise):
    e = pl.program_id(1)

    @pl.when(e == 0)
    def _():
        x = x_ref[...]
        ms = jnp.mean(x * x, axis=-1, keepdims=True)
        h = x * lax.rsqrt(ms + EPS) * g_ref[...]
        h = h * (1.0 + sc_ref[...]) + sh_ref[...]
        h_ref[...] = h.astype(h_ref.dtype)
        acc_ref[...] = jnp.zeros_like(acc_ref)
        if routed:
            logits = jnp.dot(h, rw_ref[...], precision=HIGHEST, preferred_element_type=F32) + rb_ref[...]
            comb_ref[...] = _top2_combine(logits)

    hb = h_ref[...]
    y = None
    for c0, c1 in FFN_COL_SPLITS:
        a = _mm(hb, wa_ref[0, :, c0:c1], precise)
        b = _mm(hb, wb_ref[0, :, c0:c1], precise)
        part = _mm(_silu(a) * b, wd_ref[0, c0:c1, :], precise)
        y = part if y is None else y + part
    if routed:
        lane = lax.broadcasted_iota(I32, comb_ref.shape, 1)
        w = jnp.sum(jnp.where(lane == e, comb_ref[...], 0.0), axis=-1, keepdims=True)
        acc_ref[...] += w * y
    else:
        acc_ref[...] += y

    @pl.when(e == n_chunks - 1)
    def _():
        out = x_ref[...] + g2_ref[...] * acc_ref[...]
        if final_norm:
            ms = jnp.mean(out * out, axis=-1, keepdims=True)
            out = out * lax.rsqrt(ms + EPS) * gf_ref[...]
        o_ref[...] = out


def _ffn(x, g, sc, sh, g2, gu, wd, rw, rb, gf, tm, routed, final_norm, precise):
    m = x.shape[0]
    n_chunks = wd.shape[0]
    if routed:
        a_map, b_map = (lambda i, e: (e, 0, 0)), (lambda i, e: (e, 0, 1))
    else:
        a_map, b_map = (lambda i, e: (0, 0, e)), (lambda i, e: (0, 0, n_chunks + e))
    mod_rows = sc.shape[0]
    mod_block = (1, D_MODEL) if mod_rows == 1 else (tm, D_MODEL)
    mod_map = (lambda i, e: (0, 0)) if mod_rows == 1 else (lambda i, e: (i, 0))
    row = lambda i, e: (i, 0)
    const2 = lambda i, e: (0, 0)
    return pl.pallas_call(
        functools.partial(_ffn_kernel, routed=routed, final_norm=final_norm, n_chunks=n_chunks,
                          precise=precise),
        grid=(m // tm, n_chunks),
        in_specs=[pl.BlockSpec((tm, D_MODEL), row),
                  pl.BlockSpec((1, D_MODEL), const2),
                  pl.BlockSpec(mod_block, mod_map),
                  pl.BlockSpec(mod_block, mod_map),
                  pl.BlockSpec(mod_block, mod_map),
                  pl.BlockSpec((1, D_MODEL, D_FF_CHUNK), a_map),
                  pl.BlockSpec((1, D_MODEL, D_FF_CHUNK), b_map),
                  pl.BlockSpec((1, D_FF_CHUNK, D_MODEL), lambda i, e: (e, 0, 0)),
                  pl.BlockSpec((D_MODEL, LANES), const2),
                  pl.BlockSpec((1, LANES), const2),
                  pl.BlockSpec((1, D_MODEL), const2)],
        out_specs=pl.BlockSpec((tm, D_MODEL), row),
        out_shape=jax.ShapeDtypeStruct((m, D_MODEL), F32),
        scratch_shapes=[pltpu.VMEM((tm, D_MODEL), F32 if precise else BF16),
                        pltpu.VMEM((tm, D_MODEL), F32),
                        pltpu.VMEM((tm, LANES), F32)],
        compiler_params=_cparams(("arbitrary", "arbitrary")),
        name="ffn",
    )(x, g, sc, sh, g2, gu, gu, wd, rw, rb, gf)


PAGES_PER_STEP = 32
BLOCKS_PER_PAGE = PAGE_SIZE // CMP_BLOCK


def _compress_sample_kernel(pt_ref, cache_ref, pe_ref, w1_ref, w2_ref, kc_ref, vc_ref, pages_ref, sem_ref,
                            rows_ref, x_ref, *, layer, n_steps, total_steps):
    s = pl.program_id(1)
    step = pl.program_id(0) * n_steps + s
    slot = step % 2
    nb = PAGES_PER_STEP * BLOCKS_PER_PAGE
    half = nb // 2

    def page_copy(at_step, at_slot, p):
        page = pt_ref[at_step * PAGES_PER_STEP + p]
        return pltpu.make_async_copy(cache_ref.at[layer, page, pl.ds(0, 2)], pages_ref.at[at_slot, p],
                                     sem_ref.at[at_slot])

    def start_pages(at_step, at_slot):
        for p in range(PAGES_PER_STEP):
            page_copy(at_step, at_slot, p).start(priority=p % 2)

    @pl.when(step == 0)
    def _():
        start_pages(step, slot)

    @pl.when(step + 1 < total_steps)
    def _():
        start_pages(step + 1, 1 - slot)

    for p in range(PAGES_PER_STEP):
        page_copy(step, slot, p).wait()

    eye = (lax.broadcasted_iota(I32, (PAGE_SIZE, PAGE_SIZE), 0)
           == lax.broadcasted_iota(I32, (PAGE_SIZE, PAGE_SIZE), 1)).astype(BF16)
    for j in range(2):
        for p in range(PAGES_PER_STEP):
            tile = pages_ref[slot, p, j].reshape(N_KV * HEAD_DIM, PAGE_SIZE) + pe_ref[j]
            rows_ref[j, p * PAGE_SIZE:(p + 1) * PAGE_SIZE, :] = _dot_nt(eye, tile)
    for t in range(CMP_BLOCK):
        for j in range(2):
            even = rows_ref[j, pl.ds(t, half, stride=2 * CMP_BLOCK), :]
            odd = rows_ref[j, pl.ds(CMP_BLOCK + t, half, stride=2 * CMP_BLOCK), :]
            x_ref[j, :, t * LANES:(t + 1) * LANES] = jnp.concatenate([even, odd], axis=0).astype(BF16)
    row = lax.broadcasted_iota(I32, (nb, LANES), 0)
    lane = lax.broadcasted_iota(I32, (nb, LANES), 1)
    c = s * nb + 2 * (row % half) + row // half
    aug = _pos_aug(c * CMP_BLOCK + (CMP_BLOCK - 1), lane)
    hid_k = jnp.dot(x_ref[0], w1_ref[0], preferred_element_type=F32)
    hid_v = jnp.dot(x_ref[1], w1_ref[1], preferred_element_type=F32)
    for g in range(N_KV):
        cols = slice(g * CMP_HIDDEN, (g + 1) * CMP_HIDDEN)
        hk = _silu(hid_k[:, cols]).astype(BF16)
        kc = (jnp.dot(hk, w2_ref[0], preferred_element_type=F32) + aug).astype(BF16)
        kc_ref[0, g, 0] = kc[:half]
        kc_ref[0, g, 1] = kc[half:]
        hv = _silu(hid_v[:, cols]).astype(BF16)
        vc = jnp.dot(hv, w2_ref[1], preferred_element_type=F32).astype(BF16)
        vc_ref[0, g, 0] = vc[:half]
        vc_ref[0, g, 1] = vc[half:]


def _compress_sample(cache_t, layer, page_table, pe, w1, w2):
    n_seq, n_pages = page_table.shape
    n_steps = n_pages // PAGES_PER_STEP
    nb = PAGES_PER_STEP * BLOCKS_PER_PAGE
    half = nb // 2
    nc_half = n_pages * BLOCKS_PER_PAGE // 2
    pe_flat, w1p, w2p = _compress_weights(pe, w1, w2)
    pe_t = jnp.tile(pe.transpose(0, 2, 1), (1, N_KV, BLOCKS_PER_PAGE))

    grid_spec = pltpu.PrefetchScalarGridSpec(
        num_scalar_prefetch=1,
        grid=(n_seq, n_steps),
        in_specs=[
            pl.BlockSpec(memory_space=pl.ANY),
            pl.BlockSpec((2, N_KV * HEAD_DIM, PAGE_SIZE), lambda b, s, pt: (0, 0, 0)),
            pl.BlockSpec((2, CMP_BLOCK * LANES, N_KV * CMP_HIDDEN), lambda b, s, pt: (0, 0, 0)),
            pl.BlockSpec((2, CMP_HIDDEN, LANES), lambda b, s, pt: (0, 0, 0))],
        out_specs=(pl.BlockSpec((1, N_KV, 2, half, LANES), lambda b, s, pt: (b, 0, 0, s, 0)),
                   pl.BlockSpec((1, N_KV, 2, half, LANES), lambda b, s, pt: (b, 0, 0, s, 0))),
        scratch_shapes=[pltpu.VMEM((2, PAGES_PER_STEP, 2, N_KV, HEAD_DIM, PAGE_SIZE), F32),
                        pltpu.SemaphoreType.DMA((2,)),
                        pltpu.VMEM((2, PAGES_PER_STEP * PAGE_SIZE, LANES), F32),
                        pltpu.VMEM((2, nb, CMP_BLOCK * LANES), BF16)],
    )
    kc, vc = pl.pallas_call(
        functools.partial(_compress_sample_kernel, layer=layer, n_steps=n_steps, total_steps=n_seq * n_steps),
        grid_spec=grid_spec,
        out_shape=(jax.ShapeDtypeStruct((n_seq, N_KV, 2, nc_half, LANES), BF16),
                   jax.ShapeDtypeStruct((n_seq, N_KV, 2, nc_half, LANES), BF16)),
        compiler_params=_cparams(("arbitrary", "arbitrary")),
        name="compress_sample",
    )(page_table.reshape(-1), cache_t, pe_t, w1p, w2p)
    return (kc.reshape(n_seq, N_KV, 2 * nc_half, LANES), vc.reshape(n_seq, N_KV, 2 * nc_half, LANES))


def _row_slopes(shape):
    head = lax.broadcasted_iota(I32, shape, 0)
    out = jnp.zeros(shape, F32)
    for h in range(N_HEADS):
        out = jnp.where(head == h, 2.0 ** (-8.0 * (h + 1) / N_HEADS), out)
    return out


def _sample_select_kernel(q_ref, kc_ref, vc_ref, ocmp_ref, imp_ref, *, nc):
    q = q_ref[0]
    ns = nc // 2
    rowgroup = lax.broadcasted_iota(I32, (N_HEADS, 1), 0) // GROUP
    o = jnp.zeros((N_HEADS, LANES), F32)
    for g in range(N_KV):
        s = _dot_nt(q, kc_ref[0, g])
        m = jnp.max(s, axis=-1, keepdims=True)
        p = jnp.exp(s - m)
        p = p / jnp.maximum(jnp.sum(p, axis=-1, keepdims=True), 1e-30)
        og = jnp.dot(p.astype(BF16), vc_ref[0, g], preferred_element_type=F32)
        mine = rowgroup == g
        o = jnp.where(mine, og, o)
        ph = jnp.sum(jnp.where(mine, p, 0.0), axis=0, keepdims=True)
        imp_ref[0, pl.ds(g, 1), :] = ph[:, :ns] + ph[:, ns:]
    ocmp_ref[0] = o


def _sample_select(q_aug, kc, vc):
    n_seq, _, nc, _ = kc.shape
    ns = nc // 2
    return pl.pallas_call(
        functools.partial(_sample_select_kernel, nc=nc),
        grid=(n_seq,),
        in_specs=[pl.BlockSpec((1, N_HEADS, LANES), lambda b: (b, 0, 0)),
                  pl.BlockSpec((1, N_KV, nc, LANES), lambda b: (b, 0, 0, 0)),
                  pl.BlockSpec((1, N_KV, nc, LANES), lambda b: (b, 0, 0, 0))],
        out_specs=(pl.BlockSpec((1, N_HEADS, LANES), lambda b: (b, 0, 0)),
                   pl.BlockSpec((1, N_KV, ns), lambda b: (b, 0, 0))),
        out_shape=(jax.ShapeDtypeStruct((n_seq, N_HEADS, LANES), F32),
                   jax.ShapeDtypeStruct((n_seq, N_KV, ns), F32)),
        compiler_params=_cparams(("arbitrary",)),
        name="sample_select",
    )(q_aug, kc, vc)


def _sample_topk_kernel(imp_ref, ids_ref, *, ns, k):
    imp = imp_ref[...]
    lane = lax.broadcasted_iota(I32, imp.shape, 1)
    out_lane = lax.broadcasted_iota(I32, ids_ref.shape, 1)
    big = jnp.int32(2 ** 30)
    v = jnp.where((lane == 0) | (lane == ns - 1), FORCED, imp)
    ids = jnp.zeros(ids_ref.shape, I32)
    for t in range(k):
        m = jnp.max(v, axis=-1, keepdims=True)
        first = jnp.min(jnp.where(v == m, lane, big), axis=-1, keepdims=True)
        v = jnp.where(lane == first, REMOVED, v)
        ids = jnp.where(out_lane == t, first, ids)
    ids_ref[...] = ids


def _sample_topk(imp, k):
    rows, ns = imp.shape
    return pl.pallas_call(
        functools.partial(_sample_topk_kernel, ns=ns, k=k),
        out_shape=jax.ShapeDtypeStruct((rows, LANES), I32),
        name="sample_topk",
    )(imp)


SEL_PAST = SEL_TOPK - 1


def _sample_attend_kernel(pt_ref, ids_ref, *refs, past_len):
    del pt_ref
    nblk = N_KV * SEL_PAST
    blocks = refs[:nblk]
    win_ref, new_ref, q_ref, ocmp_ref, gate_ref, o_ref = refs[nblk:]
    b = pl.program_id(0)
    slope = _row_slopes((N_HEADS, 1))
    rowgroup = lax.broadcasted_iota(I32, (N_HEADS, 1), 0) // GROUP
    gates = gate_ref[0]
    w_buf = win_ref.shape[-1]
    q = q_ref[0]
    qf = q.astype(F32)
    lane = lax.broadcasted_iota(I32, (1, PAGE_SIZE), 1)
    o = jnp.zeros((N_HEADS, HEAD_DIM), F32)
    for g in range(N_KV):
        def new_row(j):
            return new_ref[0, j * N_KV + g:j * N_KV + g + 1, :].astype(BF16).astype(F32)

        s_new = jnp.sum(qf * new_row(2), axis=-1, keepdims=True)
        scores, oks = [], []
        for t in range(SEL_PAST):
            blk = ids_ref[(b * N_KV + g) * SEL_TOPK + t]
            kt = blocks[g * SEL_PAST + t][0].astype(BF16)
            in_blk = (lane // SEL_BLOCK) == (blk % 2)
            pos = (blk // 2) * PAGE_SIZE + lane
            s = jnp.dot(q, kt, preferred_element_type=F32) - slope * (past_len - pos).astype(F32)
            scores.append(jnp.where(in_blk, s, NEG))
            oks.append(in_blk)
        m = s_new
        for s in scores:
            m = jnp.maximum(m, jnp.max(s, axis=-1, keepdims=True))
        p_new = jnp.exp(s_new - m)
        den = p_new
        acc = p_new * new_row(3)
        for t in range(SEL_PAST):
            p = jnp.where(oks[t], jnp.exp(scores[t] - m), 0.0)
            den = den + jnp.sum(p, axis=-1, keepdims=True)
            acc = acc + _dot_nt(p, blocks[g * SEL_PAST + t][1])
        o_sel = acc / jnp.maximum(den, 1e-30)
        kwt = win_ref[0, g].astype(BF16)
        dist = w_buf - lax.broadcasted_iota(I32, (1, w_buf), 1)
        ok = dist < WINDOW
        s = jnp.where(ok, jnp.dot(q, kwt, preferred_element_type=F32) - slope * dist.astype(F32), NEG)
        s_new = jnp.sum(qf * new_row(4), axis=-1, keepdims=True)
        m = jnp.maximum(s_new, jnp.max(s, axis=-1, keepdims=True))
        p = jnp.where(ok, jnp.exp(s - m), 0.0)
        p_new = jnp.exp(s_new - m)
        den = p_new + jnp.sum(p, axis=-1, keepdims=True)
        o_win = (p_new * new_row(5) + _dot_nt(p, win_ref[1, g])) / jnp.maximum(den, 1e-30)
        og = gates[:, 0:1] * ocmp_ref[0][:, :HEAD_DIM] + gates[:, 1:2] * o_sel + gates[:, 2:3] * o_win
        o = jnp.where(rowgroup == g, og, o)
    o_ref[0] = o


def _sample_attend(cache_t, win_t, layer, page_table, ids, kv_new, q, ocmp, gates, past_len):
    n_seq, n_pages = page_table.shape
    w_buf = win_t.shape[-1]
    ids3 = ids.reshape(n_seq, N_KV, SEL_TOPK)
    phys = jnp.take_along_axis(page_table[:, None, :], ids3 // 2, axis=-1)
    phys = jnp.pad(phys, ((0, 1), (0, 0), (0, 0))).astype(I32)

    def blk_map(g, t):
        return lambda b, ph, idr: (layer, ph[(b * N_KV + g) * SEL_TOPK + t], 1, g, 0, 0)

    grid_spec = pltpu.PrefetchScalarGridSpec(
        num_scalar_prefetch=2,
        grid=(n_seq,),
        in_specs=[pl.BlockSpec((None, None, 2, None, HEAD_DIM, PAGE_SIZE), blk_map(g, t))
                  for g in range(N_KV) for t in range(SEL_PAST)] + [
            pl.BlockSpec((None, None, 2, N_KV, HEAD_DIM, w_buf), lambda b, pt, idr: (layer, b, 0, 0, 0, 0)),
            pl.BlockSpec((1, 6 * N_KV, HEAD_DIM), lambda b, pt, idr: (b, 0, 0)),
            pl.BlockSpec((1, N_HEADS, HEAD_DIM), lambda b, pt, idr: (b, 0, 0)),
            pl.BlockSpec((1, N_HEADS, LANES), lambda b, pt, idr: (b, 0, 0)),
            pl.BlockSpec((1, N_HEADS, LANES), lambda b, pt, idr: (b, 0, 0))],
        out_specs=pl.BlockSpec((1, N_HEADS, HEAD_DIM), lambda b, pt, idr: (b, 0, 0)),
    )
    return pl.pallas_call(
        functools.partial(_sample_attend_kernel, past_len=past_len),
        grid_spec=grid_spec,
        out_shape=jax.ShapeDtypeStruct((n_seq, N_HEADS, HEAD_DIM), F32),
        compiler_params=_cparams(("arbitrary",)),
        name="sample_attend",
    )(phys.reshape(-1), ids.reshape(-1), *([cache_t] * (N_KV * SEL_PAST)), win_t, kv_new, q, ocmp, gates)


def _nsa_sample(cache_t, win_t, layer, page_table, kv_new, qt, gt, pe, w1, w2):
    n_seq, n_pages = page_table.shape
    past_len = n_pages * PAGE_SIZE
    kc, vc = _compress_sample(cache_t, layer, page_table, pe, w1, w2)
    q = qt.T.reshape(n_seq, N_HEADS, HEAD_DIM).astype(BF16)
    slopes = (2.0 ** (-8.0 * jnp.arange(1, N_HEADS + 1, dtype=F32) / N_HEADS)).reshape(1, N_HEADS, 1)
    aug = jnp.concatenate([slopes * 128.0, slopes, -(slopes * 128.0) * (past_len // 128),
                           jnp.zeros((1, N_HEADS, LANES - HEAD_DIM - 3), F32)], axis=-1)
    q_aug = jnp.concatenate([q, jnp.broadcast_to(aug, (n_seq, N_HEADS, LANES - HEAD_DIM)).astype(BF16)], axis=-1)
    ocmp, imp = _sample_select(q_aug, kc, vc)
    ids = _sample_topk(imp.reshape(n_seq * N_KV, -1), SEL_PAST)[:, :SEL_TOPK]
    gates = gt[:3 * N_HEADS].reshape(3, N_HEADS, n_seq).transpose(2, 1, 0)
    gates = jnp.pad(gates, ((0, 0), (0, 0), (0, LANES - 3)))
    o = _sample_attend(cache_t, win_t, layer, page_table, ids, kv_new.reshape(n_seq, 6 * N_KV, HEAD_DIM), q,
                       ocmp, gates, past_len)
    return o.reshape(n_seq, D_NSA)


def _sg_chunk_params(sg_w, sg_b):
    w = jnp.tril(sg_w).astype(BF16)
    bias = jnp.repeat(sg_b.T, SG_DIM, axis=1)
    return w, bias


def _pad_rows(a, mult):
    pad = (-a.shape[0]) % mult
    return jnp.pad(a, ((0, pad),) + ((0, 0),) * (a.ndim - 1)) if pad else a


def _prep_weights(norm_mix_g, norm_ffn_g, norm_final_g, w_in, cmp_pe, cmp_w1, cmp_w2, sg_norm_g, sg_norm_b,
                  sg_w, sg_b, w_branch_nsa, w_branch_sg, w_out, ffn_w_gu, ffn_w_down, router_w, router_b,
                  moe_w_gu, moe_w_down):
    depth = w_in.shape[0]
    layers = []
    for i in range(depth):
        wn, wt = _pack_w_in(w_in[i])
        sgw_chunk, sgb_chunk = _sg_chunk_params(sg_w[i], sg_b[i])
        lw = {
            "norm_mix": norm_mix_g[i].reshape(1, D_MODEL), "norm_ffn": norm_ffn_g[i].reshape(1, D_MODEL),
            "norm_final": norm_final_g.reshape(1, D_MODEL),
            "sg_norm_g": sg_norm_g[i].reshape(1, D_SG), "sg_norm_b": sg_norm_b[i].reshape(1, D_SG),
            "cmp_pe": cmp_pe[i], "cmp_w1": cmp_w1[i], "cmp_w2": cmp_w2[i],
            "f32": {"wn": wn, "wt": wt, "wa": w_branch_nsa[i], "wb": w_branch_sg[i], "wo": w_out[i]},
            "bf16": {"wn": wn.astype(BF16), "wt": wt.astype(BF16), "wa": w_branch_nsa[i].astype(BF16),
                     "wb": w_branch_sg[i].astype(BF16), "wo": w_out[i].astype(BF16)},
            "sgw_chunk": sgw_chunk, "sgb_chunk": sgb_chunk,
            "sgw_first": jnp.repeat(sg_w[i][:, 0, 0], SG_DIM).reshape(1, D_SG),
            "sgb_first": jnp.repeat(sg_b[i][:, 0], SG_DIM).reshape(1, D_SG),
            "routed": i % 2 == 1, "final": i == depth - 1,
        }
        if i % 2 == 0:
            gu = ffn_w_gu[i // 2][None]
            wd = ffn_w_down[i // 2].reshape(-1, D_FF_CHUNK, D_MODEL)
            lw["router_w"] = jnp.zeros((D_MODEL, LANES), F32)
            lw["router_b"] = jnp.zeros((1, LANES), F32)
        else:
            gu, wd = moe_w_gu[i // 2], moe_w_down[i // 2]
            lw["router_w"] = jnp.pad(router_w[i // 2], ((0, 0), (0, LANES - N_EXPERTS)))
            lw["router_b"] = jnp.pad(router_b[i // 2], (0, LANES - N_EXPERTS)).reshape(1, LANES)
        lw["f32"].update(gu=gu, wd=wd)
        lw["bf16"].update(gu=gu.astype(BF16), wd=wd.astype(BF16))
        layers.append(lw)
    return layers


def _prompt_layer(x, mod, lw, tm):
    sh1, sc1, g1, sh2, sc2, g2 = mod
    w = lw["bf16"]
    kvt, u, v, ga, gb, kaug, qt, vt, gt, kvc = _in_proj(x, lw["norm_mix"], sc1, sh1, w["wn"], w["wt"],
                                                         lw["sg_norm_g"], lw["sg_norm_b"], tm, "cmp", False)
    onsa = _nsa_prompt(kvc, kaug, qt, vt, gt, lw["cmp_pe"], lw["cmp_w1"], lw["cmp_w2"])
    x = _mix_out(x, onsa, u, v, ga, gb, g1, w["wa"], w["wb"], w["wo"], lw["sgw_chunk"], lw["sgb_chunk"],
                 lw["ffn_tm"], True, False)
    x = _ffn(x, lw["norm_ffn"], sc2, sh2, g2, w["gu"], w["wd"], lw["router_w"], lw["router_b"],
             lw["norm_final"], lw["ffn_tm"], lw["routed"], lw["final"], False)
    return x, kvt


def _sample_layer(x, mod, lw, cache_t, win_t, layer, page_table):
    sh1, sc1, g1, sh2, sc2, g2 = mod
    tm = x.shape[0]
    w = lw["f32"]
    kvt, u, v, ga, gb, _, qt, _, gt, kv = _in_proj(x, lw["norm_mix"], sc1, sh1, w["wn"], w["wt"],
                                                   lw["sg_norm_g"], lw["sg_norm_b"], tm, "all", True)
    onsa = _nsa_sample(cache_t, win_t, layer, page_table, kv, qt, gt, lw["cmp_pe"], lw["cmp_w1"], lw["cmp_w2"])
    x = _mix_out(x, onsa, u, v, ga, gb, g1, w["wa"], w["wb"], w["wo"], lw["sgw_first"], lw["sgb_first"],
                 tm, False, True)
    x = _ffn(x, lw["norm_ffn"], sc2, sh2, g2, w["gu"], w["wd"], lw["router_w"], lw["router_b"],
             lw["norm_final"], tm, lw["routed"], lw["final"], True)
    return x, kv, kvt, v


PROMPT_ROW_TILE = 512
PROMPT_FFN_ROW_TILE = 1024


def kernel(x_prompt, x_sample, cache_kv, state_win, page_table, c_prompt, c_sample, norm_mix_g, norm_ffn_g,
           norm_final_g, w_ada, b_ada, w_in, cmp_pe, cmp_w1, cmp_w2, sg_norm_g, sg_norm_b, sg_w, sg_b,
           w_branch_nsa, w_branch_sg, w_out, ffn_w_gu, ffn_w_down, router_w, router_b, moe_w_gu, moe_w_down):
    batch, seq, _ = x_prompt.shape
    n_seq, dec_seq, _ = x_sample.shape
    depth = w_in.shape[0]
    assert batch == 1 and dec_seq == 1
    assert seq % PROMPT_FFN_ROW_TILE == 0 and seq // Q_BLOCK <= 256
    past_len = page_table.shape[1] * PAGE_SIZE
    assert past_len % CHUNK == 0 and past_len // LANES <= 256
    assert state_win.shape[2] == WINDOW and page_table.shape[1] % PAGES_PER_STEP == 0

    layers = _prep_weights(norm_mix_g, norm_ffn_g, norm_final_g, w_in, cmp_pe, cmp_w1, cmp_w2, sg_norm_g,
                           sg_norm_b, sg_w, sg_b, w_branch_nsa, w_branch_sg, w_out, ffn_w_gu, ffn_w_down,
                           router_w, router_b, moe_w_gu, moe_w_down)
    c_all = _pad_rows(jnp.concatenate([c_prompt, c_sample], axis=0), SUBLANES)
    mods = _ada(c_all, w_ada, b_ada)

    cache_t = cache_kv.transpose(0, 1, 3, 4, 5, 2)
    win_t = state_win.transpose(0, 1, 3, 4, 5, 2)

    xp = x_prompt[0]
    xs = x_sample[:, 0]
    kv_p, kv_s, win_p, win_new, sgv_s = [], [], [], [], []
    w_keep = min(WINDOW, seq)
    kv_rows = 4 * N_KV * HEAD_DIM
    for i in range(depth):
        lw = dict(layers[i])
        lw["ffn_tm"] = PROMPT_FFN_ROW_TILE
        mod_p = tuple(mods[i, 0:1, j * D_MODEL:(j + 1) * D_MODEL] for j in range(6))
        mod_s = tuple(mods[i, 1:1 + n_seq, j * D_MODEL:(j + 1) * D_MODEL] for j in range(6))
        xp, kvt_p = _prompt_layer(xp, mod_p, lw, PROMPT_ROW_TILE)
        xs, kvs, kvt_s, v_s = _sample_layer(xs, mod_s, lw, cache_t, win_t, i, page_table)
        kv_p.append(kvt_p[:kv_rows].reshape(4, N_KV, HEAD_DIM, seq))
        win_p.append(kvt_p[kv_rows:, seq - w_keep:].reshape(2, N_KV, HEAD_DIM, w_keep))
        kv_s.append(kvs[:, :kv_rows].reshape(n_seq, 1, 4, N_KV, HEAD_DIM))
        win_new.append(kvt_s[kv_rows:].reshape(2, N_KV, HEAD_DIM, n_seq).transpose(3, 0, 1, 2)[..., None])
        sgv_s.append(v_s.reshape(n_seq, 1, D_SG))
    kv_prompt = jnp.stack(kv_p).transpose(0, 4, 1, 2, 3)[:, None]
    win_prompt = jnp.stack(win_p).transpose(0, 4, 1, 2, 3)[:, None]
    win_sample = jnp.concatenate([win_t[..., 1:], jnp.stack(win_new)], axis=-1).transpose(0, 1, 5, 2, 3, 4)
    return (xp[None], xs[:, None], kv_prompt, jnp.stack(kv_s), win_prompt, win_sample, jnp.stack(sgv_s))
```
